```python
import jax, jax.numpy as jnp
from jax import lax
import numpy as np

D_MODEL = 1024
BATCH = 4
SEQ = 8192
DEPTH = 1

HG_HEADS = 4
HG_DK = 128
HG_DV = 128
HG_KDIM = HG_HEADS * HG_DK
HG_WIDTH = HG_HEADS * HG_DV
HG_CHUNK = 64
SB_HEADS = 8
SB_DH = 64
SB_WIDTH = SB_HEADS * SB_DH
SB_BLOCK = 128
D_MIX = HG_WIDTH + SB_WIDTH
IN_SPLITS = (HG_KDIM, 2 * HG_KDIM, 2 * HG_KDIM + HG_WIDTH, 2 * HG_KDIM + 2 * HG_WIDTH,
             2 * HG_KDIM + 2 * HG_WIDTH + SB_WIDTH, 2 * HG_KDIM + 2 * HG_WIDTH + 2 * SB_WIDTH)
IN_COLS = 2 * HG_KDIM + 2 * HG_WIDTH + 3 * SB_WIDTH
N_EXPERTS = 32
TOP_K = 4
D_FF = 1024
MOE_BLOCK = 256
SWIGLU_LIMIT = 7.0
SWIGLU_ALPHA = 1.702
NORM_EPS = 1e-5

kernel_name = "hymba_hgrn2_stickbreaking_moe"


def rms_norm(x, gain):
    xf = x.astype(jnp.float32)
    var = jnp.mean(xf * xf, axis=-1, keepdims=True)
    return xf * lax.rsqrt(var + NORM_EPS) * gain.astype(jnp.float32)


def hgrn2_group(q_raw, f_raw, i_raw, g_raw, lb, gain):
    B, S, _ = q_raw.shape
    nc = S // HG_CHUNK
    q = jax.nn.silu(q_raw)
    f_sig = jax.nn.sigmoid(f_raw)
    logf = jnp.log(lb + (1.0 - lb) * f_sig)
    k = (1.0 - lb) * (1.0 - f_sig)

    def chunks(t, d):
        return t.reshape(B, nc, HG_CHUNK, HG_HEADS, d).transpose(1, 0, 3, 2, 4)

    causal = jnp.tril(jnp.ones((HG_CHUNK, HG_CHUNK), dtype=bool))

    def step(state, inp):
        qc, kc, gc, vc = inp
        G = jnp.cumsum(gc, axis=2)
        diff = G[:, :, :, None, :] - G[:, :, None, :, :]
        decay = jnp.exp(jnp.where(causal[:, :, None], diff, -jnp.inf))
        scores = jnp.einsum('bhtk,bhtsk,bhsk->bhts', qc, decay, kc)
        o = (jnp.einsum('bhts,bhsv->bhtv', scores, vc)
             + jnp.einsum('bhtk,bhkv->bhtv', qc * jnp.exp(G), state))
        G_last = G[:, :, -1:, :]
        new_state = (jnp.exp(G_last[:, :, 0, :])[..., None] * state
                     + jnp.einsum('bhsk,bhsv->bhkv', kc * jnp.exp(G_last - G), vc))
        return new_state, o

    state0 = jnp.zeros((B, HG_HEADS, HG_DK, HG_DV), jnp.float32)
    _, o = lax.scan(step, state0, (chunks(q, HG_DK), chunks(k, HG_DK), chunks(logf, HG_DK), chunks(i_raw, HG_DV)))
    o = o.transpose(1, 0, 3, 2, 4).reshape(B, S, HG_HEADS, HG_DV)
    o = rms_norm(o, gain.reshape(HG_HEADS, HG_DV)).reshape(B, S, HG_WIDTH)
    return o * jax.nn.silu(g_raw)


def stick_breaking_group(q_raw, k_raw, v_raw, gain):
    B, S, _ = q_raw.shape
    nqb = S // SB_BLOCK
    scale = SB_DH ** -0.5

    def heads(t):
        return t.reshape(B, S, SB_HEADS, SB_DH).transpose(0, 2, 1, 3)

    q, k, v = heads(q_raw), heads(k_raw), heads(v_raw)
    blk = jnp.arange(SB_BLOCK)

    def query_block(qi):
        t0 = qi * SB_BLOCK
        qb = lax.dynamic_slice_in_dim(q, t0, SB_BLOCK, axis=2)
        t_pos = t0 + blk

        def cond(c):
            return c[0] >= 0

        def body(c):
            j, acc, out = c
            s0 = j * SB_BLOCK
            kb = lax.dynamic_slice_in_dim(k, s0, SB_BLOCK, axis=2)
            vb = lax.dynamic_slice_in_dim(v, s0, SB_BLOCK, axis=2)
            mask = (s0 + blk)[None, :] < t_pos[:, None]
            z = jnp.einsum('bhtd,bhsd->bhts', qb, kb) * scale
            sp = jnp.where(mask, jax.nn.softplus(z), 0.0)
            later = lax.cumsum(sp, axis=3, reverse=True) - sp
            log_a = jax.nn.log_sigmoid(z) - later - acc[..., None]
            a = jnp.where(mask, jnp.exp(log_a), 0.0)
            out = out + jnp.einsum('bhts,bhsd->bhtd', a, vb)
            return j - 1, acc + jnp.sum(sp, axis=-1), out

        init = (qi, jnp.zeros((B, SB_HEADS, SB_BLOCK), jnp.float32),
                jnp.zeros((B, SB_HEADS, SB_BLOCK, SB_DH), jnp.float32))
        _, _, out = lax.while_loop(cond, body, init)
        return out

    o = lax.map(query_block, jnp.arange(nqb, dtype=jnp.int32))
    o = o.transpose(1, 0, 3, 2, 4).reshape(B, S, SB_HEADS, SB_DH)
    return rms_norm(o, gain.reshape(SB_HEADS, SB_DH)).reshape(B, S, SB_WIDTH)


def moe_ffn(u, w_router, b_router, w_gate, b_gate, w_up, b_up, w_down, b_down):
    N, D = u.shape
    NK = N * TOP_K
    logits = jnp.einsum('nd,de->ne', u, w_router) + b_router
    top_vals, top_idx = lax.top_k(logits, TOP_K)
    gates = jax.nn.softmax(top_vals.astype(jnp.float32), axis=-1)
    flat_e = top_idx.reshape(-1)
    flat_tok = jnp.repeat(jnp.arange(N, dtype=jnp.int32), TOP_K)
    order = jnp.argsort(flat_e)
    sorted_e = flat_e[order]
    sorted_tok = flat_tok[order]
    sorted_gate = gates.reshape(-1)[order]
    counts = jnp.zeros((N_EXPERTS,), jnp.int32).at[flat_e].add(1)
    padded = ((counts + MOE_BLOCK - 1) // MOE_BLOCK) * MOE_BLOCK
    start_sorted = jnp.cumsum(counts) - counts
    cum_pad = jnp.cumsum(padded)
    start_pad = cum_pad - padded
    dest = start_pad[sorted_e] + (jnp.arange(NK, dtype=jnp.int32) - start_sorted[sorted_e])
    P = ((NK + MOE_BLOCK - 1) // MOE_BLOCK) * MOE_BLOCK + N_EXPERTS * MOE_BLOCK
    nb = P // MOE_BLOCK
    block_start = jnp.arange(nb, dtype=jnp.int32) * MOE_BLOCK
    block_e = jnp.minimum(jnp.sum(cum_pad[None, :] <= block_start[:, None], axis=1), N_EXPERTS - 1)
    x_disp = jnp.zeros((P, D), u.dtype).at[dest].set(u[sorted_tok])

    def expert_block(args):
        xb, e = args
        hg = xb @ w_gate[e] + b_gate[e]
        hu = xb @ w_up[e] + b_up[e]
        hg = jnp.minimum(hg, SWIGLU_LIMIT)
        hu = jnp.clip(hu, -SWIGLU_LIMIT, SWIGLU_LIMIT)
        glu = hg * jax.nn.sigmoid(SWIGLU_ALPHA * hg)
        return ((hu + 1.0) * glu) @ w_down[e] + b_down[e]

    y_disp = lax.map(expert_block, (x_disp.reshape(nb, MOE_BLOCK, D), block_e)).reshape(P, D)
    y = y_disp[dest] * sorted_gate[:, None].astype(y_disp.dtype)
    return jax.ops.segment_sum(y, sorted_tok, num_segments=N)


def setup_inputs(seed: int = 0) -> dict:
    key = jax.random.key(seed)
    ks = jax.random.split(key, 18)
    f32 = jnp.float32
    nrm = lambda k, shape, s: jax.random.normal(k, shape, f32) * s
    gain = lambda k, shape: 1.0 + 0.02 * jax.random.normal(k, shape, f32)
    return {
        'x': jax.random.normal(ks[0], (BATCH, SEQ, D_MODEL), f32),
        'w_in': nrm(ks[1], (DEPTH, D_MODEL, IN_COLS), D_MODEL ** -0.5),
        'w_out': nrm(ks[2], (DEPTH, D_MIX, D_MODEL), D_MIX ** -0.5),
        'hg_lb_logits': nrm(ks[3], (DEPTH + 1, HG_KDIM), 0.5),
        'hg_norm_gain': gain(ks[4], (DEPTH, HG_WIDTH)),
        'sb_norm_gain': gain(ks[5], (DEPTH, SB_WIDTH)),
        'norm_mix_gain': gain(ks[6], (DEPTH, D_MODEL)),
        'norm_ffn_gain': gain(ks[7], (DEPTH, D_MODEL)),
        'w_router': nrm(ks[8], (DEPTH, D_MODEL, N_EXPERTS), D_MODEL ** -0.5),
        'b_router': nrm(ks[9], (DEPTH, N_EXPERTS), 0.01),
        'w_gate': nrm(ks[10], (DEPTH, N_EXPERTS, D_MODEL, D_FF), D_MODEL ** -0.5),
        'b_gate': nrm(ks[11], (DEPTH, N_EXPERTS, D_FF), 0.01),
        'w_up': nrm(ks[12], (DEPTH, N_EXPERTS, D_MODEL, D_FF), D_MODEL ** -0.5),
        'b_up': nrm(ks[13], (DEPTH, N_EXPERTS, D_FF), 0.01),
        'w_down': nrm(ks[14], (DEPTH, N_EXPERTS, D_FF, D_MODEL), D_FF ** -0.5),
        'b_down': nrm(ks[15], (DEPTH, N_EXPERTS, D_MODEL), 0.01),
        'norm_final_gain': gain(ks[16], (D_MODEL,)),
    }


def reference(x, w_in, w_out, hg_lb_logits, hg_norm_gain, sb_norm_gain, norm_mix_gain, norm_ffn_gain,
              w_router, b_router, w_gate, b_gate, w_up, b_up, w_down, b_down, norm_final_gain):
    B, S, D = x.shape
    lb_all = jnp.cumsum(jax.nn.softmax(hg_lb_logits.astype(jnp.float32), axis=0), axis=0)
    h = x.astype(jnp.float32)
    for l in range(DEPTH):
        u = rms_norm(h, norm_mix_gain[l])
        proj = jnp.einsum('bsd,dc->bsc', u, w_in[l]).astype(jnp.float32)
        hq, hf, hi, hg, sq, sk, sv = jnp.split(proj, IN_SPLITS, axis=-1)
        o_hg = hgrn2_group(hq, hf, hi, hg, lb_all[l], hg_norm_gain[l])
        o_sb = stick_breaking_group(sq, sk, sv, sb_norm_gain[l])
        mix = jnp.concatenate([o_hg, o_sb], axis=-1)
        h = h + jnp.einsum('bsc,cd->bsd', mix, w_out[l])
        u = rms_norm(h, norm_ffn_gain[l]).reshape(B * S, D)
        h = h + moe_ffn(u, w_router[l], b_router[l], w_gate[l], b_gate[l], w_up[l], b_up[l],
                        w_down[l], b_down[l]).reshape(B, S, D).astype(jnp.float32)
    return rms_norm(h, norm_final_gain).astype(x.dtype)
```

```python
import functools

import jax
import jax.numpy as jnp
from jax import lax
from jax.experimental import pallas as pl
from jax.experimental.pallas import tpu as pltpu

F32 = jnp.float32
BF16 = jnp.bfloat16
I32 = jnp.int32

NORM_EPS = 1e-5
HG_HEADS = 4
HG_DK = 128
HG_WIDTH = HG_HEADS * HG_DK
HG_CHUNK = 64
HG_LEVELS = (32, 16, 8)
HG_DIAG = 8
SB_HEADS = 8
SB_DH = 64
SB_WIDTH = SB_HEADS * SB_DH
SB_BLOCK = 128
SB_STATIC_BLOCKS = 3
SB_SKIP_THRESHOLD = 104.0
N_EXPERTS = 32
TOP_K = 4
MOE_BLOCK = 256
SWIGLU_LIMIT = 7.0
SWIGLU_ALPHA = 1.702
NEG_BIG = -1e30

V7X_VMEM_LIMIT_BYTES = 56 * 1024 * 1024


def _cparams(n_axes, vmem_bytes=None):
    return pltpu.CompilerParams(
        dimension_semantics=("arbitrary",) * n_axes,
        vmem_limit_bytes=vmem_bytes,
    )


def _sigmoid(x):
    return 1.0 / (1.0 + jnp.exp(-x))


def _split2(x):
    hi = x.astype(BF16)
    lo = (x - hi.astype(F32)).astype(BF16)
    return hi, lo


def _split3(x):
    hi = x.astype(BF16)
    r = x - hi.astype(F32)
    mid = r.astype(BF16)
    lo = (r - mid.astype(F32)).astype(BF16)
    return hi, mid, lo


def _dot(a, b):
    return jnp.dot(a, b, preferred_element_type=F32)


def _dot_nt(a, b):
    return lax.dot_general(a, b, (((1,), (1,)), ((), ())), preferred_element_type=F32)


def _dot_tn(a, b):
    return lax.dot_general(a, b, (((0,), (0,)), ((), ())), preferred_element_type=F32)


def _in_proj_kernel(x_ref, gain_ref, w_ref, lbl_ref,
                    q_ref, k_ref, lf_ref, v_ref, g_ref, sq_ref, sk_ref, sv_ref):
    x = x_ref[...]
    var = jnp.mean(x * x, axis=-1, keepdims=True)
    u = (x * lax.rsqrt(var + NORM_EPS) * gain_ref[...]).astype(BF16)

    lbl = lbl_ref[...]
    mx = jnp.max(lbl, axis=0, keepdims=True)
    ex = jnp.exp(lbl - mx)
    lb = ex[0:1, :] / jnp.sum(ex, axis=0, keepdims=True)

    def seg(i):
        return _dot(u, w_ref[:, i * HG_WIDTH:(i + 1) * HG_WIDTH])

    hq = seg(0)
    q_ref[...] = hq * _sigmoid(hq)
    f_sig = _sigmoid(seg(1))
    lf_ref[...] = jnp.log(lb + (1.0 - lb) * f_sig)
    k_ref[...] = (1.0 - lb) * (1.0 - f_sig)
    v_ref[...] = seg(2)
    hg = seg(3)
    g_ref[...] = hg * _sigmoid(hg)
    sq_ref[...] = (seg(4) * (SB_DH ** -0.5)).astype(BF16)
    sk_ref[...] = seg(5).astype(BF16)
    sv_ref[...] = seg(6).astype(BF16)


def _in_proj(x2, gain, w_in_bf, lb_logits, tm=512):
    n, d = x2.shape
    cols = w_in_bf.shape[1]
    row = lambda i: (i, 0)
    fix = lambda i: (0, 0)
    o_f32 = jax.ShapeDtypeStruct((n, HG_WIDTH), F32)
    o_bf = jax.ShapeDtypeStruct((n, SB_WIDTH), BF16)
    return pl.pallas_call(
        _in_proj_kernel,
        grid=(n // tm,),
        in_specs=[
            pl.BlockSpec((tm, d), row),
            pl.BlockSpec((1, d), fix),
            pl.BlockSpec((d, cols), fix),
            pl.BlockSpec(lb_logits.shape, fix),
        ],
        out_specs=[pl.BlockSpec((tm, HG_WIDTH), row)] * 8,
        out_shape=[o_f32] * 5 + [o_bf] * 3,
        compiler_params=_cparams(1, V7X_VMEM_LIMIT_BYTES),
        name="in_proj",
    )(x2, gain, w_in_bf, lb_logits)


def _hgrn2_consts():
    c = HG_CHUNK
    t = jnp.arange(c)[:, None]
    s = jnp.arange(c)[None, :]
    mats = [(s <= t)]
    for lv in HG_LEVELS:
        ref = (t // (2 * lv)) * (2 * lv) + lv - 1
        mats.append(s <= ref)
    return jnp.concatenate(mats, axis=0).astype(BF16)


def _hgrn2_kernel(q_ref, k_ref, lf_ref, v_ref, g_ref, gain_ref, cmat_ref, o_ref, st_ref, *, ts):
    c = HG_CHUNK
    w = HG_WIDTH

    @pl.when(pl.program_id(1) == 0)
    def _():
        st_ref[...] = jnp.zeros_like(st_ref)

    row_w = lax.broadcasted_iota(I32, (c, w), 0)
    row_c = lax.broadcasted_iota(I32, (c, c), 0)
    col_c = lax.broadcasted_iota(I32, (c, c), 1)
    row_d = lax.broadcasted_iota(I32, (HG_DIAG, w), 0)
    dk_bits = HG_DK.bit_length() - 1
    bd = ((lax.broadcasted_iota(I32, (w, w), 0) >> dk_bits)
          == (lax.broadcasted_iota(I32, (w, w), 1) >> dk_bits)).astype(BF16)
    cmat = cmat_ref[...]
    gain = gain_ref[...]

    def chunk(ci, carry):
        r0 = pl.multiple_of(ci * c, c)
        q = q_ref[0, pl.ds(r0, c), :]
        kk = k_ref[0, pl.ds(r0, c), :]
        lf = lf_ref[0, pl.ds(r0, c), :]
        v = v_ref[0, pl.ds(r0, c), :]
        g = g_ref[0, pl.ds(r0, c), :]

        lf_h, lf_m, lf_l = _split3(lf)
        gg = _dot(cmat, lf_h) + _dot(cmat, lf_m) + _dot(cmat, lf_l)
        G = gg[0:c]

        scores = [jnp.zeros((c, c), F32) for _ in range(HG_HEADS)]
        for li, lv in enumerate(HG_LEVELS):
            gref = gg[(li + 1) * c:(li + 2) * c]
            is_q = (row_w & (2 * lv - 1)) >= lv
            e = jnp.exp(jnp.where(is_q, G - gref, gref - G))
            ql = jnp.where(is_q, q * e, 0.0).astype(BF16)
            kl = jnp.where(is_q, 0.0, kk * e).astype(BF16)
            grp_bits = (2 * lv).bit_length() - 1
            same = (row_c >> grp_bits) == (col_c >> grp_bits)
            for h in range(HG_HEADS):
                sl = slice(h * HG_DK, (h + 1) * HG_DK)
                scores[h] = scores[h] + jnp.where(same, _dot_nt(ql[:, sl], kl[:, sl]), 0.0)

        tiles = []
        for b in range(c // HG_DIAG):
            rs = slice(b * HG_DIAG, (b + 1) * HG_DIAG)
            gb, qb, kb = G[rs], q[rs], kk[rs]
            for s in range(HG_DIAG):
                gs = jnp.broadcast_to(gb[s:s + 1, :], (HG_DIAG, w))
                ks = jnp.broadcast_to(kb[s:s + 1, :], (HG_DIAG, w))
                e = jnp.exp(jnp.where(row_d >= s, gb - gs, NEG_BIG))
                tiles.append(qb * e * ks)
        p_all = jnp.concatenate(tiles, axis=0)
        p_h, p_l = _split2(p_all)
        r_all = _dot(p_h, bd) + _dot(p_l, bd)
        o_blocks = []
        for b in range(c // HG_DIAG):
            vb = v[b * HG_DIAG:(b + 1) * HG_DIAG]
            ob = jnp.zeros((HG_DIAG, w), F32)
            for s in range(HG_DIAG):
                i0 = (b * HG_DIAG + s) * HG_DIAG
                vs = jnp.broadcast_to(vb[s:s + 1, :], (HG_DIAG, w))
                ob = ob + r_all[i0:i0 + HG_DIAG] * vs
            o_blocks.append(ob)
        o = jnp.concatenate(o_blocks, axis=0)

        qg = (q * jnp.exp(G)).astype(BF16)
        g_last = jnp.broadcast_to(G[c - 1:c, :], (c, w))
        kh = (kk * jnp.exp(g_last - G)).astype(BF16)
        dec = jnp.exp(G[c - 1:c, :])
        v_bf = v.astype(BF16)
        outs = []
        for h in range(HG_HEADS):
            sl = slice(h * HG_DK, (h + 1) * HG_DK)
            st = st_ref[h]
            oh = (o[:, sl]
                  + _dot(scores[h].astype(BF16), v_bf[:, sl])
                  + _dot_nt(qg[:, sl], st.astype(BF16)))
            st_ref[h] = st * dec[:, sl] + _dot_tn(v_bf[:, sl], kh[:, sl])
            var = jnp.mean(oh * oh, axis=-1, keepdims=True)
            outs.append(oh * lax.rsqrt(var + NORM_EPS))
        on = jnp.concatenate(outs, axis=-1) * gain * g
        o_ref[0, pl.ds(r0, c), :] = on.astype(o_ref.dtype)
        return carry

    lax.fori_loop(0, ts // c, chunk, 0)


def _hgrn2(q, k, lf, v, g, gain, ts=256):
    b, s, w = q.shape
    blk = lambda bi, si: (bi, si, 0)
    fix = lambda bi, si: (0, 0)
    cmat = _hgrn2_consts()
    return pl.pallas_call(
        functools.partial(_hgrn2_kernel, ts=ts),
        grid=(b, s // ts),
        in_specs=[pl.BlockSpec((1, ts, w), blk)] * 5 + [
            pl.BlockSpec((1, w), fix),
            pl.BlockSpec(cmat.shape, fix),
        ],
        out_specs=pl.BlockSpec((1, ts, w), blk),
        out_shape=jax.ShapeDtypeStruct((b, s, w), BF16),
        scratch_shapes=[pltpu.VMEM((HG_HEADS, HG_DK, HG_DK), F32)],
        compiler_params=_cparams(2, V7X_VMEM_LIMIT_BYTES),
        name="hgrn2",
    )(q, k, lf, v, g, gain, cmat)


def _sb_kernel(q_ref, k0_ref, k1_ref, k2_ref, v0_ref, v1_ref, v2_ref, gain_ref, kall_ref, vall_ref,
               o_ref, acc_ref, out_ref, kbuf_ref, vbuf_ref, sem_ref):
    tb = SB_BLOCK
    bi = pl.program_id(0)
    qi = pl.program_id(1)
    n_pairs = SB_HEADS // 2
    pair_w = 2 * SB_DH

    acc_ref[...] = jnp.zeros_like(acc_ref)
    out_ref[...] = jnp.zeros_like(out_ref)

    t_io = lax.broadcasted_iota(I32, (tb, tb), 0)
    s_io = lax.broadcasted_iota(I32, (tb, tb), 1)
    causal = s_io < t_io
    u_row = lax.broadcasted_iota(I32, (tb, 2 * tb), 0)
    u_col = lax.broadcasted_iota(I32, (tb, 2 * tb), 1)
    um = jnp.where(jnp.logical_or(u_col >= tb, u_row > u_col), 1.0, 0.0).astype(BF16)
    lane = lax.broadcasted_iota(I32, (tb, pair_w), 1)
    lo_half = lane < SB_DH

    def process(load_k, load_v, diag):
        mins = []
        for p in range(n_pairs):
            sl = slice(p * pair_w, (p + 1) * pair_w)
            q2 = q_ref[0, :, sl]
            k2 = load_k(sl)
            v2 = load_v(sl)
            zero = jnp.zeros_like(q2)
            o_pair = jnp.zeros((tb, pair_w), F32)
            for half in range(2):
                h = 2 * p + half
                keep = lo_half if half == 0 else jnp.logical_not(lo_half)
                z = _dot_nt(jnp.where(keep, q2, zero), k2)
                sp_full = jnp.maximum(z, 0.0) + jnp.log(1.0 + jnp.exp(-jnp.abs(z)))
                sp = jnp.where(causal, sp_full, 0.0) if diag else sp_full
                sp_h, sp_l = _split2(sp)
                lt = _dot(sp_h, um) + _dot(sp_l, um)
                acc = acc_ref[h]
                log_a = (z - sp_full) - lt[:, :tb] - acc
                a = jnp.exp(log_a)
                if diag:
                    a = jnp.where(causal, a, 0.0)
                o_pair = o_pair + _dot(a.astype(BF16), jnp.where(keep, v2, zero))
                acc_new = acc + lt[:, tb:]
                acc_ref[h] = acc_new
                mins.append(jnp.min(acc_new))
            out_ref[:, sl] += o_pair
        return functools.reduce(jnp.minimum, mins)

    def blocked(ref):
        return lambda sl: ref[0, :, sl]

    def whole(ref):
        return lambda sl: ref[:, sl]

    m0 = process(blocked(k0_ref), blocked(v0_ref), True)

    def step1():
        return process(blocked(k1_ref), blocked(v1_ref), False)

    m1 = lax.cond(qi >= 1, step1, lambda: m0)

    def step2():
        return process(blocked(k2_ref), blocked(v2_ref), False)

    m2 = lax.cond(jnp.logical_and(qi >= 2, m1 <= SB_SKIP_THRESHOLD), step2, lambda: m1)

    def cond(cr):
        j, m = cr
        return jnp.logical_and(j >= 0, m <= SB_SKIP_THRESHOLD)

    def body(cr):
        j, _ = cr
        r0 = pl.multiple_of(j * tb, tb)
        ck = pltpu.make_async_copy(kall_ref.at[bi, pl.ds(r0, tb), :], kbuf_ref, sem_ref.at[0])
        cv = pltpu.make_async_copy(vall_ref.at[bi, pl.ds(r0, tb), :], vbuf_ref, sem_ref.at[1])
        ck.start()
        cv.start()
        ck.wait()
        cv.wait()
        return j - 1, process(whole(kbuf_ref), whole(vbuf_ref), False)

    lax.while_loop(cond, body, (qi - SB_STATIC_BLOCKS, m2))

    o = out_ref[...]
    wd = SB_WIDTH
    dh_bits = SB_DH.bit_length() - 1
    bd = ((lax.broadcasted_iota(I32, (wd, wd), 0) >> dh_bits)
          == (lax.broadcasted_iota(I32, (wd, wd), 1) >> dh_bits)).astype(BF16)
    sq_h, sq_m, sq_l = _split3(o * o)
    var = (_dot(sq_h, bd) + _dot(sq_m, bd) + _dot(sq_l, bd)) * (1.0 / SB_DH)
    o_ref[0] = (o * lax.rsqrt(var + NORM_EPS) * gain_ref[...]).astype(o_ref.dtype)


def _stickbreak(sq, sk, sv, gain):
    b, s, w = sq.shape
    tb = SB_BLOCK
    cur = lambda bi, qi: (bi, qi, 0)
    prev1 = lambda bi, qi: (bi, jnp.maximum(qi - 1, 0), 0)
    prev2 = lambda bi, qi: (bi, jnp.maximum(qi - 2, 0), 0)
    fix = lambda bi, qi: (0, 0)
    blk = (1, tb, w)
    return pl.pallas_call(
        _sb_kernel,
        grid=(b, s // tb),
        in_specs=[
            pl.BlockSpec(blk, cur),
            pl.BlockSpec(blk, cur), pl.BlockSpec(blk, prev1), pl.BlockSpec(blk, prev2),
            pl.BlockSpec(blk, cur), pl.BlockSpec(blk, prev1), pl.BlockSpec(blk, prev2),
            pl.BlockSpec((1, w), fix),
            pl.BlockSpec(memory_space=pl.ANY),
            pl.BlockSpec(memory_space=pl.ANY),
        ],
        out_specs=pl.BlockSpec(blk, cur),
        out_shape=jax.ShapeDtypeStruct((b, s, w), BF16),
        scratch_shapes=[
            pltpu.VMEM((SB_HEADS, tb, tb), F32),
            pltpu.VMEM((tb, w), F32),
            pltpu.VMEM((tb, w), BF16),
            pltpu.VMEM((tb, w), BF16),
            pltpu.SemaphoreType.DMA((2,)),
        ],
        compiler_params=_cparams(2, V7X_VMEM_LIMIT_BYTES),
        name="stickbreak",
    )(sq, sk, sk, sk, sv, sv, sv, gain, sk, sv)


def _mix_router_kernel(ohg_ref, osb_ref, x_ref, wout_ref, gffn_ref, wrt_ref, br_ref,
                       h_ref, u_ref, idx_ref, rank_ref, gate_ref, cnt_ref, carry_ref, *, tm):
    @pl.when(pl.program_id(0) == 0)
    def _():
        carry_ref[...] = jnp.zeros_like(carry_ref)

    h = (x_ref[...]
         + _dot(ohg_ref[...], wout_ref[0:HG_WIDTH, :])
         + _dot(osb_ref[...], wout_ref[HG_WIDTH:HG_WIDTH + SB_WIDTH, :]))
    h_ref[...] = h
    var = jnp.mean(h * h, axis=-1, keepdims=True)
    u = h * lax.rsqrt(var + NORM_EPS) * gffn_ref[...]
    u_ref[...] = u

    u_h, u_l = _split2(u)
    w_h, w_l = _split2(wrt_ref[...])
    logits = _dot_nt(w_h, u_h) + _dot_nt(w_h, u_l) + _dot_nt(w_l, u_h) + br_ref[...]

    e_io = lax.broadcasted_iota(I32, (N_EXPERTS, tm), 0).astype(F32)
    vals = logits
    member = jnp.zeros((N_EXPERTS, tm), F32)
    top_v, top_i = [], []
    for _ in range(TOP_K):
        m = jnp.max(vals, axis=0, keepdims=True)
        idx = jnp.min(jnp.where(vals == m, e_io, float(N_EXPERTS)), axis=0, keepdims=True)
        sel = e_io == idx
        top_v.append(m)
        top_i.append(idx)
        member = member + jnp.where(sel, 1.0, 0.0)
        vals = jnp.where(sel, -jnp.inf, vals)

    ex = [jnp.exp(tv - top_v[0]) for tv in top_v]
    den = ex[0] + ex[1] + ex[2] + ex[3]
    gates = [e / den for e in ex]

    n_io = lax.broadcasted_iota(I32, (tm, tm), 0)
    m_io = lax.broadcasted_iota(I32, (tm, tm), 1)
    before = jnp.where(n_io < m_io, 1.0, 0.0).astype(BF16)
    carry = carry_ref[...]
    cexcl = _dot(member.astype(BF16), before) + carry
    carry_new = carry + jnp.sum(member, axis=1, keepdims=True)
    carry_ref[...] = carry_new
    cnt_ref[...] = jnp.broadcast_to(carry_new, cnt_ref.shape).astype(I32)

    idx_ref[...] = jnp.concatenate(top_i, axis=0).astype(I32)
    ranks = [jnp.sum(jnp.where(e_io == ti, cexcl, 0.0), axis=0, keepdims=True) for ti in top_i]
    rank_ref[...] = jnp.concatenate(ranks, axis=0).astype(I32)

    r_io = lax.broadcasted_iota(I32, (128, tm), 0)
    gfull = jnp.zeros((128, tm), F32)
    for kk_, gk in enumerate(gates):
        gfull = jnp.where(r_io == kk_, jnp.broadcast_to(gk, (128, tm)), gfull)
    gate_ref[...] = gfull.T


def _mix_router(ohg, osb, x2, w_out_bf, g_ffn, w_router_t, b_router_col, tm=256):
    n, d = x2.shape
    row = lambda i: (i, 0)
    col = lambda i: (0, i)
    fix = lambda i: (0, 0)
    return pl.pallas_call(
        functools.partial(_mix_router_kernel, tm=tm),
        grid=(n // tm,),
        in_specs=[
            pl.BlockSpec((tm, HG_WIDTH), row),
            pl.BlockSpec((tm, SB_WIDTH), row),
            pl.BlockSpec((tm, d), row),
            pl.BlockSpec(w_out_bf.shape, fix),
            pl.BlockSpec((1, d), fix),
            pl.BlockSpec(w_router_t.shape, fix),
            pl.BlockSpec(b_router_col.shape, fix),
        ],
        out_specs=[
            pl.BlockSpec((tm, d), row),
            pl.BlockSpec((tm, d), row),
            pl.BlockSpec((TOP_K, tm), col),
            pl.BlockSpec((TOP_K, tm), col),
            pl.BlockSpec((tm, 128), row),
            pl.BlockSpec((N_EXPERTS, 128), fix),
        ],
        out_shape=[
            jax.ShapeDtypeStruct((n, d), F32),
            jax.ShapeDtypeStruct((n, d), F32),
            jax.ShapeDtypeStruct((TOP_K, n), I32),
            jax.ShapeDtypeStruct((TOP_K, n), I32),
            jax.ShapeDtypeStruct((n, 128), F32),
            jax.ShapeDtypeStruct((N_EXPERTS, 128), I32),
        ],
        scratch_shapes=[pltpu.VMEM((N_EXPERTS, 1), F32)],
        compiler_params=_cparams(1, V7X_VMEM_LIMIT_BYTES),
        name="mix_router",
    )(ohg, osb, x2, w_out_bf, g_ffn, w_router_t, b_router_col)


def _dest_kernel(start_ref, idx_ref, rank_ref, dest_ref):
    idx = idx_ref[...]
    base = jnp.zeros(idx.shape, I32)
    for e in range(N_EXPERTS):
        base = jnp.where(idx == e, start_ref[e], base)
    dest_ref[...] = base + rank_ref[...]


def _dest(start_pad, idx_t, rank_t):
    return pl.pallas_call(
        _dest_kernel,
        in_specs=[
            pl.BlockSpec(memory_space=pltpu.SMEM),
            pl.BlockSpec(memory_space=pltpu.VMEM),
            pl.BlockSpec(memory_space=pltpu.VMEM),
        ],
        out_specs=pl.BlockSpec(memory_space=pltpu.VMEM),
        out_shape=jax.ShapeDtypeStruct(idx_t.shape, I32),
        name="dest",
    )(start_pad, idx_t, rank_t)


def _dispatch_kernel(cnt_ref, start_ref, nused_ref, dest_ref, u_ref, zero_ref, xd_ref, sem_ref, zsem_ref,
                     *, tm):
    i = pl.program_id(0)
    n_steps = pl.num_programs(0)

    def row_copy(n, k):
        return pltpu.make_async_copy(
            u_ref.at[pl.ds(i * tm + n, 1)], xd_ref.at[pl.ds(dest_ref[k, n], 1)], sem_ref)

    def wait_rows():
        pltpu.make_async_copy(
            u_ref.at[pl.ds(0, TOP_K * tm)], xd_ref.at[pl.ds(0, TOP_K * tm)], sem_ref).wait()

    @pl.when(i > 0)
    def _():
        wait_rows()

    def issue(n, carry):
        for k in range(TOP_K):
            row_copy(n, k).start()
        return carry

    lax.fori_loop(0, tm, issue, 0, unroll=8)

    @pl.when(i == n_steps - 1)
    def _():
        wait_rows()

    @pl.when(i == 0)
    def _():
        def per_expert(e, carry):
            cnt = cnt_ref[e]
            pad_end = ((cnt + MOE_BLOCK - 1) // MOE_BLOCK) * MOE_BLOCK
            base = start_ref[e]

            def zcopy(r):
                return pltpu.make_async_copy(
                    zero_ref.at[pl.ds(0, 1)], xd_ref.at[pl.ds(base + r, 1)], zsem_ref)

            def start(r, c2):
                zcopy(r).start()
                return c2

            def wait(r, c2):
                zcopy(r).wait()
                return c2

            lax.fori_loop(cnt, pad_end, start, 0)
            lax.fori_loop(cnt, pad_end, wait, 0)
            return carry

        lax.fori_loop(0, N_EXPERTS, per_expert, 0)

        def zblock(blk):
            r0 = pl.multiple_of(blk * MOE_BLOCK, MOE_BLOCK)
            return pltpu.make_async_copy(zero_ref, xd_ref.at[pl.ds(r0, MOE_BLOCK)], zsem_ref)

        def bstart(blk, c2):
            zblock(blk).start()
            return c2

        def bwait(blk, c2):
            zblock(blk).wait()
            return c2

        n_blocks = xd_ref.shape[0] // MOE_BLOCK
        lax.fori_loop(nused_ref[0], n_blocks, bstart, 0)
        lax.fori_loop(nused_ref[0], n_blocks, bwait, 0)


def _dispatch(counts, start_pad, n_used, dest_t, u, n_slots, tm=256):
    n, d = u.shape
    zero_blk = jnp.zeros((MOE_BLOCK, d), u.dtype)
    grid_spec = pltpu.PrefetchScalarGridSpec(
        num_scalar_prefetch=3,
        grid=(n // tm,),
        in_specs=[
            pl.BlockSpec((TOP_K, tm), lambda i, c, s, nu: (0, i), memory_space=pltpu.SMEM),
            pl.BlockSpec(memory_space=pl.ANY),
            pl.BlockSpec(memory_space=pl.ANY),
        ],
        out_specs=pl.BlockSpec(memory_space=pl.ANY),
        scratch_shapes=[pltpu.SemaphoreType.DMA(()), pltpu.SemaphoreType.DMA(())],
    )
    return pl.pallas_call(
        functools.partial(_dispatch_kernel, tm=tm),
        grid_spec=grid_spec,
        out_shape=jax.ShapeDtypeStruct((n_slots, d), u.dtype),
        compiler_params=_cparams(1),
        name="dispatch",
    )(counts, start_pad, n_used, dest_t, u, zero_blk)


def _moe_kernel(be_ref, xb_ref, nused_ref, x_ref, wg_ref, bg_ref, wu_ref, bu_ref, wd_ref, bd_ref, y_ref):
    @pl.when(pl.program_id(0) < nused_ref[0])
    def _():
        x = x_ref[...].astype(BF16)
        hg = _dot(x, wg_ref[0]) + bg_ref[0]
        hu = _dot(x, wu_ref[0]) + bu_ref[0]
        hg = jnp.minimum(hg, SWIGLU_LIMIT)
        hu = jnp.clip(hu, -SWIGLU_LIMIT, SWIGLU_LIMIT)
        glu = hg * _sigmoid(SWIGLU_ALPHA * hg)
        act = ((hu + 1.0) * glu).astype(BF16)
        y_ref[...] = _dot(act, wd_ref[0]) + bd_ref[0]

    @pl.when(pl.program_id(0) >= nused_ref[0])
    def _():
        y_ref[...] = jnp.zeros_like(y_ref)


def _moe(block_e, x_blk, n_used, x_disp, wg, bg, wu, bu, wd, bd):
    p, d = x_disp.shape
    f = wg.shape[2]
    nb = p // MOE_BLOCK
    xmap = lambda i, be, xb, nu: (xb[i], 0)
    wmap = lambda i, be, xb, nu: (be[i], 0, 0)
    grid_spec = pltpu.PrefetchScalarGridSpec(
        num_scalar_prefetch=3,
        grid=(nb,),
        in_specs=[
            pl.BlockSpec((MOE_BLOCK, d), xmap),
            pl.BlockSpec((1, d, f), wmap),
            pl.BlockSpec((1, 1, f), wmap),
            pl.BlockSpec((1, d, f), wmap),
            pl.BlockSpec((1, 1, f), wmap),
            pl.BlockSpec((1, f, d), wmap),
            pl.BlockSpec((1, 1, d), wmap),
        ],
        out_specs=pl.BlockSpec((MOE_BLOCK, d), lambda i, be, xb, nu: (i, 0)),
    )
    return pl.pallas_call(
        _moe_kernel,
        grid_spec=grid_spec,
        out_shape=jax.ShapeDtypeStruct((p, d), F32),
        compiler_params=_cparams(1, V7X_VMEM_LIMIT_BYTES),
        name="moe",
    )(block_e, x_blk, n_used, x_disp, wg, bg, wu, bu, wd, bd)


def _combine_kernel(dest_ref, y_ref, h_ref, gate_ref, gain_ref, o_ref, ybuf_ref, sem_ref, *, tm):
    def issue(n, carry):
        for k in range(TOP_K):
            pltpu.make_async_copy(
                y_ref.at[pl.ds(dest_ref[k, n], 1)], ybuf_ref.at[pl.ds(k * tm + n, 1)], sem_ref).start()
        return carry

    lax.fori_loop(0, tm, issue, 0, unroll=8)
    pltpu.make_async_copy(y_ref.at[pl.ds(0, TOP_K * tm)], ybuf_ref, sem_ref).wait()

    gate = gate_ref[...]
    acc = h_ref[...]
    for k in range(TOP_K):
        acc = acc + gate[:, k:k + 1] * ybuf_ref[k * tm:(k + 1) * tm, :]
    var = jnp.mean(acc * acc, axis=-1, keepdims=True)
    o_ref[...] = (acc * lax.rsqrt(var + NORM_EPS) * gain_ref[...]).astype(o_ref.dtype)


def _combine(dest_t, y_disp, h, gates, gain, tm=256):
    n, d = h.shape
    row = lambda i: (i, 0)
    fix = lambda i: (0, 0)
    return pl.pallas_call(
        functools.partial(_combine_kernel, tm=tm),
        grid=(n // tm,),
        in_specs=[
            pl.BlockSpec((TOP_K, tm), lambda i: (0, i), memory_space=pltpu.SMEM),
            pl.BlockSpec(memory_space=pl.ANY),
            pl.BlockSpec((tm, d), row),
            pl.BlockSpec((tm, 128), row),
            pl.BlockSpec((1, d), fix),
        ],
        out_specs=pl.BlockSpec((tm, d), row),
        out_shape=jax.ShapeDtypeStruct((n, d), F32),
        scratch_shapes=[pltpu.VMEM((TOP_K * tm, d), F32), pltpu.SemaphoreType.DMA(())],
        compiler_params=_cparams(1, V7X_VMEM_LIMIT_BYTES),
        name="combine",
    )(dest_t, y_disp, h, gates, gain)


def _routing_tables(counts, n_pairs):
    padded = ((counts + MOE_BLOCK - 1) // MOE_BLOCK) * MOE_BLOCK
    cum_pad = jnp.cumsum(padded)
    start_pad = (cum_pad - padded).astype(I32)
    n_slots = ((n_pairs + MOE_BLOCK - 1) // MOE_BLOCK) * MOE_BLOCK + N_EXPERTS * MOE_BLOCK
    nb = n_slots // MOE_BLOCK
    n_used = (cum_pad[-1] // MOE_BLOCK).astype(I32)
    blk = jnp.arange(nb, dtype=I32)
    x_blk = jnp.minimum(blk, n_used - 1)
    block_start = x_blk * MOE_BLOCK
    block_e = jnp.minimum(jnp.sum(cum_pad[None, :] <= block_start[:, None], axis=1), N_EXPERTS - 1)
    return start_pad, block_e.astype(I32), x_blk, n_used.reshape(1), n_slots


def kernel(x, w_in, w_out, hg_lb_logits, hg_norm_gain, sb_norm_gain, norm_mix_gain, norm_ffn_gain,
           w_router, b_router, w_gate, b_gate, w_up, b_up, w_down, b_down, norm_final_gain):
    b, s, d = x.shape
    n = b * s
    f = w_gate.shape[-1]
    assert w_in.shape[0] == 1 and hg_lb_logits.shape[0] == 2, "single-layer trunk only"
    x2 = x.reshape(n, d).astype(F32)
    r3 = lambda a: a.reshape(b, s, a.shape[-1])

    q, k, lf, v, g, sq, sk, sv = _in_proj(
        x2, norm_mix_gain[0].reshape(1, d), w_in[0].astype(BF16), hg_lb_logits.astype(F32))
    o_hg = _hgrn2(r3(q), r3(k), r3(lf), r3(v), r3(g), hg_norm_gain[0].reshape(1, HG_WIDTH))
    o_sb = _stickbreak(r3(sq), r3(sk), r3(sv), sb_norm_gain[0].reshape(1, SB_WIDTH))
    h_mid, u, idx_t, rank_t, gates, cnt = _mix_router(
        o_hg.reshape(n, HG_WIDTH), o_sb.reshape(n, SB_WIDTH), x2, w_out[0].astype(BF16),
        norm_ffn_gain[0].reshape(1, d), w_router[0].T.astype(F32),
        b_router[0].reshape(N_EXPERTS, 1).astype(F32))
    counts = cnt[:, 0]
    start_pad, block_e, x_blk, n_used, n_slots = _routing_tables(counts, n * TOP_K)
    dest_t = _dest(start_pad, idx_t, rank_t)
    x_disp = _dispatch(counts, start_pad, n_used, dest_t, u, n_slots)
    y_disp = _moe(block_e, x_blk, n_used, x_disp,
                  w_gate[0].astype(BF16), b_gate[0].reshape(N_EXPERTS, 1, f),
                  w_up[0].astype(BF16), b_up[0].reshape(N_EXPERTS, 1, f),
                  w_down[0].astype(BF16), b_down[0].reshape(N_EXPERTS, 1, d))
    out = _combine(dest_t, y_disp, h_mid, gates, norm_final_gain.reshape(1, d))
    return out.reshape(b, s, d).astype(x.dtype)
```

```python
import functools

import jax
import jax.numpy as jnp
from jax import lax
from jax.experimental import pallas as pl
from jax.experimental.pallas import tpu as pltpu

F32 = jnp.float32
BF16 = jnp.bfloat16
I32 = jnp.int32

NORM_EPS = 1e-5
HG_HEADS = 4
HG_DK = 128
HG_WIDTH = HG_HEADS * HG_DK
HG_CHUNK = 64
HG_LEVELS = (32, 16, 8)
HG_DIAG = 8
SB_HEADS = 8
SB_DH = 64
SB_WIDTH = SB_HEADS * SB_DH
SB_BLOCK = 128
SB_STATIC_BLOCKS = 3
SB_SKIP_THRESHOLD = 104.0
N_EXPERTS = 32
TOP_K = 4
MOE_BLOCK = 256
SWIGLU_LIMIT = 7.0
SWIGLU_ALPHA = 1.702
NEG_BIG = -1e30

V7X_VMEM_LIMIT_BYTES = 56 * 1024 * 1024
ROW_TILE = 8
LANES = 128
D_MODEL = ROW_TILE * LANES


def _cparams(n_axes, vmem_bytes=None):
    return pltpu.CompilerParams(
        dimension_semantics=("arbitrary",) * n_axes,
        vmem_limit_bytes=vmem_bytes,
    )


def _sigmoid(x):
    return 1.0 / (1.0 + jnp.exp(-x))


def _split2(x):
    hi = x.astype(BF16)
    lo = (x - hi.astype(F32)).astype(BF16)
    return hi, lo


def _split3(x):
    hi = x.astype(BF16)
    r = x - hi.astype(F32)
    mid = r.astype(BF16)
    lo = (r - mid.astype(F32)).astype(BF16)
    return hi, mid, lo


def _dot(a, b):
    return jnp.dot(a, b, preferred_element_type=F32)


def _store_row_tiles(ref, x, base=0):
    rows = x.shape[0]
    for j in range(ROW_TILE):
        ref[pl.ds(base * ROW_TILE + j, rows, stride=ROW_TILE), :] = x[:, j * LANES:(j + 1) * LANES]


def _load_row_tiles(ref, rows, base=0, chunk=None):
    if chunk is not None:
        return ref[pl.ds(base * ROW_TILE + chunk, rows, stride=ROW_TILE), :]
    return jnp.concatenate(
        [ref[pl.ds(base * ROW_TILE + j, rows, stride=ROW_TILE), :] for j in range(ROW_TILE)], axis=-1)


def _dot_nt(a, b):
    return lax.dot_general(a, b, (((1,), (1,)), ((), ())), preferred_element_type=F32)


def _dot_tn(a, b):
    return lax.dot_general(a, b, (((0,), (0,)), ((), ())), preferred_element_type=F32)


def _in_proj_kernel(x_ref, gain_ref, w_ref, lbl_ref,
                    q_ref, k_ref, lf_ref, v_ref, g_ref, sq_ref, sk_ref, sv_ref):
    x = x_ref[...]
    var = jnp.mean(x * x, axis=-1, keepdims=True)
    u = (x * lax.rsqrt(var + NORM_EPS) * gain_ref[...]).astype(BF16)

    lbl = lbl_ref[...]
    mx = jnp.max(lbl, axis=0, keepdims=True)
    ex = jnp.exp(lbl - mx)
    lb = ex[0:1, :] / jnp.sum(ex, axis=0, keepdims=True)

    def seg(i):
        return _dot(u, w_ref[:, i * HG_WIDTH:(i + 1) * HG_WIDTH])

    hq = seg(0)
    q_ref[...] = hq * _sigmoid(hq)
    f_sig = _sigmoid(seg(1))
    lf_ref[...] = jnp.log(lb + (1.0 - lb) * f_sig)
    k_ref[...] = (1.0 - lb) * (1.0 - f_sig)
    v_ref[...] = seg(2)
    hg = seg(3)
    g_ref[...] = hg * _sigmoid(hg)
    sq_ref[...] = (seg(4) * (SB_DH ** -0.5)).astype(BF16)
    sk_ref[...] = seg(5).astype(BF16)
    sv_ref[...] = seg(6).astype(BF16)


def _in_proj(x2, gain, w_in_bf, lb_logits, tm=512):
    n, d = x2.shape
    cols = w_in_bf.shape[1]
    row = lambda i: (i, 0)
    fix = lambda i: (0, 0)
    o_f32 = jax.ShapeDtypeStruct((n, HG_WIDTH), F32)
    o_bf = jax.ShapeDtypeStruct((n, SB_WIDTH), BF16)
    return pl.pallas_call(
        _in_proj_kernel,
        grid=(n // tm,),
        in_specs=[
            pl.BlockSpec((tm, d), row),
            pl.BlockSpec((1, d), fix),
            pl.BlockSpec((d, cols), fix),
            pl.BlockSpec(lb_logits.shape, fix),
        ],
        out_specs=[pl.BlockSpec((tm, HG_WIDTH), row)] * 8,
        out_shape=[o_f32] * 5 + [o_bf] * 3,
        compiler_params=_cparams(1, V7X_VMEM_LIMIT_BYTES),
        name="in_proj",
    )(x2, gain, w_in_bf, lb_logits)


def _hgrn2_consts():
    c = HG_CHUNK
    t = jnp.arange(c)[:, None]
    s = jnp.arange(c)[None, :]
    mats = [(s <= t)]
    for lv in HG_LEVELS:
        ref = (t // (2 * lv)) * (2 * lv) + lv - 1
        mats.append(s <= ref)
    return jnp.concatenate(mats, axis=0).astype(BF16)


def _hgrn2_kernel(q_ref, k_ref, lf_ref, v_ref, g_ref, gain_ref, cmat_ref, o_ref, st_ref, *, ts):
    c = HG_CHUNK
    w = HG_WIDTH

    @pl.when(pl.program_id(1) == 0)
    def _():
        st_ref[...] = jnp.zeros_like(st_ref)

    row_w = lax.broadcasted_iota(I32, (c, w), 0)
    row_c = lax.broadcasted_iota(I32, (c, c), 0)
    col_c = lax.broadcasted_iota(I32, (c, c), 1)
    row_d = lax.broadcasted_iota(I32, (HG_DIAG, w), 0)
    dk_bits = HG_DK.bit_length() - 1
    bd = ((lax.broadcasted_iota(I32, (w, w), 0) >> dk_bits)
          == (lax.broadcasted_iota(I32, (w, w), 1) >> dk_bits)).astype(BF16)
    cmat = cmat_ref[...]
    gain = gain_ref[...]

    def chunk(ci, carry):
        r0 = pl.multiple_of(ci * c, c)
        q = q_ref[0, pl.ds(r0, c), :]
        kk = k_ref[0, pl.ds(r0, c), :]
        lf = lf_ref[0, pl.ds(r0, c), :]
        v = v_ref[0, pl.ds(r0, c), :]
        g = g_ref[0, pl.ds(r0, c), :]

        lf_h, lf_l = _split2(lf)
        gg = _dot(cmat, lf_h) + _dot(cmat, lf_l)
        G = gg[0:c]

        scores = [jnp.zeros((c, c), F32) for _ in range(HG_HEADS)]
        for li, lv in enumerate(HG_LEVELS):
            gref = gg[(li + 1) * c:(li + 2) * c]
            is_q = (row_w & (2 * lv - 1)) >= lv
            e = jnp.exp(jnp.where(is_q, G - gref, gref - G))
            ql = jnp.where(is_q, q * e, 0.0).astype(BF16)
            kl = jnp.where(is_q, 0.0, kk * e).astype(BF16)
            grp_bits = (2 * lv).bit_length() - 1
            same = (row_c >> grp_bits) == (col_c >> grp_bits)
            for h in range(HG_HEADS):
                sl = slice(h * HG_DK, (h + 1) * HG_DK)
                scores[h] = scores[h] + jnp.where(same, _dot_nt(ql[:, sl], kl[:, sl]), 0.0)

        tiles = []
        for b in range(c // HG_DIAG):
            rs = slice(b * HG_DIAG, (b + 1) * HG_DIAG)
            gb, qb, kb = G[rs], q[rs], kk[rs]
            for s in range(HG_DIAG):
                gs = jnp.broadcast_to(gb[s:s + 1, :], (HG_DIAG, w))
                ks = jnp.broadcast_to(kb[s:s + 1, :], (HG_DIAG, w))
                e = jnp.exp(jnp.where(row_d >= s, gb - gs, NEG_BIG))
                tiles.append(qb * e * ks)
        p_all = jnp.concatenate(tiles, axis=0)
        r_all = _dot(p_all.astype(BF16), bd)
        o_blocks = []
        for b in range(c // HG_DIAG):
            vb = v[b * HG_DIAG:(b + 1) * HG_DIAG]
            ob = jnp.zeros((HG_DIAG, w), F32)
            for s in range(HG_DIAG):
                i0 = (b * HG_DIAG + s) * HG_DIAG
                vs = jnp.broadcast_to(vb[s:s + 1, :], (HG_DIAG, w))
                ob = ob + r_all[i0:i0 + HG_DIAG] * vs
            o_blocks.append(ob)
        o = jnp.concatenate(o_blocks, axis=0)

        qg = (q * jnp.exp(G)).astype(BF16)
        g_last = jnp.broadcast_to(G[c - 1:c, :], (c, w))
        kh = (kk * jnp.exp(g_last - G)).astype(BF16)
        dec = jnp.exp(G[c - 1:c, :])
        v_bf = v.astype(BF16)
        outs = []
        for h in range(HG_HEADS):
            sl = slice(h * HG_DK, (h + 1) * HG_DK)
            st = st_ref[h]
            oh = (o[:, sl]
                  + _dot(scores[h].astype(BF16), v_bf[:, sl])
                  + _dot_nt(qg[:, sl], st.astype(BF16)))
            st_ref[h] = st * dec[:, sl] + _dot_tn(v_bf[:, sl], kh[:, sl])
            var = jnp.mean(oh * oh, axis=-1, keepdims=True)
            outs.append(oh * lax.rsqrt(var + NORM_EPS))
        on = jnp.concatenate(outs, axis=-1) * gain * g
        o_ref[0, pl.ds(r0, c), :] = on.astype(o_ref.dtype)
        return carry

    lax.fori_loop(0, ts // c, chunk, 0)


def _hgrn2(q, k, lf, v, g, gain, ts=256):
    b, s, w = q.shape
    blk = lambda bi, si: (bi, si, 0)
    fix = lambda bi, si: (0, 0)
    cmat = _hgrn2_consts()
    return pl.pallas_call(
        functools.partial(_hgrn2_kernel, ts=ts),
        grid=(b, s // ts),
        in_specs=[pl.BlockSpec((1, ts, w), blk)] * 5 + [
            pl.BlockSpec((1, w), fix),
            pl.BlockSpec(cmat.shape, fix),
        ],
        out_specs=pl.BlockSpec((1, ts, w), blk),
        out_shape=jax.ShapeDtypeStruct((b, s, w), BF16),
        scratch_shapes=[pltpu.VMEM((HG_HEADS, HG_DK, HG_DK), F32)],
        compiler_params=_cparams(2, V7X_VMEM_LIMIT_BYTES),
        name="hgrn2",
    )(q, k, lf, v, g, gain, cmat)


def _sb_kernel(q_ref, k0_ref, k1_ref, k2_ref, v0_ref, v1_ref, v2_ref, gain_ref, kall_ref, vall_ref,
               o_ref, acc_ref, out_ref, kbuf_ref, vbuf_ref, sem_ref):
    tb = SB_BLOCK
    bi = pl.program_id(0)
    qi = pl.program_id(1)
    n_pairs = SB_HEADS // 2
    pair_w = 2 * SB_DH

    acc_ref[...] = jnp.zeros_like(acc_ref)
    out_ref[...] = jnp.zeros_like(out_ref)

    t_io = lax.broadcasted_iota(I32, (SB_HEADS * tb, tb), 0) & (tb - 1)
    s_io = lax.broadcasted_iota(I32, (SB_HEADS * tb, tb), 1)
    causal = s_io < t_io
    u_row = lax.broadcasted_iota(I32, (tb, 2 * tb), 0)
    u_col = lax.broadcasted_iota(I32, (tb, 2 * tb), 1)
    um = jnp.where(jnp.logical_or(u_col >= tb, u_row > u_col), 1.0, 0.0).astype(BF16)
    lane = lax.broadcasted_iota(I32, (tb, pair_w), 1)
    lo_half = lane < SB_DH
    keeps = (lo_half, jnp.logical_not(lo_half))

    def process(load_k, load_v, diag):
        zs = []
        for p in range(n_pairs):
            sl = slice(p * pair_w, (p + 1) * pair_w)
            q2 = q_ref[0, :, sl]
            k2 = load_k(sl)
            zero = jnp.zeros_like(q2)
            for half in range(2):
                zs.append(_dot_nt(jnp.where(keeps[half], q2, zero), k2))
        z = jnp.concatenate(zs, axis=0)
        sp_full = jnp.maximum(z, 0.0) + jnp.log(1.0 + jnp.exp(-jnp.abs(z)))
        sp = jnp.where(causal, sp_full, 0.0) if diag else sp_full
        sp_h, sp_l = _split2(sp)
        lt = _dot(sp_h, um) + _dot(sp_l, um)
        acc = acc_ref[...]
        a = jnp.exp((z - sp_full) - lt[:, :tb] - acc)
        if diag:
            a = jnp.where(causal, a, 0.0)
        a = a.astype(BF16)
        acc_new = acc + lt[:, tb:]
        acc_ref[...] = acc_new
        for p in range(n_pairs):
            sl = slice(p * pair_w, (p + 1) * pair_w)
            v2 = load_v(sl)
            zero = jnp.zeros_like(v2)
            o_pair = jnp.zeros((tb, pair_w), F32)
            for half in range(2):
                h = 2 * p + half
                o_pair = o_pair + _dot(a[h * tb:(h + 1) * tb], jnp.where(keeps[half], v2, zero))
            out_ref[:, sl] += o_pair
        return jnp.min(acc_new)

    def blocked(ref):
        return lambda sl: ref[0, :, sl]

    def whole(ref):
        return lambda sl: ref[:, sl]

    m0 = process(blocked(k0_ref), blocked(v0_ref), True)

    def step1():
        return process(blocked(k1_ref), blocked(v1_ref), False)

    m1 = lax.cond(qi >= 1, step1, lambda: m0)

    def step2():
        return process(blocked(k2_ref), blocked(v2_ref), False)

    m2 = lax.cond(jnp.logical_and(qi >= 2, m1 <= SB_SKIP_THRESHOLD), step2, lambda: m1)

    def cond(cr):
        j, m = cr
        return jnp.logical_and(j >= 0, m <= SB_SKIP_THRESHOLD)

    def body(cr):
        j, _ = cr
        r0 = pl.multiple_of(j * tb, tb)
        ck = pltpu.make_async_copy(kall_ref.at[bi, pl.ds(r0, tb), :], kbuf_ref, sem_ref.at[0])
        cv = pltpu.make_async_copy(vall_ref.at[bi, pl.ds(r0, tb), :], vbuf_ref, sem_ref.at[1])
        ck.start()
        cv.start()
        ck.wait()
        cv.wait()
        return j - 1, process(whole(kbuf_ref), whole(vbuf_ref), False)

    lax.while_loop(cond, body, (qi - SB_STATIC_BLOCKS, m2))

    o = out_ref[...]
    wd = SB_WIDTH
    dh_bits = SB_DH.bit_length() - 1
    bd = ((lax.broadcasted_iota(I32, (wd, wd), 0) >> dh_bits)
          == (lax.broadcasted_iota(I32, (wd, wd), 1) >> dh_bits)).astype(BF16)
    sq_h, sq_m, sq_l = _split3(o * o)
    var = (_dot(sq_h, bd) + _dot(sq_m, bd) + _dot(sq_l, bd)) * (1.0 / SB_DH)
    o_ref[0] = (o * lax.rsqrt(var + NORM_EPS) * gain_ref[...]).astype(o_ref.dtype)


def _stickbreak(sq, sk, sv, gain):
    b, s, w = sq.shape
    tb = SB_BLOCK
    cur = lambda bi, qi: (bi, qi, 0)
    prev1 = lambda bi, qi: (bi, jnp.maximum(qi - 1, 0), 0)
    prev2 = lambda bi, qi: (bi, jnp.maximum(qi - 2, 0), 0)
    fix = lambda bi, qi: (0, 0)
    blk = (1, tb, w)
    return pl.pallas_call(
        _sb_kernel,
        grid=(b, s // tb),
        in_specs=[
            pl.BlockSpec(blk, cur),
            pl.BlockSpec(blk, cur), pl.BlockSpec(blk, prev1), pl.BlockSpec(blk, prev2),
            pl.BlockSpec(blk, cur), pl.BlockSpec(blk, prev1), pl.BlockSpec(blk, prev2),
            pl.BlockSpec((1, w), fix),
            pl.BlockSpec(memory_space=pl.ANY),
            pl.BlockSpec(memory_space=pl.ANY),
        ],
        out_specs=pl.BlockSpec(blk, cur),
        out_shape=jax.ShapeDtypeStruct((b, s, w), BF16),
        scratch_shapes=[
            pltpu.VMEM((SB_HEADS * tb, tb), F32),
            pltpu.VMEM((tb, w), F32),
            pltpu.VMEM((tb, w), BF16),
            pltpu.VMEM((tb, w), BF16),
            pltpu.SemaphoreType.DMA((2,)),
        ],
        compiler_params=_cparams(2, V7X_VMEM_LIMIT_BYTES),
        name="stickbreak",
    )(sq, sk, sk, sk, sv, sv, sv, gain, sk, sv)


def _mix_router_kernel(ohg_ref, osb_ref, x_ref, wout_ref, gffn_ref, wrt_ref, br_ref,
                       h_ref, u_ref, idx_ref, rank_ref, gate_ref, cnt_ref, carry_ref, *, tm):
    @pl.when(pl.program_id(0) == 0)
    def _():
        carry_ref[...] = jnp.zeros_like(carry_ref)

    h = (x_ref[...]
         + _dot(ohg_ref[...], wout_ref[0:HG_WIDTH, :])
         + _dot(osb_ref[...], wout_ref[HG_WIDTH:HG_WIDTH + SB_WIDTH, :]))
    h_ref[...] = h
    var = jnp.mean(h * h, axis=-1, keepdims=True)
    u = h * lax.rsqrt(var + NORM_EPS) * gffn_ref[...]
    _store_row_tiles(u_ref, u)

    u_h, u_l = _split2(u)
    w_h, w_l = _split2(wrt_ref[...])
    logits = _dot_nt(w_h, u_h) + _dot_nt(w_h, u_l) + _dot_nt(w_l, u_h) + br_ref[...]

    e_io = lax.broadcasted_iota(I32, (N_EXPERTS, tm), 0).astype(F32)
    vals = logits
    member = jnp.zeros((N_EXPERTS, tm), F32)
    top_v, top_i = [], []
    for _ in range(TOP_K):
        m = jnp.max(vals, axis=0, keepdims=True)
        idx = jnp.min(jnp.where(vals == m, e_io, float(N_EXPERTS)), axis=0, keepdims=True)
        sel = e_io == idx
        top_v.append(m)
        top_i.append(idx)
        member = member + jnp.where(sel, 1.0, 0.0)
        vals = jnp.where(sel, -jnp.inf, vals)

    ex = [jnp.exp(tv - top_v[0]) for tv in top_v]
    den = ex[0] + ex[1] + ex[2] + ex[3]
    gates = [e / den for e in ex]

    n_io = lax.broadcasted_iota(I32, (tm, tm), 0)
    m_io = lax.broadcasted_iota(I32, (tm, tm), 1)
    before = jnp.where(n_io < m_io, 1.0, 0.0).astype(BF16)
    carry = carry_ref[...]
    cexcl = _dot(member.astype(BF16), before) + carry
    carry_new = carry + jnp.sum(member, axis=1, keepdims=True)
    carry_ref[...] = carry_new
    cnt_ref[...] = jnp.broadcast_to(carry_new, cnt_ref.shape).astype(I32)

    idx_ref[...] = jnp.concatenate(top_i, axis=0).astype(I32)
    ranks = [jnp.sum(jnp.where(e_io == ti, cexcl, 0.0), axis=0, keepdims=True) for ti in top_i]
    rank_ref[...] = jnp.concatenate(ranks, axis=0).astype(I32)

    r_io = lax.broadcasted_iota(I32, (128, tm), 0)
    gfull = jnp.zeros((128, tm), F32)
    for kk_, gk in enumerate(gates):
        gfull = jnp.where(r_io == kk_, jnp.broadcast_to(gk, (128, tm)), gfull)
    gate_ref[...] = gfull.T


def _mix_router(ohg, osb, x2, w_out_bf, g_ffn, w_router_t, b_router_col, tm=256):
    n, d = x2.shape
    row = lambda i: (i, 0)
    col = lambda i: (0, i)
    fix = lambda i: (0, 0)
    return pl.pallas_call(
        functools.partial(_mix_router_kernel, tm=tm),
        grid=(n // tm,),
        in_specs=[
            pl.BlockSpec((tm, HG_WIDTH), row),
            pl.BlockSpec((tm, SB_WIDTH), row),
            pl.BlockSpec((tm, d), row),
            pl.BlockSpec(w_out_bf.shape, fix),
            pl.BlockSpec((1, d), fix),
            pl.BlockSpec(w_router_t.shape, fix),
            pl.BlockSpec(b_router_col.shape, fix),
        ],
        out_specs=[
            pl.BlockSpec((tm, d), row),
            pl.BlockSpec((tm * ROW_TILE, LANES), row),
            pl.BlockSpec((TOP_K, tm), col),
            pl.BlockSpec((TOP_K, tm), col),
            pl.BlockSpec((tm, 128), row),
            pl.BlockSpec((N_EXPERTS, 128), fix),
        ],
        out_shape=[
            jax.ShapeDtypeStruct((n, d), F32),
            jax.ShapeDtypeStruct((n * ROW_TILE, LANES), F32),
            jax.ShapeDtypeStruct((TOP_K, n), I32),
            jax.ShapeDtypeStruct((TOP_K, n), I32),
            jax.ShapeDtypeStruct((n, 128), F32),
            jax.ShapeDtypeStruct((N_EXPERTS, 128), I32),
        ],
        scratch_shapes=[pltpu.VMEM((N_EXPERTS, 1), F32)],
        compiler_params=_cparams(1, V7X_VMEM_LIMIT_BYTES),
        name="mix_router",
    )(ohg, osb, x2, w_out_bf, g_ffn, w_router_t, b_router_col)


def _dest_kernel(start_ref, idx_ref, rank_ref, dest_ref):
    idx = idx_ref[...]
    base = jnp.zeros(idx.shape, I32)
    for e in range(N_EXPERTS):
        base = jnp.where(idx == e, start_ref[e], base)
    dest_ref[...] = base + rank_ref[...]


def _dest(start_pad, idx_t, rank_t):
    return pl.pallas_call(
        _dest_kernel,
        in_specs=[
            pl.BlockSpec(memory_space=pltpu.SMEM),
            pl.BlockSpec(memory_space=pltpu.VMEM),
            pl.BlockSpec(memory_space=pltpu.VMEM),
        ],
        out_specs=pl.BlockSpec(memory_space=pltpu.VMEM),
        out_shape=jax.ShapeDtypeStruct(idx_t.shape, I32),
        name="dest",
    )(start_pad, idx_t, rank_t)


def _row_tile(ref, row):
    return ref.at[pl.ds(pl.multiple_of(row * ROW_TILE, ROW_TILE), ROW_TILE)]


def _dispatch_kernel(cnt_ref, start_ref, nused_ref, dest_ref, u_ref, zero_ref, xd_ref, sem_ref, zsem_ref,
                     *, tm):
    i = pl.program_id(0)

    def issue(n, carry):
        for k in range(TOP_K):
            pltpu.make_async_copy(
                _row_tile(u_ref, n), _row_tile(xd_ref, dest_ref[k, n]), sem_ref).start(priority=k % 2)
        return carry

    lax.fori_loop(0, tm, issue, 0, unroll=8)
    for k in range(TOP_K):
        pltpu.make_async_copy(u_ref, xd_ref.at[pl.ds(0, tm * ROW_TILE)], sem_ref).wait()

    @pl.when(i == 0)
    def _():
        def per_expert(e, carry):
            cnt = cnt_ref[e]
            pad_end = ((cnt + MOE_BLOCK - 1) // MOE_BLOCK) * MOE_BLOCK
            base = start_ref[e]

            def zcopy(r):
                return pltpu.make_async_copy(_row_tile(zero_ref, 0), _row_tile(xd_ref, base + r), zsem_ref)

            def start(r, c2):
                zcopy(r).start()
                return c2

            def wait(r, c2):
                zcopy(r).wait()
                return c2

            lax.fori_loop(cnt, pad_end, start, 0)
            lax.fori_loop(cnt, pad_end, wait, 0)
            return carry

        lax.fori_loop(0, N_EXPERTS, per_expert, 0)

        blk_rows = MOE_BLOCK * ROW_TILE

        def zblock(blk):
            r0 = pl.multiple_of(blk * blk_rows, blk_rows)
            return pltpu.make_async_copy(zero_ref, xd_ref.at[pl.ds(r0, blk_rows)], zsem_ref)

        def bstart(blk, c2):
            zblock(blk).start()
            return c2

        def bwait(blk, c2):
            zblock(blk).wait()
            return c2

        n_blocks = xd_ref.shape[0] // blk_rows
        lax.fori_loop(nused_ref[0], n_blocks, bstart, 0)
        lax.fori_loop(nused_ref[0], n_blocks, bwait, 0)


def _dispatch(counts, start_pad, n_used, dest_t, u_rt, n_slots, tm=256):
    zero_blk = jnp.zeros((MOE_BLOCK * ROW_TILE, LANES), u_rt.dtype)
    grid_spec = pltpu.PrefetchScalarGridSpec(
        num_scalar_prefetch=3,
        grid=(u_rt.shape[0] // (tm * ROW_TILE),),
        in_specs=[
            pl.BlockSpec((TOP_K, tm), lambda i, c, s, nu: (0, i), memory_space=pltpu.SMEM),
            pl.BlockSpec((tm * ROW_TILE, LANES), lambda i, c, s, nu: (i, 0)),
            pl.BlockSpec(memory_space=pl.ANY),
        ],
        out_specs=pl.BlockSpec(memory_space=pl.ANY),
        scratch_shapes=[pltpu.SemaphoreType.DMA(()), pltpu.SemaphoreType.DMA(())],
    )
    return pl.pallas_call(
        functools.partial(_dispatch_kernel, tm=tm),
        grid_spec=grid_spec,
        out_shape=jax.ShapeDtypeStruct((n_slots * ROW_TILE, LANES), u_rt.dtype),
        compiler_params=_cparams(1),
        name="dispatch",
    )(counts, start_pad, n_used, dest_t, u_rt, zero_blk)


def _moe_kernel(be_ref, xb_ref, nused_ref, x_ref, wg_ref, bg_ref, wu_ref, bu_ref, wd_ref, bd_ref, y_ref):
    @pl.when(pl.program_id(0) < nused_ref[0])
    def _():
        x = _load_row_tiles(x_ref, MOE_BLOCK).astype(BF16)
        hg = _dot(x, wg_ref[0]) + bg_ref[0]
        hu = _dot(x, wu_ref[0]) + bu_ref[0]
        hg = jnp.minimum(hg, SWIGLU_LIMIT)
        hu = jnp.clip(hu, -SWIGLU_LIMIT, SWIGLU_LIMIT)
        glu = hg * _sigmoid(SWIGLU_ALPHA * hg)
        act = ((hu + 1.0) * glu).astype(BF16)
        _store_row_tiles(y_ref, _dot(act, wd_ref[0]) + bd_ref[0])

    @pl.when(pl.program_id(0) >= nused_ref[0])
    def _():
        y_ref[...] = jnp.zeros_like(y_ref)


def _moe(block_e, x_blk, n_used, x_disp, wg, bg, wu, bu, wd, bd):
    d, f = wg.shape[1], wg.shape[2]
    p = x_disp.shape[0] // ROW_TILE
    nb = p // MOE_BLOCK
    xmap = lambda i, be, xb, nu: (xb[i], 0)
    wmap = lambda i, be, xb, nu: (be[i], 0, 0)
    grid_spec = pltpu.PrefetchScalarGridSpec(
        num_scalar_prefetch=3,
        grid=(nb,),
        in_specs=[
            pl.BlockSpec((MOE_BLOCK * ROW_TILE, LANES), xmap),
            pl.BlockSpec((1, d, f), wmap),
            pl.BlockSpec((1, 1, f), wmap),
            pl.BlockSpec((1, d, f), wmap),
            pl.BlockSpec((1, 1, f), wmap),
            pl.BlockSpec((1, f, d), wmap),
            pl.BlockSpec((1, 1, d), wmap),
        ],
        out_specs=pl.BlockSpec((MOE_BLOCK * ROW_TILE, LANES), lambda i, be, xb, nu: (i, 0)),
    )
    return pl.pallas_call(
        _moe_kernel,
        grid_spec=grid_spec,
        out_shape=jax.ShapeDtypeStruct((p * ROW_TILE, LANES), F32),
        compiler_params=_cparams(1, V7X_VMEM_LIMIT_BYTES),
        name="moe",
    )(block_e, x_blk, n_used, x_disp, wg, bg, wu, bu, wd, bd)


def _combine_kernel(dest_ref, y_ref, h_ref, gate_ref, gain_ref, o_ref, ybuf_ref, sem_ref, *, tm):
    def issue(n, carry):
        for k in range(TOP_K):
            pltpu.make_async_copy(
                _row_tile(y_ref, dest_ref[k, n]), _row_tile(ybuf_ref, k * tm + n), sem_ref
            ).start(priority=k % 2)
        return carry

    lax.fori_loop(0, tm, issue, 0, unroll=8)
    pltpu.make_async_copy(y_ref.at[pl.ds(0, TOP_K * tm * ROW_TILE)], ybuf_ref, sem_ref).wait()

    gate = gate_ref[...]
    h = h_ref[...]
    chunks = []
    for j in range(ROW_TILE):
        c = h[:, j * LANES:(j + 1) * LANES]
        for k in range(TOP_K):
            c = c + gate[:, k:k + 1] * _load_row_tiles(ybuf_ref, tm, base=k * tm, chunk=j)
        chunks.append(c)
    acc = jnp.concatenate(chunks, axis=-1)
    var = jnp.mean(acc * acc, axis=-1, keepdims=True)
    o_ref[...] = (acc * lax.rsqrt(var + NORM_EPS) * gain_ref[...]).astype(o_ref.dtype)


def _combine(dest_t, y_disp, h, gates, gain, tm=256):
    n, d = h.shape
    row = lambda i: (i, 0)
    fix = lambda i: (0, 0)
    return pl.pallas_call(
        functools.partial(_combine_kernel, tm=tm),
        grid=(n // tm,),
        in_specs=[
            pl.BlockSpec((TOP_K, tm), lambda i: (0, i), memory_space=pltpu.SMEM),
            pl.BlockSpec(memory_space=pl.ANY),
            pl.BlockSpec((tm, d), row),
            pl.BlockSpec((tm, 128), row),
            pl.BlockSpec((1, d), fix),
        ],
        out_specs=pl.BlockSpec((tm, d), row),
        out_shape=jax.ShapeDtypeStruct((n, d), F32),
        scratch_shapes=[pltpu.VMEM((TOP_K * tm * ROW_TILE, LANES), F32), pltpu.SemaphoreType.DMA(())],
        compiler_params=_cparams(1, V7X_VMEM_LIMIT_BYTES),
        name="combine",
    )(dest_t, y_disp, h, gates, gain)


def _routing_tables(counts, n_pairs):
    padded = ((counts + MOE_BLOCK - 1) // MOE_BLOCK) * MOE_BLOCK
    cum_pad = jnp.cumsum(padded)
    start_pad = (cum_pad - padded).astype(I32)
    n_slots = ((n_pairs + MOE_BLOCK - 1) // MOE_BLOCK) * MOE_BLOCK + N_EXPERTS * MOE_BLOCK
    nb = n_slots // MOE_BLOCK
    n_used = (cum_pad[-1] // MOE_BLOCK).astype(I32)
    blk = jnp.arange(nb, dtype=I32)
    x_blk = jnp.minimum(blk, n_used - 1)
    block_start = x_blk * MOE_BLOCK
    block_e = jnp.minimum(jnp.sum(cum_pad[None, :] <= block_start[:, None], axis=1), N_EXPERTS - 1)
    return start_pad, block_e.astype(I32), x_blk, n_used.reshape(1), n_slots


def kernel(x, w_in, w_out, hg_lb_logits, hg_norm_gain, sb_norm_gain, norm_mix_gain, norm_ffn_gain,
           w_router, b_router, w_gate, b_gate, w_up, b_up, w_down, b_down, norm_final_gain):
    b, s, d = x.shape
    n = b * s
    f = w_gate.shape[-1]
    assert w_in.shape[0] == 1 and hg_lb_logits.shape[0] == 2, "single-layer trunk only"
    x2 = x.reshape(n, d).astype(F32)
    r3 = lambda a: a.reshape(b, s, a.shape[-1])

    q, k, lf, v, g, sq, sk, sv = _in_proj(
        x2, norm_mix_gain[0].reshape(1, d), w_in[0].astype(BF16), hg_lb_logits.astype(F32))
    o_hg = _hgrn2(r3(q), r3(k), r3(lf), r3(v), r3(g), hg_norm_gain[0].reshape(1, HG_WIDTH))
    o_sb = _stickbreak(r3(sq), r3(sk), r3(sv), sb_norm_gain[0].reshape(1, SB_WIDTH))
    h_mid, u, idx_t, rank_t, gates, cnt = _mix_router(
        o_hg.reshape(n, HG_WIDTH), o_sb.reshape(n, SB_WIDTH), x2, w_out[0].astype(BF16),
        norm_ffn_gain[0].reshape(1, d), w_router[0].T.astype(F32),
        b_router[0].reshape(N_EXPERTS, 1).astype(F32))
    counts = cnt[:, 0]
    start_pad, block_e, x_blk, n_used, n_slots = _routing_tables(counts, n * TOP_K)
    dest_t = _dest(start_pad, idx_t, rank_t)
    x_disp = _dispatch(counts, start_pad, n_used, dest_t, u, n_slots)
    y_disp = _moe(block_e, x_blk, n_used, x_disp,
                  w_gate[0].astype(BF16), b_gate[0].reshape(N_EXPERTS, 1, f),
                  w_up[0].astype(BF16), b_up[0].reshape(N_EXPERTS, 1, f),
                  w_down[0].astype(BF16), b_down[0].reshape(N_EXPERTS, 1, d))
    out = _combine(dest_t, y_disp, h_mid, gates, norm_final_gain.reshape(1, d))
    return out.reshape(b, s, d).astype(x.dtype)
```

```python
import functools

import jax
import jax.numpy as jnp
from jax import lax
from jax.experimental import pallas as pl
from jax.experimental.pallas import tpu as pltpu

F32 = jnp.float32
BF16 = jnp.bfloat16
I32 = jnp.int32

NORM_EPS = 1e-5
HG_HEADS = 4
HG_DK = 128
HG_WIDTH = HG_HEADS * HG_DK
HG_CHUNK = 64
HG_LEVELS = (32, 16, 8)
HG_DIAG = 8
SB_HEADS = 8
SB_DH = 64
SB_WIDTH = SB_HEADS * SB_DH
SB_BLOCK = 128
SB_STATIC_BLOCKS = 3
SB_SKIP_THRESHOLD = 104.0
N_EXPERTS = 32
TOP_K = 4
MOE_BLOCK = 256
SWIGLU_LIMIT = 7.0
SWIGLU_ALPHA = 1.702
NEG_BIG = -1e30

V7X_VMEM_LIMIT_BYTES = 56 * 1024 * 1024
ROW_TILE = 8
LANES = 128
D_MODEL = ROW_TILE * LANES


def _cparams(n_axes, vmem_bytes=None):
    return pltpu.CompilerParams(
        dimension_semantics=("arbitrary",) * n_axes,
        vmem_limit_bytes=vmem_bytes,
    )


def _sigmoid(x):
    return 1.0 / (1.0 + jnp.exp(-x))


def _split2(x):
    hi = x.astype(BF16)
    lo = (x - hi.astype(F32)).astype(BF16)
    return hi, lo


def _split3(x):
    hi = x.astype(BF16)
    r = x - hi.astype(F32)
    mid = r.astype(BF16)
    lo = (r - mid.astype(F32)).astype(BF16)
    return hi, mid, lo


def _dot(a, b):
    return jnp.dot(a, b, preferred_element_type=F32)


def _store_row_tiles(ref, x, base=0):
    rows = x.shape[0]
    for j in range(ROW_TILE):
        ref[pl.ds(base * ROW_TILE + j, rows, stride=ROW_TILE), :] = x[:, j * LANES:(j + 1) * LANES]


def _load_row_tiles(ref, rows, base=0, chunk=None):
    if chunk is not None:
        return ref[pl.ds(base * ROW_TILE + chunk, rows, stride=ROW_TILE), :]
    return jnp.concatenate(
        [ref[pl.ds(base * ROW_TILE + j, rows, stride=ROW_TILE), :] for j in range(ROW_TILE)], axis=-1)


def _dot_nt(a, b):
    return lax.dot_general(a, b, (((1,), (1,)), ((), ())), preferred_element_type=F32)


def _dot_tn(a, b):
    return lax.dot_general(a, b, (((0,), (0,)), ((), ())), preferred_element_type=F32)


def _in_proj_kernel(x_ref, gain_ref, w_ref, lbl_ref,
                    q_ref, k_ref, lf_ref, v_ref, g_ref, sq_ref, sk_ref, sv_ref):
    x = x_ref[...]
    var = jnp.mean(x * x, axis=-1, keepdims=True)
    u = (x * lax.rsqrt(var + NORM_EPS) * gain_ref[...]).astype(BF16)

    lbl = lbl_ref[...]
    mx = jnp.max(lbl, axis=0, keepdims=True)
    ex = jnp.exp(lbl - mx)
    lb = ex[0:1, :] / jnp.sum(ex, axis=0, keepdims=True)

    def seg(i):
        return _dot(u, w_ref[:, i * HG_WIDTH:(i + 1) * HG_WIDTH])

    hq = seg(0)
    q_ref[...] = hq * _sigmoid(hq)
    f_sig = _sigmoid(seg(1))
    lf_ref[...] = jnp.log(lb + (1.0 - lb) * f_sig)
    k_ref[...] = (1.0 - lb) * (1.0 - f_sig)
    v_ref[...] = seg(2)
    hg = seg(3)
    g_ref[...] = hg * _sigmoid(hg)
    sq_ref[...] = (seg(4) * (SB_DH ** -0.5)).astype(BF16)
    sk_ref[...] = seg(5).astype(BF16)
    sv_ref[...] = seg(6).astype(BF16)


def _in_proj(x2, gain, w_in_bf, lb_logits, tm=512):
    n, d = x2.shape
    cols = w_in_bf.shape[1]
    row = lambda i: (i, 0)
    fix = lambda i: (0, 0)
    o_f32 = jax.ShapeDtypeStruct((n, HG_WIDTH), F32)
    o_bf = jax.ShapeDtypeStruct((n, SB_WIDTH), BF16)
    return pl.pallas_call(
        _in_proj_kernel,
        grid=(n // tm,),
        in_specs=[
            pl.BlockSpec((tm, d), row),
            pl.BlockSpec((1, d), fix),
            pl.BlockSpec((d, cols), fix),
            pl.BlockSpec(lb_logits.shape, fix),
        ],
        out_specs=[pl.BlockSpec((tm, HG_WIDTH), row)] * 8,
        out_shape=[o_f32] * 5 + [o_bf] * 3,
        compiler_params=_cparams(1, V7X_VMEM_LIMIT_BYTES),
        name="in_proj",
    )(x2, gain, w_in_bf, lb_logits)


def _hgrn2_consts():
    c = HG_CHUNK
    t = jnp.arange(c)[:, None]
    s = jnp.arange(c)[None, :]
    mats = [(s <= t)]
    for lv in HG_LEVELS:
        ref = (t // (2 * lv)) * (2 * lv) + lv - 1
        mats.append(s <= ref)
    return jnp.concatenate(mats, axis=0).astype(BF16)


def _hgrn2_kernel(q_ref, k_ref, lf_ref, v_ref, g_ref, gain_ref, cmat_ref, o_ref, st_ref, *, ts):
    c = HG_CHUNK
    w = HG_WIDTH

    @pl.when(pl.program_id(1) == 0)
    def _():
        st_ref[...] = jnp.zeros_like(st_ref)

    row_w = lax.broadcasted_iota(I32, (c, w), 0)
    row_c = lax.broadcasted_iota(I32, (c, c), 0)
    col_c = lax.broadcasted_iota(I32, (c, c), 1)
    row_d = lax.broadcasted_iota(I32, (HG_DIAG, w), 0)
    dk_bits = HG_DK.bit_length() - 1
    bd = ((lax.broadcasted_iota(I32, (w, w), 0) >> dk_bits)
          == (lax.broadcasted_iota(I32, (w, w), 1) >> dk_bits)).astype(BF16)
    cmat = cmat_ref[...]
    gain = gain_ref[...]

    def chunk(ci, carry):
        r0 = pl.multiple_of(ci * c, c)
        q = q_ref[0, pl.ds(r0, c), :]
        kk = k_ref[0, pl.ds(r0, c), :]
        lf = lf_ref[0, pl.ds(r0, c), :]
        v = v_ref[0, pl.ds(r0, c), :]
        g = g_ref[0, pl.ds(r0, c), :]

        lf_h, lf_l = _split2(lf)
        gg = _dot(cmat, lf_h) + _dot(cmat, lf_l)
        G = gg[0:c]

        scores = [jnp.zeros((c, c), F32) for _ in range(HG_HEADS)]
        for li, lv in enumerate(HG_LEVELS):
            gref = gg[(li + 1) * c:(li + 2) * c]
            is_q = (row_w & (2 * lv - 1)) >= lv
            e = jnp.exp(jnp.where(is_q, G - gref, gref - G))
            ql = jnp.where(is_q, q * e, 0.0).astype(BF16)
            kl = jnp.where(is_q, 0.0, kk * e).astype(BF16)
            grp_bits = (2 * lv).bit_length() - 1
            same = (row_c >> grp_bits) == (col_c >> grp_bits)
            for h in range(HG_HEADS):
                sl = slice(h * HG_DK, (h + 1) * HG_DK)
                scores[h] = scores[h] + jnp.where(same, _dot_nt(ql[:, sl], kl[:, sl]), 0.0)

        tiles = []
        for b in range(c // HG_DIAG):
            rs = slice(b * HG_DIAG, (b + 1) * HG_DIAG)
            gb, qb, kb = G[rs], q[rs], kk[rs]
            for s in range(HG_DIAG):
                gs = jnp.broadcast_to(gb[s:s + 1, :], (HG_DIAG, w))
                ks = jnp.broadcast_to(kb[s:s + 1, :], (HG_DIAG, w))
                e = jnp.exp(jnp.where(row_d >= s, gb - gs, NEG_BIG))
                tiles.append(qb * e * ks)
        p_all = jnp.concatenate(tiles, axis=0)
        r_all = _dot(p_all.astype(BF16), bd)
        o_blocks = []
        for b in range(c // HG_DIAG):
            vb = v[b * HG_DIAG:(b + 1) * HG_DIAG]
            ob = jnp.zeros((HG_DIAG, w), F32)
            for s in range(HG_DIAG):
                i0 = (b * HG_DIAG + s) * HG_DIAG
                vs = jnp.broadcast_to(vb[s:s + 1, :], (HG_DIAG, w))
                ob = ob + r_all[i0:i0 + HG_DIAG] * vs
            o_blocks.append(ob)
        o = jnp.concatenate(o_blocks, axis=0)

        qg = (q * jnp.exp(G)).astype(BF16)
        g_last = jnp.broadcast_to(G[c - 1:c, :], (c, w))
        kh = (kk * jnp.exp(g_last - G)).astype(BF16)
        dec = jnp.exp(G[c - 1:c, :])
        v_bf = v.astype(BF16)
        outs = []
        for h in range(HG_HEADS):
            sl = slice(h * HG_DK, (h + 1) * HG_DK)
            st = st_ref[h]
            oh = (o[:, sl]
                  + _dot(scores[h].astype(BF16), v_bf[:, sl])
                  + _dot_nt(qg[:, sl], st.astype(BF16)))
            st_ref[h] = st * dec[:, sl] + _dot_tn(v_bf[:, sl], kh[:, sl])
            var = jnp.mean(oh * oh, axis=-1, keepdims=True)
            outs.append(oh * lax.rsqrt(var + NORM_EPS))
        on = jnp.concatenate(outs, axis=-1) * gain * g
        o_ref[0, pl.ds(r0, c), :] = on.astype(o_ref.dtype)
        return carry

    lax.fori_loop(0, ts // c, chunk, 0)


def _hgrn2(q, k, lf, v, g, gain, ts=256):
    b, s, w = q.shape
    blk = lambda bi, si: (bi, si, 0)
    fix = lambda bi, si: (0, 0)
    cmat = _hgrn2_consts()
    return pl.pallas_call(
        functools.partial(_hgrn2_kernel, ts=ts),
        grid=(b, s // ts),
        in_specs=[pl.BlockSpec((1, ts, w), blk)] * 5 + [
            pl.BlockSpec((1, w), fix),
            pl.BlockSpec(cmat.shape, fix),
        ],
        out_specs=pl.BlockSpec((1, ts, w), blk),
        out_shape=jax.ShapeDtypeStruct((b, s, w), BF16),
        scratch_shapes=[pltpu.VMEM((HG_HEADS, HG_DK, HG_DK), F32)],
        compiler_params=_cparams(2, V7X_VMEM_LIMIT_BYTES),
        name="hgrn2",
    )(q, k, lf, v, g, gain, cmat)


def _sb_kernel(q_ref, k0_ref, k1_ref, k2_ref, v0_ref, v1_ref, v2_ref, gain_ref, kall_ref, vall_ref,
               o_ref, acc_ref, out_ref, kbuf_ref, vbuf_ref, sem_ref):
    tb = SB_BLOCK
    bi = pl.program_id(0)
    qi = pl.program_id(1)
    n_pairs = SB_HEADS // 2
    pair_w = 2 * SB_DH

    acc_ref[...] = jnp.zeros_like(acc_ref)
    out_ref[...] = jnp.zeros_like(out_ref)

    t_io = lax.broadcasted_iota(I32, (SB_HEADS * tb, tb), 0) & (tb - 1)
    s_io = lax.broadcasted_iota(I32, (SB_HEADS * tb, tb), 1)
    causal = s_io < t_io
    u_row = lax.broadcasted_iota(I32, (tb, 2 * tb), 0)
    u_col = lax.broadcasted_iota(I32, (tb, 2 * tb), 1)
    um = jnp.where(jnp.logical_or(u_col >= tb, u_row > u_col), 1.0, 0.0).astype(BF16)
    lane = lax.broadcasted_iota(I32, (tb, pair_w), 1)
    lo_half = lane < SB_DH
    keeps = (lo_half, jnp.logical_not(lo_half))

    def process(load_k, load_v, diag):
        zs = []
        for p in range(n_pairs):
            sl = slice(p * pair_w, (p + 1) * pair_w)
            q2 = q_ref[0, :, sl]
            k2 = load_k(sl)
            zero = jnp.zeros_like(q2)
            for half in range(2):
                zs.append(_dot_nt(jnp.where(keeps[half], q2, zero), k2))
        z = jnp.concatenate(zs, axis=0)
        sp_full = jnp.maximum(z, 0.0) + jnp.log(1.0 + jnp.exp(-jnp.abs(z)))
        sp = jnp.where(causal, sp_full, 0.0) if diag else sp_full
        sp_h, sp_l = _split2(sp)
        lt = _dot(sp_h, um) + _dot(sp_l, um)
        acc = acc_ref[...]
        a = jnp.exp((z - sp_full) - lt[:, :tb] - acc)
        if diag:
            a = jnp.where(causal, a, 0.0)
        a = a.astype(BF16)
        acc_new = acc + lt[:, tb:]
        acc_ref[...] = acc_new
        for p in range(n_pairs):
            sl = slice(p * pair_w, (p + 1) * pair_w)
            v2 = load_v(sl)
            zero = jnp.zeros_like(v2)
            o_pair = jnp.zeros((tb, pair_w), F32)
            for half in range(2):
                h = 2 * p + half
                o_pair = o_pair + _dot(a[h * tb:(h + 1) * tb], jnp.where(keeps[half], v2, zero))
            out_ref[:, sl] += o_pair
        return jnp.min(acc_new)

    def blocked(ref):
        return lambda sl: ref[0, :, sl]

    def whole(ref):
        return lambda sl: ref[:, sl]

    m0 = process(blocked(k0_ref), blocked(v0_ref), True)

    def step1():
        return process(blocked(k1_ref), blocked(v1_ref), False)

    m1 = lax.cond(qi >= 1, step1, lambda: m0)

    def step2():
        return process(blocked(k2_ref), blocked(v2_ref), False)

    m2 = lax.cond(jnp.logical_and(qi >= 2, m1 <= SB_SKIP_THRESHOLD), step2, lambda: m1)

    def cond(cr):
        j, m = cr
        return jnp.logical_and(j >= 0, m <= SB_SKIP_THRESHOLD)

    def body(cr):
        j, _ = cr
        r0 = pl.multiple_of(j * tb, tb)
        ck = pltpu.make_async_copy(kall_ref.at[bi, pl.ds(r0, tb), :], kbuf_ref, sem_ref.at[0])
        cv = pltpu.make_async_copy(vall_ref.at[bi, pl.ds(r0, tb), :], vbuf_ref, sem_ref.at[1])
        ck.start()
        cv.start()
        ck.wait()
        cv.wait()
        return j - 1, process(whole(kbuf_ref), whole(vbuf_ref), False)

    lax.while_loop(cond, body, (qi - SB_STATIC_BLOCKS, m2))

    o = out_ref[...]
    wd = SB_WIDTH
    dh_bits = SB_DH.bit_length() - 1
    bd = ((lax.broadcasted_iota(I32, (wd, wd), 0) >> dh_bits)
          == (lax.broadcasted_iota(I32, (wd, wd), 1) >> dh_bits)).astype(BF16)
    sq_h, sq_m, sq_l = _split3(o * o)
    var = (_dot(sq_h, bd) + _dot(sq_m, bd) + _dot(sq_l, bd)) * (1.0 / SB_DH)
    o_ref[0] = (o * lax.rsqrt(var + NORM_EPS) * gain_ref[...]).astype(o_ref.dtype)


def _stickbreak(sq, sk, sv, gain):
    b, s, w = sq.shape
    tb = SB_BLOCK
    cur = lambda bi, qi: (bi, qi, 0)
    prev1 = lambda bi, qi: (bi, jnp.maximum(qi - 1, 0), 0)
    prev2 = lambda bi, qi: (bi, jnp.maximum(qi - 2, 0), 0)
    fix = lambda bi, qi: (0, 0)
    blk = (1, tb, w)
    return pl.pallas_call(
        _sb_kernel,
        grid=(b, s // tb),
        in_specs=[
            pl.BlockSpec(blk, cur),
            pl.BlockSpec(blk, cur), pl.BlockSpec(blk, prev1), pl.BlockSpec(blk, prev2),
            pl.BlockSpec(blk, cur), pl.BlockSpec(blk, prev1), pl.BlockSpec(blk, prev2),
            pl.BlockSpec((1, w), fix),
            pl.BlockSpec(memory_space=pl.ANY),
            pl.BlockSpec(memory_space=pl.ANY),
        ],
        out_specs=pl.BlockSpec(blk, cur),
        out_shape=jax.ShapeDtypeStruct((b, s, w), BF16),
        scratch_shapes=[
            pltpu.VMEM((SB_HEADS * tb, tb), F32),
            pltpu.VMEM((tb, w), F32),
            pltpu.VMEM((tb, w), BF16),
            pltpu.VMEM((tb, w), BF16),
            pltpu.SemaphoreType.DMA((2,)),
        ],
        compiler_params=_cparams(2, V7X_VMEM_LIMIT_BYTES),
        name="stickbreak",
    )(sq, sk, sk, sk, sv, sv, sv, gain, sk, sv)


def _mix_router_kernel(ohg_ref, osb_ref, x_ref, wout_ref, gffn_ref, wrt_ref, br_ref,
                       h_ref, u_ref, idx_ref, rank_ref, gate_ref, cnt_ref, carry_ref, *, tm):
    @pl.when(pl.program_id(0) == 0)
    def _():
        carry_ref[...] = jnp.zeros_like(carry_ref)

    h = (x_ref[...]
         + _dot(ohg_ref[...], wout_ref[0:HG_WIDTH, :])
         + _dot(osb_ref[...], wout_ref[HG_WIDTH:HG_WIDTH + SB_WIDTH, :]))
    h_ref[...] = h
    var = jnp.mean(h * h, axis=-1, keepdims=True)
    u = h * lax.rsqrt(var + NORM_EPS) * gffn_ref[...]
    _store_row_tiles(u_ref, u)

    u_h, u_l = _split2(u)
    w_h, w_l = _split2(wrt_ref[...])
    logits = _dot_nt(w_h, u_h) + _dot_nt(w_h, u_l) + _dot_nt(w_l, u_h) + br_ref[...]

    e_io = lax.broadcasted_iota(I32, (N_EXPERTS, tm), 0).astype(F32)
    vals = logits
    member = jnp.zeros((N_EXPERTS, tm), F32)
    top_v, top_i = [], []
    for _ in range(TOP_K):
        m = jnp.max(vals, axis=0, keepdims=True)
        idx = jnp.min(jnp.where(vals == m, e_io, float(N_EXPERTS)), axis=0, keepdims=True)
        sel = e_io == idx
        top_v.append(m)
        top_i.append(idx)
        member = member + jnp.where(sel, 1.0, 0.0)
        vals = jnp.where(sel, -jnp.inf, vals)

    ex = [jnp.exp(tv - top_v[0]) for tv in top_v]
    den = ex[0] + ex[1] + ex[2] + ex[3]
    gates = [e / den for e in ex]

    n_io = lax.broadcasted_iota(I32, (tm, tm), 0)
    m_io = lax.broadcasted_iota(I32, (tm, tm), 1)
    before = jnp.where(n_io < m_io, 1.0, 0.0).astype(BF16)
    carry = carry_ref[...]
    cexcl = _dot(member.astype(BF16), before) + carry
    carry_new = carry + jnp.sum(member, axis=1, keepdims=True)
    carry_ref[...] = carry_new
    cnt_ref[...] = jnp.broadcast_to(carry_new, cnt_ref.shape).astype(I32)

    idx_ref[...] = jnp.concatenate(top_i, axis=0).astype(I32)
    ranks = [jnp.sum(jnp.where(e_io == ti, cexcl, 0.0), axis=0, keepdims=True) for ti in top_i]
    rank_ref[...] = jnp.concatenate(ranks, axis=0).astype(I32)

    r_io = lax.broadcasted_iota(I32, (128, tm), 0)
    gfull = jnp.zeros((128, tm), F32)
    for kk_, gk in enumerate(gates):
        gfull = jnp.where(r_io == kk_, jnp.broadcast_to(gk, (128, tm)), gfull)
    gate_ref[...] = gfull.T


def _mix_router(ohg, osb, x2, w_out_bf, g_ffn, w_router_t, b_router_col, tm=256):
    n, d = x2.shape
    row = lambda i: (i, 0)
    col = lambda i: (0, i)
    fix = lambda i: (0, 0)
    return pl.pallas_call(
        functools.partial(_mix_router_kernel, tm=tm),
        grid=(n // tm,),
        in_specs=[
            pl.BlockSpec((tm, HG_WIDTH), row),
            pl.BlockSpec((tm, SB_WIDTH), row),
            pl.BlockSpec((tm, d), row),
            pl.BlockSpec(w_out_bf.shape, fix),
            pl.BlockSpec((1, d), fix),
            pl.BlockSpec(w_router_t.shape, fix),
            pl.BlockSpec(b_router_col.shape, fix),
        ],
        out_specs=[
            pl.BlockSpec((tm, d), row),
            pl.BlockSpec((tm * ROW_TILE, LANES), row),
            pl.BlockSpec((TOP_K, tm), col),
            pl.BlockSpec((TOP_K, tm), col),
            pl.BlockSpec((tm, 128), row),
            pl.BlockSpec((N_EXPERTS, 128), fix),
        ],
        out_shape=[
            jax.ShapeDtypeStruct((n, d), F32),
            jax.ShapeDtypeStruct((n * ROW_TILE, LANES), F32),
            jax.ShapeDtypeStruct((TOP_K, n), I32),
            jax.ShapeDtypeStruct((TOP_K, n), I32),
            jax.ShapeDtypeStruct((n, 128), F32),
            jax.ShapeDtypeStruct((N_EXPERTS, 128), I32),
        ],
        scratch_shapes=[pltpu.VMEM((N_EXPERTS, 1), F32)],
        compiler_params=_cparams(1, V7X_VMEM_LIMIT_BYTES),
        name="mix_router",
    )(ohg, osb, x2, w_out_bf, g_ffn, w_router_t, b_router_col)


def _dest_kernel(start_ref, idx_ref, rank_ref, dest_ref):
    idx = idx_ref[...]
    base = jnp.zeros(idx.shape, I32)
    for e in range(N_EXPERTS):
        base = jnp.where(idx == e, start_ref[e], base)
    dest_ref[...] = base + rank_ref[...]


def _dest(start_pad, idx_t, rank_t):
    return pl.pallas_call(
        _dest_kernel,
        in_specs=[
            pl.BlockSpec(memory_space=pltpu.SMEM),
            pl.BlockSpec(memory_space=pltpu.VMEM),
            pl.BlockSpec(memory_space=pltpu.VMEM),
        ],
        out_specs=pl.BlockSpec(memory_space=pltpu.VMEM),
        out_shape=jax.ShapeDtypeStruct(idx_t.shape, I32),
        name="dest",
    )(start_pad, idx_t, rank_t)


def _row_tile(ref, row):
    return ref.at[pl.ds(pl.multiple_of(row * ROW_TILE, ROW_TILE), ROW_TILE)]


def _slot_token_kernel(cnt_ref, start_ref, dest_ref, tok_ref, *, tn):
    i = pl.program_id(0)
    n_slots = tok_ref.shape[0]

    @pl.when(i == 0)
    def _():
        def fill(lo, hi):
            def body(r, carry):
                tok_ref[r] = 0
                return carry
            lax.fori_loop(lo, hi, body, 0)

        def per_expert(e, end_prev):
            cnt = cnt_ref[e]
            base = start_ref[e]
            pad_end = base + ((cnt + MOE_BLOCK - 1) // MOE_BLOCK) * MOE_BLOCK
            fill(base + cnt, pad_end)
            return pad_end

        used_end = lax.fori_loop(0, N_EXPERTS, per_expert, 0)
        fill(used_end, n_slots)

    def scatter(n, carry):
        for k in range(TOP_K):
            tok_ref[dest_ref[k, n]] = i * tn + n
        return carry

    lax.fori_loop(0, tn, scatter, 0, unroll=8)


def _slot_token(counts, start_pad, dest_t, n_slots, tn=4096):
    n = dest_t.shape[1]
    tn = min(tn, n)
    grid_spec = pltpu.PrefetchScalarGridSpec(
        num_scalar_prefetch=2,
        grid=(n // tn,),
        in_specs=[pl.BlockSpec((TOP_K, tn), lambda i, c, s: (0, i), memory_space=pltpu.SMEM)],
        out_specs=pl.BlockSpec((n_slots,), lambda i, c, s: (0,), memory_space=pltpu.SMEM),
    )
    return pl.pallas_call(
        functools.partial(_slot_token_kernel, tn=tn),
        grid_spec=grid_spec,
        out_shape=jax.ShapeDtypeStruct((n_slots,), I32),
        compiler_params=_cparams(1),
        name="slot_token",
    )(counts, start_pad, dest_t)


MOE_PAIR = 2


def _moe_kernel(be_ref, tok0_ref, tokn_ref, u_ref, *refs):
    n_w = 6 * MOE_PAIR
    w_refs, y_ref = refs[:n_w], refs[n_w]
    bufs, sem_ref = refs[n_w + 1:n_w + 1 + MOE_PAIR], refs[n_w + 1 + MOE_PAIR]
    i = pl.program_id(0)
    blk_rows = MOE_BLOCK * ROW_TILE

    def gather(tok_ref, half):
        for r in range(MOE_BLOCK):
            pltpu.make_async_copy(
                _row_tile(u_ref, tok_ref[0, 0, half * MOE_BLOCK + r]), _row_tile(bufs[half], r),
                sem_ref.at[half]).start(priority=r % 2)

    def wait(half):
        pltpu.make_async_copy(u_ref.at[pl.ds(0, blk_rows)], bufs[half], sem_ref.at[half]).wait()

    @pl.when(i == 0)
    def _():
        for half in range(MOE_PAIR):
            gather(tok0_ref, half)

    for half in range(MOE_PAIR):
        wg_ref, bg_ref, wu_ref, bu_ref, wd_ref, bd_ref = w_refs[6 * half:6 * half + 6]
        wait(half)
        x = _load_row_tiles(bufs[half], MOE_BLOCK).astype(BF16)
        gather(tokn_ref, half)
        hg = _dot(x, wg_ref[0]) + bg_ref[0]
        hu = _dot(x, wu_ref[0]) + bu_ref[0]
        hg = jnp.minimum(hg, SWIGLU_LIMIT)
        hu = jnp.clip(hu, -SWIGLU_LIMIT, SWIGLU_LIMIT)
        glu = hg * _sigmoid(SWIGLU_ALPHA * hg)
        act = ((hu + 1.0) * glu).astype(BF16)
        _store_row_tiles(y_ref, _dot(act, wd_ref[0]) + bd_ref[0], base=half * MOE_BLOCK)

    @pl.when(i == pl.num_programs(0) - 1)
    def _():
        for half in range(MOE_PAIR):
            wait(half)


def _moe(block_e, slot_tok, u_rt, wg, bg, wu, bu, wd, bd):
    d, f = wg.shape[1], wg.shape[2]
    n_slots = slot_tok.shape[0]
    step_rows = MOE_PAIR * MOE_BLOCK
    n_steps = n_slots // step_rows
    tok3 = slot_tok.reshape(n_steps, 1, step_rows)
    w_specs, w_args = [], []
    for half in range(MOE_PAIR):
        wmap = lambda i, be, half=half: (be[MOE_PAIR * i + half], 0, 0)
        w_specs += [pl.BlockSpec((1, d, f), wmap), pl.BlockSpec((1, 1, f), wmap),
                    pl.BlockSpec((1, d, f), wmap), pl.BlockSpec((1, 1, f), wmap),
                    pl.BlockSpec((1, f, d), wmap), pl.BlockSpec((1, 1, d), wmap)]
        w_args += [wg, bg, wu, bu, wd, bd]
    grid_spec = pltpu.PrefetchScalarGridSpec(
        num_scalar_prefetch=1,
        grid=(n_steps,),
        in_specs=[
            pl.BlockSpec((1, 1, step_rows), lambda i, be: (0, 0, 0), memory_space=pltpu.SMEM),
            pl.BlockSpec((1, 1, step_rows), lambda i, be: (jnp.minimum(i + 1, n_steps - 1), 0, 0),
                         memory_space=pltpu.SMEM),
            pl.BlockSpec(memory_space=pl.ANY),
        ] + w_specs,
        out_specs=pl.BlockSpec((step_rows * ROW_TILE, LANES), lambda i, be: (i, 0)),
        scratch_shapes=[pltpu.VMEM((MOE_BLOCK * ROW_TILE, LANES), F32)] * MOE_PAIR
        + [pltpu.SemaphoreType.DMA((MOE_PAIR,))],
    )
    return pl.pallas_call(
        _moe_kernel,
        grid_spec=grid_spec,
        out_shape=jax.ShapeDtypeStruct((n_slots * ROW_TILE, LANES), F32),
        compiler_params=_cparams(1, V7X_VMEM_LIMIT_BYTES),
        name="moe",
    )(block_e, tok3, tok3, u_rt, *w_args)


def _combine_kernel(dest_ref, y_ref, h_ref, gate_ref, gain_ref, o_ref, ybuf_ref, sem_ref, *, tm):
    def issue(n, carry):
        for k in range(TOP_K):
            pltpu.make_async_copy(
                _row_tile(y_ref, dest_ref[k, n]), _row_tile(ybuf_ref, k * tm + n), sem_ref
            ).start(priority=k % 2)
        return carry

    lax.fori_loop(0, tm, issue, 0, unroll=8)
    pltpu.make_async_copy(y_ref.at[pl.ds(0, TOP_K * tm * ROW_TILE)], ybuf_ref, sem_ref).wait()

    gate = gate_ref[...]
    h = h_ref[...]
    chunks = []
    for j in range(ROW_TILE):
        c = h[:, j * LANES:(j + 1) * LANES]
        for k in range(TOP_K):
            c = c + gate[:, k:k + 1] * _load_row_tiles(ybuf_ref, tm, base=k * tm, chunk=j)
        chunks.append(c)
    acc = jnp.concatenate(chunks, axis=-1)
    var = jnp.mean(acc * acc, axis=-1, keepdims=True)
    o_ref[...] = (acc * lax.rsqrt(var + NORM_EPS) * gain_ref[...]).astype(o_ref.dtype)


def _combine(dest_t, y_disp, h, gates, gain, tm=256):
    n, d = h.shape
    row = lambda i: (i, 0)
    fix = lambda i: (0, 0)
    return pl.pallas_call(
        functools.partial(_combine_kernel, tm=tm),
        grid=(n // tm,),
        in_specs=[
            pl.BlockSpec((TOP_K, tm), lambda i: (0, i), memory_space=pltpu.SMEM),
            pl.BlockSpec(memory_space=pl.ANY),
            pl.BlockSpec((tm, d), row),
            pl.BlockSpec((tm, 128), row),
            pl.BlockSpec((1, d), fix),
        ],
        out_specs=pl.BlockSpec((tm, d), row),
        out_shape=jax.ShapeDtypeStruct((n, d), F32),
        scratch_shapes=[pltpu.VMEM((TOP_K * tm * ROW_TILE, LANES), F32), pltpu.SemaphoreType.DMA(())],
        compiler_params=_cparams(1, V7X_VMEM_LIMIT_BYTES),
        name="combine",
    )(dest_t, y_disp, h, gates, gain)


def _routing_tables(counts, n_pairs):
    padded = ((counts + MOE_BLOCK - 1) // MOE_BLOCK) * MOE_BLOCK
    cum_pad = jnp.cumsum(padded)
    start_pad = (cum_pad - padded).astype(I32)
    n_slots = ((n_pairs + MOE_BLOCK - 1) // MOE_BLOCK) * MOE_BLOCK + N_EXPERTS * MOE_BLOCK
    nb = n_slots // MOE_BLOCK
    block_start = jnp.arange(nb, dtype=I32) * MOE_BLOCK
    block_e = jnp.minimum(jnp.sum(cum_pad[None, :] <= block_start[:, None], axis=1), N_EXPERTS - 1)
    return start_pad, block_e.astype(I32), n_slots


def kernel(x, w_in, w_out, hg_lb_logits, hg_norm_gain, sb_norm_gain, norm_mix_gain, norm_ffn_gain,
           w_router, b_router, w_gate, b_gate, w_up, b_up, w_down, b_down, norm_final_gain):
    b, s, d = x.shape
    n = b * s
    f = w_gate.shape[-1]
    assert w_in.shape[0] == 1 and hg_lb_logits.shape[0] == 2, "single-layer trunk only"
    x2 = x.reshape(n, d).astype(F32)
    r3 = lambda a: a.reshape(b, s, a.shape[-1])

    q, k, lf, v, g, sq, sk, sv = _in_proj(
        x2, norm_mix_gain[0].reshape(1, d), w_in[0].astype(BF16), hg_lb_logits.astype(F32))
    o_hg = _hgrn2(r3(q), r3(k), r3(lf), r3(v), r3(g), hg_norm_gain[0].reshape(1, HG_WIDTH))
    o_sb = _stickbreak(r3(sq), r3(sk), r3(sv), sb_norm_gain[0].reshape(1, SB_WIDTH))
    h_mid, u, idx_t, rank_t, gates, cnt = _mix_router(
        o_hg.reshape(n, HG_WIDTH), o_sb.reshape(n, SB_WIDTH), x2, w_out[0].astype(BF16),
        norm_ffn_gain[0].reshape(1, d), w_router[0].T.astype(F32),
        b_router[0].reshape(N_EXPERTS, 1).astype(F32))
    counts = cnt[:, 0]
    start_pad, block_e, n_slots = _routing_tables(counts, n * TOP_K)
    dest_t = _dest(start_pad, idx_t, rank_t)
    slot_tok = _slot_token(counts, start_pad, dest_t, n_slots)
    y_disp = _moe(block_e, slot_tok, u,
                  w_gate[0].astype(BF16), b_gate[0].reshape(N_EXPERTS, 1, f),
                  w_up[0].astype(BF16), b_up[0].reshape(N_EXPERTS, 1, f),
                  w_down[0].astype(BF16), b_down[0].reshape(N_EXPERTS, 1, d))
    out = _combine(dest_t, y_disp, h_mid, gates, norm_final_gain.reshape(1, d))
    return out.reshape(b, s, d).astype(x.dtype)
```

```python
import functools

import jax
import jax.numpy as jnp
from jax import lax
from jax.experimental import pallas as pl
from jax.experimental.pallas import tpu as pltpu

F32 = jnp.float32
BF16 = jnp.bfloat16
I32 = jnp.int32

NORM_EPS = 1e-5
HG_HEADS = 4
HG_DK = 128
HG_WIDTH = HG_HEADS * HG_DK
HG_CHUNK = 64
HG_LEVELS = (32, 16, 8)
HG_DIAG = 8
SB_HEADS = 8
SB_DH = 64
SB_WIDTH = SB_HEADS * SB_DH
SB_BLOCK = 128
SB_STATIC_BLOCKS = 3
SB_SKIP_THRESHOLD = 104.0
N_EXPERTS = 32
TOP_K = 4
MOE_BLOCK = 256
SWIGLU_LIMIT = 7.0
SWIGLU_ALPHA = 1.702
NEG_BIG = -1e30

V7X_VMEM_LIMIT_BYTES = 56 * 1024 * 1024
ROW_TILE = 8
LANES = 128
D_MODEL = ROW_TILE * LANES


def _cparams(n_axes, vmem_bytes=None):
    return pltpu.CompilerParams(
        dimension_semantics=("arbitrary",) * n_axes,
        vmem_limit_bytes=vmem_bytes,
    )


def _sigmoid(x):
    return 1.0 / (1.0 + jnp.exp(-x))


def _split2(x):
    hi = x.astype(BF16)
    lo = (x - hi.astype(F32)).astype(BF16)
    return hi, lo


def _split3(x):
    hi = x.astype(BF16)
    r = x - hi.astype(F32)
    mid = r.astype(BF16)
    lo = (r - mid.astype(F32)).astype(BF16)
    return hi, mid, lo


def _dot(a, b):
    return jnp.dot(a, b, preferred_element_type=F32)


def _store_row_tiles(ref, x, base=0):
    rows = x.shape[0]
    for j in range(ROW_TILE):
        ref[pl.ds(base * ROW_TILE + j, rows, stride=ROW_TILE), :] = x[:, j * LANES:(j + 1) * LANES]


def _load_row_tiles(ref, rows, base=0, chunk=None):
    if chunk is not None:
        return ref[pl.ds(base * ROW_TILE + chunk, rows, stride=ROW_TILE), :]
    return jnp.concatenate(
        [ref[pl.ds(base * ROW_TILE + j, rows, stride=ROW_TILE), :] for j in range(ROW_TILE)], axis=-1)


def _dot_nt(a, b):
    return lax.dot_general(a, b, (((1,), (1,)), ((), ())), preferred_element_type=F32)


def _dot_tn(a, b):
    return lax.dot_general(a, b, (((0,), (0,)), ((), ())), preferred_element_type=F32)


def _in_proj_kernel(x_ref, gain_ref, w_ref, lbl_ref,
                    q_ref, k_ref, lf_ref, v_ref, g_ref, sq_ref, sk_ref, sv_ref):
    x = x_ref[...]
    var = jnp.mean(x * x, axis=-1, keepdims=True)
    u = (x * lax.rsqrt(var + NORM_EPS) * gain_ref[...]).astype(BF16)

    lbl = lbl_ref[...]
    mx = jnp.max(lbl, axis=0, keepdims=True)
    ex = jnp.exp(lbl - mx)
    lb = ex[0:1, :] / jnp.sum(ex, axis=0, keepdims=True)

    def seg(i):
        return _dot(u, w_ref[:, i * HG_WIDTH:(i + 1) * HG_WIDTH])

    hq = seg(0)
    q_ref[...] = hq * _sigmoid(hq)
    f_sig = _sigmoid(seg(1))
    lf_ref[...] = jnp.log(lb + (1.0 - lb) * f_sig)
    k_ref[...] = (1.0 - lb) * (1.0 - f_sig)
    v_ref[...] = seg(2)
    hg = seg(3)
    g_ref[...] = hg * _sigmoid(hg)
    sq_ref[...] = (seg(4) * (SB_DH ** -0.5)).astype(BF16)
    sk_ref[...] = seg(5).astype(BF16)
    sv_ref[...] = seg(6).astype(BF16)


def _in_proj(x2, gain, w_in_bf, lb_logits, tm=512):
    n, d = x2.shape
    cols = w_in_bf.shape[1]
    row = lambda i: (i, 0)
    fix = lambda i: (0, 0)
    o_f32 = jax.ShapeDtypeStruct((n, HG_WIDTH), F32)
    o_bf = jax.ShapeDtypeStruct((n, SB_WIDTH), BF16)
    return pl.pallas_call(
        _in_proj_kernel,
        grid=(n // tm,),
        in_specs=[
            pl.BlockSpec((tm, d), row),
            pl.BlockSpec((1, d), fix),
            pl.BlockSpec((d, cols), fix),
            pl.BlockSpec(lb_logits.shape, fix),
        ],
        out_specs=[pl.BlockSpec((tm, HG_WIDTH), row)] * 8,
        out_shape=[o_f32] * 5 + [o_bf] * 3,
        compiler_params=_cparams(1, V7X_VMEM_LIMIT_BYTES),
        name="in_proj",
    )(x2, gain, w_in_bf, lb_logits)


def _hgrn2_consts():
    c = HG_CHUNK
    t = jnp.arange(c)[:, None]
    s = jnp.arange(c)[None, :]
    mats = [(s <= t)]
    for lv in HG_LEVELS:
        ref = (t // (2 * lv)) * (2 * lv) + lv - 1
        mats.append(s <= ref)
    return jnp.concatenate(mats, axis=0).astype(BF16)


def _hgrn2_kernel(q_ref, k_ref, lf_ref, v_ref, g_ref, gain_ref, cmat_ref, *refs, ts, n_cast):
    c = HG_CHUNK
    w = HG_WIDTH
    cast_in, o_ref = refs[:n_cast], refs[n_cast]
    cast_out, st_ref = refs[n_cast + 1:2 * n_cast + 1], refs[2 * n_cast + 1]
    for src, dst in zip(cast_in, cast_out):
        dst[...] = src[...].astype(dst.dtype)

    @pl.when(pl.program_id(1) == 0)
    def _():
        st_ref[...] = jnp.zeros_like(st_ref)

    row_w = lax.broadcasted_iota(I32, (c, w), 0)
    row_c = lax.broadcasted_iota(I32, (c, c), 0)
    col_c = lax.broadcasted_iota(I32, (c, c), 1)
    row_d = lax.broadcasted_iota(I32, (HG_DIAG, w), 0)
    dk_bits = HG_DK.bit_length() - 1
    bd = ((lax.broadcasted_iota(I32, (w, w), 0) >> dk_bits)
          == (lax.broadcasted_iota(I32, (w, w), 1) >> dk_bits)).astype(BF16)
    cmat = cmat_ref[...]
    gain = gain_ref[...]

    def chunk(ci, carry):
        r0 = pl.multiple_of(ci * c, c)
        q = q_ref[0, pl.ds(r0, c), :]
        kk = k_ref[0, pl.ds(r0, c), :]
        lf = lf_ref[0, pl.ds(r0, c), :]
        v = v_ref[0, pl.ds(r0, c), :]
        g = g_ref[0, pl.ds(r0, c), :]

        lf_h, lf_l = _split2(lf)
        gg = _dot(cmat, lf_h) + _dot(cmat, lf_l)
        G = gg[0:c]

        scores = [jnp.zeros((c, c), F32) for _ in range(HG_HEADS)]
        for li, lv in enumerate(HG_LEVELS):
            gref = gg[(li + 1) * c:(li + 2) * c]
            is_q = (row_w & (2 * lv - 1)) >= lv
            e = jnp.exp(jnp.where(is_q, G - gref, gref - G))
            ql = jnp.where(is_q, q * e, 0.0).astype(BF16)
            kl = jnp.where(is_q, 0.0, kk * e).astype(BF16)
            grp_bits = (2 * lv).bit_length() - 1
            same = (row_c >> grp_bits) == (col_c >> grp_bits)
            for h in range(HG_HEADS):
                sl = slice(h * HG_DK, (h + 1) * HG_DK)
                scores[h] = scores[h] + jnp.where(same, _dot_nt(ql[:, sl], kl[:, sl]), 0.0)

        tiles = []
        for b in range(c // HG_DIAG):
            rs = slice(b * HG_DIAG, (b + 1) * HG_DIAG)
            gb, qb, kb = G[rs], q[rs], kk[rs]
            for s in range(HG_DIAG):
                gs = jnp.broadcast_to(gb[s:s + 1, :], (HG_DIAG, w))
                ks = jnp.broadcast_to(kb[s:s + 1, :], (HG_DIAG, w))
                e = jnp.exp(jnp.where(row_d >= s, gb - gs, NEG_BIG))
                tiles.append(qb * e * ks)
        p_all = jnp.concatenate(tiles, axis=0)
        r_all = _dot(p_all.astype(BF16), bd)
        o_blocks = []
        for b in range(c // HG_DIAG):
            vb = v[b * HG_DIAG:(b + 1) * HG_DIAG]
            ob = jnp.zeros((HG_DIAG, w), F32)
            for s in range(HG_DIAG):
                i0 = (b * HG_DIAG + s) * HG_DIAG
                vs = jnp.broadcast_to(vb[s:s + 1, :], (HG_DIAG, w))
                ob = ob + r_all[i0:i0 + HG_DIAG] * vs
            o_blocks.append(ob)
        o = jnp.concatenate(o_blocks, axis=0)

        qg = (q * jnp.exp(G)).astype(BF16)
        g_last = jnp.broadcast_to(G[c - 1:c, :], (c, w))
        kh = (kk * jnp.exp(g_last - G)).astype(BF16)
        dec = jnp.exp(G[c - 1:c, :])
        v_bf = v.astype(BF16)
        outs = []
        for h in range(HG_HEADS):
            sl = slice(h * HG_DK, (h + 1) * HG_DK)
            st = st_ref[h]
            oh = (o[:, sl]
                  + _dot(scores[h].astype(BF16), v_bf[:, sl])
                  + _dot_nt(qg[:, sl], st.astype(BF16)))
            st_ref[h] = st * dec[:, sl] + _dot_tn(v_bf[:, sl], kh[:, sl])
            var = jnp.mean(oh * oh, axis=-1, keepdims=True)
            outs.append(oh * lax.rsqrt(var + NORM_EPS))
        on = jnp.concatenate(outs, axis=-1) * gain * g
        o_ref[0, pl.ds(r0, c), :] = on.astype(o_ref.dtype)
        return carry

    lax.fori_loop(0, ts // c, chunk, 0)


CAST_BLOCK_BYTES_MAX = 2 * 1024 * 1024


def _hgrn2(q, k, lf, v, g, gain, cast_f32=(), ts=256):
    b, s, w = q.shape
    n_s = s // ts
    blk = lambda bi, si: (bi, si, 0)
    fix = lambda bi, si: (0, 0)
    cmat = _hgrn2_consts()
    n_steps = b * n_s
    flat = [a.reshape(-1, a.shape[-1]) for a in cast_f32]
    rows = [a.shape[0] // n_steps for a in flat]
    riding = all(a.shape[0] % n_steps == 0 and r % 16 == 0 and r * a.shape[1] * 4 <= CAST_BLOCK_BYTES_MAX
                 for a, r in zip(flat, rows))
    if not riding:
        flat, rows = [], []
    cast_specs = [pl.BlockSpec((r, a.shape[1]), lambda bi, si: (bi * n_s + si, 0)) for a, r in zip(flat, rows)]
    outs = pl.pallas_call(
        functools.partial(_hgrn2_kernel, ts=ts, n_cast=len(flat)),
        grid=(b, n_s),
        in_specs=[pl.BlockSpec((1, ts, w), blk)] * 5 + [
            pl.BlockSpec((1, w), fix),
            pl.BlockSpec(cmat.shape, fix),
        ] + cast_specs,
        out_specs=[pl.BlockSpec((1, ts, w), blk)] + cast_specs,
        out_shape=[jax.ShapeDtypeStruct((b, s, w), BF16)]
        + [jax.ShapeDtypeStruct(a.shape, BF16) for a in flat],
        scratch_shapes=[pltpu.VMEM((HG_HEADS, HG_DK, HG_DK), F32)],
        compiler_params=_cparams(2, V7X_VMEM_LIMIT_BYTES),
        name="hgrn2",
    )(q, k, lf, v, g, gain, cmat, *flat)
    if riding:
        casted = [o.reshape(a.shape) for o, a in zip(outs[1:], cast_f32)]
    else:
        casted = [a.astype(BF16) for a in cast_f32]
    return outs[0], casted


def _sb_kernel(q_ref, k0_ref, k1_ref, k2_ref, v0_ref, v1_ref, v2_ref, gain_ref, kall_ref, vall_ref,
               o_ref, acc_ref, out_ref, kbuf_ref, vbuf_ref, sem_ref):
    tb = SB_BLOCK
    bi = pl.program_id(0)
    qi = pl.program_id(1)
    n_pairs = SB_HEADS // 2
    pair_w = 2 * SB_DH

    acc_ref[...] = jnp.zeros_like(acc_ref)
    out_ref[...] = jnp.zeros_like(out_ref)

    t_io = lax.broadcasted_iota(I32, (SB_HEADS * tb, tb), 0) & (tb - 1)
    s_io = lax.broadcasted_iota(I32, (SB_HEADS * tb, tb), 1)
    causal = s_io < t_io
    u_row = lax.broadcasted_iota(I32, (tb, 2 * tb), 0)
    u_col = lax.broadcasted_iota(I32, (tb, 2 * tb), 1)
    um = jnp.where(jnp.logical_or(u_col >= tb, u_row > u_col), 1.0, 0.0).astype(BF16)
    lane = lax.broadcasted_iota(I32, (tb, pair_w), 1)
    lo_half = lane < SB_DH
    keeps = (lo_half, jnp.logical_not(lo_half))

    def process(blocks):
        pre = []
        for load_k, _, diag, wgt in blocks:
            zs = []
            for p in range(n_pairs):
                sl = slice(p * pair_w, (p + 1) * pair_w)
                q2 = q_ref[0, :, sl]
                k2 = load_k(sl)
                zero = jnp.zeros_like(q2)
                for half in range(2):
                    zs.append(_dot_nt(jnp.where(keeps[half], q2, zero), k2))
            z = jnp.concatenate(zs, axis=0)
            sp_full = jnp.maximum(z, 0.0) + jnp.log(1.0 + jnp.exp(-jnp.abs(z)))
            sp = jnp.where(causal, sp_full, 0.0) if diag else sp_full
            if wgt is not None:
                sp = sp * wgt
            sp_h, sp_l = _split2(sp)
            lt = _dot(sp_h, um) + _dot(sp_l, um)
            pre.append(((z - sp_full) - lt[:, :tb], lt[:, tb:]))
        acc = acc_ref[...]
        for (_, load_v, diag, wgt), (base, total) in zip(blocks, pre):
            a = jnp.exp(base - acc)
            if diag:
                a = jnp.where(causal, a, 0.0)
            if wgt is not None:
                a = a * wgt
            a = a.astype(BF16)
            acc = acc + total
            for p in range(n_pairs):
                sl = slice(p * pair_w, (p + 1) * pair_w)
                v2 = load_v(sl)
                zero = jnp.zeros_like(v2)
                o_pair = jnp.zeros((tb, pair_w), F32)
                for half in range(2):
                    h = 2 * p + half
                    o_pair = o_pair + _dot(a[h * tb:(h + 1) * tb], jnp.where(keeps[half], v2, zero))
                out_ref[:, sl] += o_pair
        acc_ref[...] = acc
        return jnp.min(acc)

    def blocked(ref):
        return lambda sl: ref[0, :, sl]

    def whole(ref):
        return lambda sl: ref[:, sl]

    on1 = jnp.where(qi >= 1, 1.0, 0.0)
    on2 = jnp.where(qi >= 2, 1.0, 0.0)
    m2 = process([
        (blocked(k0_ref), blocked(v0_ref), True, None),
        (blocked(k1_ref), blocked(v1_ref), False, on1),
        (blocked(k2_ref), blocked(v2_ref), False, on2),
    ])

    def cond(cr):
        j, m = cr
        return jnp.logical_and(j >= 0, m <= SB_SKIP_THRESHOLD)

    def body(cr):
        j, _ = cr
        r0 = pl.multiple_of(j * tb, tb)
        ck = pltpu.make_async_copy(kall_ref.at[bi, pl.ds(r0, tb), :], kbuf_ref, sem_ref.at[0])
        cv = pltpu.make_async_copy(vall_ref.at[bi, pl.ds(r0, tb), :], vbuf_ref, sem_ref.at[1])
        ck.start()
        cv.start()
        ck.wait()
        cv.wait()
        return j - 1, process([(whole(kbuf_ref), whole(vbuf_ref), False, None)])

    lax.while_loop(cond, body, (qi - SB_STATIC_BLOCKS, m2))

    o = out_ref[...]
    wd = SB_WIDTH
    dh_bits = SB_DH.bit_length() - 1
    bd = ((lax.broadcasted_iota(I32, (wd, wd), 0) >> dh_bits)
          == (lax.broadcasted_iota(I32, (wd, wd), 1) >> dh_bits)).astype(BF16)
    sq_h, sq_l = _split2(o * o)
    var = (_dot(sq_h, bd) + _dot(sq_l, bd)) * (1.0 / SB_DH)
    o_ref[0] = (o * lax.rsqrt(var + NORM_EPS) * gain_ref[...]).astype(o_ref.dtype)


def _stickbreak(sq, sk, sv, gain):
    b, s, w = sq.shape
    tb = SB_BLOCK
    cur = lambda bi, qi: (bi, qi, 0)
    prev1 = lambda bi, qi: (bi, jnp.maximum(qi - 1, 0), 0)
    prev2 = lambda bi, qi: (bi, jnp.maximum(qi - 2, 0), 0)
    fix = lambda bi, qi: (0, 0)
    blk = (1, tb, w)
    return pl.pallas_call(
        _sb_kernel,
        grid=(b, s // tb),
        in_specs=[
            pl.BlockSpec(blk, cur),
            pl.BlockSpec(blk, cur), pl.BlockSpec(blk, prev1), pl.BlockSpec(blk, prev2),
            pl.BlockSpec(blk, cur), pl.BlockSpec(blk, prev1), pl.BlockSpec(blk, prev2),
            pl.BlockSpec((1, w), fix),
            pl.BlockSpec(memory_space=pl.ANY),
            pl.BlockSpec(memory_space=pl.ANY),
        ],
        out_specs=pl.BlockSpec(blk, cur),
        out_shape=jax.ShapeDtypeStruct((b, s, w), BF16),
        scratch_shapes=[
            pltpu.VMEM((SB_HEADS * tb, tb), F32),
            pltpu.VMEM((tb, w), F32),
            pltpu.VMEM((tb, w), BF16),
            pltpu.VMEM((tb, w), BF16),
            pltpu.SemaphoreType.DMA((2,)),
        ],
        compiler_params=_cparams(2, V7X_VMEM_LIMIT_BYTES),
        name="stickbreak",
    )(sq, sk, sk, sk, sv, sv, sv, gain, sk, sv)


def _mix_router_kernel(ohg_ref, osb_ref, x_ref, wout_ref, gffn_ref, wrt_ref, br_ref,
                       h_ref, u_ref, idx_ref, rank_ref, gate_ref, cnt_ref, carry_ref, *, tm):
    @pl.when(pl.program_id(0) == 0)
    def _():
        carry_ref[...] = jnp.zeros_like(carry_ref)

    h = (x_ref[...]
         + _dot(ohg_ref[...], wout_ref[0:HG_WIDTH, :])
         + _dot(osb_ref[...], wout_ref[HG_WIDTH:HG_WIDTH + SB_WIDTH, :]))
    h_ref[...] = h
    var = jnp.mean(h * h, axis=-1, keepdims=True)
    u = h * lax.rsqrt(var + NORM_EPS) * gffn_ref[...]
    _store_row_tiles(u_ref, u)

    u_h, u_l = _split2(u)
    w_h, w_l = _split2(wrt_ref[...])
    logits = _dot_nt(w_h, u_h) + _dot_nt(w_h, u_l) + _dot_nt(w_l, u_h) + br_ref[...]

    e_io = lax.broadcasted_iota(I32, (N_EXPERTS, tm), 0).astype(F32)
    vals = logits
    member = jnp.zeros((N_EXPERTS, tm), F32)
    top_v, top_i = [], []
    for _ in range(TOP_K):
        m = jnp.max(vals, axis=0, keepdims=True)
        idx = jnp.min(jnp.where(vals == m, e_io, float(N_EXPERTS)), axis=0, keepdims=True)
        sel = e_io == idx
        top_v.append(m)
        top_i.append(idx)
        member = member + jnp.where(sel, 1.0, 0.0)
        vals = jnp.where(sel, -jnp.inf, vals)

    ex = [jnp.exp(tv - top_v[0]) for tv in top_v]
    den = ex[0] + ex[1] + ex[2] + ex[3]
    gates = [e / den for e in ex]

    n_io = lax.broadcasted_iota(I32, (tm, tm), 0)
    m_io = lax.broadcasted_iota(I32, (tm, tm), 1)
    before = jnp.where(n_io < m_io, 1.0, 0.0).astype(BF16)
    carry = carry_ref[...]
    cexcl = _dot(member.astype(BF16), before) + carry
    carry_new = carry + jnp.sum(member, axis=1, keepdims=True)
    carry_ref[...] = carry_new
    cnt_ref[...] = jnp.broadcast_to(carry_new, cnt_ref.shape).astype(I32)

    idx_ref[...] = jnp.concatenate(top_i, axis=0).astype(I32)
    ranks = [jnp.sum(jnp.where(e_io == ti, cexcl, 0.0), axis=0, keepdims=True) for ti in top_i]
    rank_ref[...] = jnp.concatenate(ranks, axis=0).astype(I32)

    r_io = lax.broadcasted_iota(I32, (128, tm), 0)
    gfull = jnp.zeros((128, tm), F32)
    for kk_, gk in enumerate(gates):
        gfull = jnp.where(r_io == kk_, jnp.broadcast_to(gk, (128, tm)), gfull)
    gate_ref[...] = gfull.T


def _mix_router(ohg, osb, x2, w_out_bf, g_ffn, w_router_t, b_router_col, tm=512):
    n, d = x2.shape
    tm = min(tm, n)
    row = lambda i: (i, 0)
    col = lambda i: (0, i)
    fix = lambda i: (0, 0)
    return pl.pallas_call(
        functools.partial(_mix_router_kernel, tm=tm),
        grid=(n // tm,),
        in_specs=[
            pl.BlockSpec((tm, HG_WIDTH), row),
            pl.BlockSpec((tm, SB_WIDTH), row),
            pl.BlockSpec((tm, d), row),
            pl.BlockSpec(w_out_bf.shape, fix),
            pl.BlockSpec((1, d), fix),
            pl.BlockSpec(w_router_t.shape, fix),
            pl.BlockSpec(b_router_col.shape, fix),
        ],
        out_specs=[
            pl.BlockSpec((tm, d), row),
            pl.BlockSpec((tm * ROW_TILE, LANES), row),
            pl.BlockSpec((TOP_K, tm), col),
            pl.BlockSpec((TOP_K, tm), col),
            pl.BlockSpec((tm, 128), row),
            pl.BlockSpec((N_EXPERTS, 128), fix),
        ],
        out_shape=[
            jax.ShapeDtypeStruct((n, d), F32),
            jax.ShapeDtypeStruct((n * ROW_TILE, LANES), F32),
            jax.ShapeDtypeStruct((TOP_K, n), I32),
            jax.ShapeDtypeStruct((TOP_K, n), I32),
            jax.ShapeDtypeStruct((n, 128), F32),
            jax.ShapeDtypeStruct((N_EXPERTS, 128), I32),
        ],
        scratch_shapes=[pltpu.VMEM((N_EXPERTS, 1), F32)],
        compiler_params=_cparams(1, V7X_VMEM_LIMIT_BYTES),
        name="mix_router",
    )(ohg, osb, x2, w_out_bf, g_ffn, w_router_t, b_router_col)


def _dest_kernel(start_ref, idx_ref, rank_ref, dest_ref):
    idx = idx_ref[...]
    base = jnp.zeros(idx.shape, I32)
    for e in range(N_EXPERTS):
        base = jnp.where(idx == e, start_ref[e], base)
    dest_ref[...] = base + rank_ref[...]


def _dest(start_pad, idx_t, rank_t):
    return pl.pallas_call(
        _dest_kernel,
        in_specs=[
            pl.BlockSpec(memory_space=pltpu.SMEM),
            pl.BlockSpec(memory_space=pltpu.VMEM),
            pl.BlockSpec(memory_space=pltpu.VMEM),
        ],
        out_specs=pl.BlockSpec(memory_space=pltpu.VMEM),
        out_shape=jax.ShapeDtypeStruct(idx_t.shape, I32),
        name="dest",
    )(start_pad, idx_t, rank_t)


def _row_tile(ref, row):
    return ref.at[pl.ds(pl.multiple_of(row * ROW_TILE, ROW_TILE), ROW_TILE)]


def _slot_token_kernel(cnt_ref, start_ref, dest_ref, tok_ref, *, tn):
    i = pl.program_id(0)
    n_slots = tok_ref.shape[0]

    @pl.when(i == 0)
    def _():
        def fill(lo, hi):
            def body(r, carry):
                tok_ref[r] = 0
                return carry
            lax.fori_loop(lo, hi, body, 0)

        def per_expert(e, end_prev):
            cnt = cnt_ref[e]
            base = start_ref[e]
            pad_end = base + ((cnt + MOE_BLOCK - 1) // MOE_BLOCK) * MOE_BLOCK
            fill(base + cnt, pad_end)
            return pad_end

        used_end = lax.fori_loop(0, N_EXPERTS, per_expert, 0)
        fill(used_end, n_slots)

    def scatter(n, carry):
        for k in range(TOP_K):
            tok_ref[dest_ref[k, n]] = i * tn + n
        return carry

    lax.fori_loop(0, tn, scatter, 0, unroll=8)


def _slot_token(counts, start_pad, dest_t, n_slots, tn=4096):
    n = dest_t.shape[1]
    tn = min(tn, n)
    grid_spec = pltpu.PrefetchScalarGridSpec(
        num_scalar_prefetch=2,
        grid=(n // tn,),
        in_specs=[pl.BlockSpec((TOP_K, tn), lambda i, c, s: (0, i), memory_space=pltpu.SMEM)],
        out_specs=pl.BlockSpec((n_slots,), lambda i, c, s: (0,), memory_space=pltpu.SMEM),
    )
    return pl.pallas_call(
        functools.partial(_slot_token_kernel, tn=tn),
        grid_spec=grid_spec,
        out_shape=jax.ShapeDtypeStruct((n_slots,), I32),
        compiler_params=_cparams(1),
        name="slot_token",
    )(counts, start_pad, dest_t)


MOE_PAIR = 2
MOE_FF_CHUNK = 256
MOE_ANCHOR_LAG = 1


def _moe_kernel(be_ref, tok0_ref, tokn_ref, u_ref, *refs):
    n_w = 6 * MOE_PAIR
    w_refs, y_ref = refs[:n_w], refs[n_w]
    bufs, sem_ref = refs[n_w + 1:n_w + 1 + MOE_PAIR], refs[n_w + 1 + MOE_PAIR]
    i = pl.program_id(0)
    blk_rows = MOE_BLOCK * ROW_TILE

    def gather(tok_ref, half, rows=range(MOE_BLOCK)):
        for r in rows:
            pltpu.make_async_copy(
                _row_tile(u_ref, tok_ref[0, 0, half * MOE_BLOCK + r]), _row_tile(bufs[half], r),
                sem_ref.at[half]).start(priority=r % 2)

    def wait(half):
        pltpu.make_async_copy(
            u_ref.at[pl.ds(0, blk_rows)], bufs[half].at[pl.ds(0, blk_rows)], sem_ref.at[half]).wait()

    @pl.when(i == 0)
    def _():
        for half in range(MOE_PAIR):
            bufs[half][blk_rows:blk_rows + ROW_TILE, :] = jnp.zeros((ROW_TILE, LANES), F32)
            gather(tok0_ref, half)

    for half in range(MOE_PAIR):
        wg_ref, bg_ref, wu_ref, bu_ref, wd_ref, bd_ref = w_refs[6 * half:6 * half + 6]
        wait(half)
        x = _load_row_tiles(bufs[half], MOE_BLOCK).astype(BF16)
        f = wg_ref.shape[2]
        n_chunks = f // MOE_FF_CHUNK
        up_per = (MOE_BLOCK * 7 // 8) // (2 * n_chunks)
        n_down = max(n_chunks // 2, 1)
        sizes = [up_per] * (2 * n_chunks)
        rest = MOE_BLOCK - sum(sizes)
        sizes += [rest // n_down + (1 if g < rest % n_down else 0) for g in range(n_down)]
        sizes += [0] * (n_chunks - n_down)
        bounds = [sum(sizes[:g]) for g in range(len(sizes) + 1)]
        assert bounds[-1] == MOE_BLOCK
        groups = iter(range(bounds[g], bounds[g + 1]) for g in range(len(sizes)))
        def anchor():
            off = jnp.minimum(tokn_ref[0, 0, 0], 0) * ROW_TILE
            spare = bufs[half][pl.ds(pl.multiple_of(blk_rows + off, ROW_TILE), ROW_TILE), :]
            z1 = spare[0:1, :]
            return jnp.concatenate([z1] * (MOE_FF_CHUNK // LANES), axis=-1)

        zs = [None] * MOE_ANCHOR_LAG

        def step_anchor():
            gather(tokn_ref, half, next(groups))
            zs.append(anchor())
            return zs.pop(0)

        def plus(a, z):
            return a if z is None else a + z

        acts = []
        z = None
        for c in range(n_chunks):
            sl = slice(c * MOE_FF_CHUNK, (c + 1) * MOE_FF_CHUNK)
            hg = plus(_dot(x, wg_ref[0, :, sl]) + bg_ref[0, :, sl], z)
            z = step_anchor()
            hu = plus(_dot(x, wu_ref[0, :, sl]) + bu_ref[0, :, sl], z)
            z = step_anchor()
            hg = jnp.minimum(hg, SWIGLU_LIMIT)
            hu = jnp.clip(hu, -SWIGLU_LIMIT, SWIGLU_LIMIT)
            glu = hg * _sigmoid(SWIGLU_ALPHA * hg)
            acts.append(((hu + 1.0) * glu).astype(BF16))
        y = bd_ref[0]
        for c in range(n_chunks):
            y = y + _dot(acts[c], wd_ref[0, c * MOE_FF_CHUNK:(c + 1) * MOE_FF_CHUNK, :])
            if z is not None:
                y = y + jnp.concatenate([z] * (y.shape[1] // z.shape[1]), axis=-1)
            z = step_anchor()
        _store_row_tiles(y_ref, y, base=half * MOE_BLOCK)

    @pl.when(i == pl.num_programs(0) - 1)
    def _():
        for half in range(MOE_PAIR):
            wait(half)


def _moe(block_e, slot_tok, u_rt, wg, bg, wu, bu, wd, bd):
    d, f = wg.shape[1], wg.shape[2]
    n_slots = slot_tok.shape[0]
    step_rows = MOE_PAIR * MOE_BLOCK
    n_steps = n_slots // step_rows
    tok3 = slot_tok.reshape(n_steps, 1, step_rows)
    w_specs, w_args = [], []
    for half in range(MOE_PAIR):
        wmap = lambda i, be, half=half: (be[MOE_PAIR * i + half], 0, 0)
        w_specs += [pl.BlockSpec((1, d, f), wmap), pl.BlockSpec((1, 1, f), wmap),
                    pl.BlockSpec((1, d, f), wmap), pl.BlockSpec((1, 1, f), wmap),
                    pl.BlockSpec((1, f, d), wmap), pl.BlockSpec((1, 1, d), wmap)]
        w_args += [wg, bg, wu, bu, wd, bd]
    grid_spec = pltpu.PrefetchScalarGridSpec(
        num_scalar_prefetch=1,
        grid=(n_steps,),
        in_specs=[
            pl.BlockSpec((1, 1, step_rows), lambda i, be: (0, 0, 0), memory_space=pltpu.SMEM),
            pl.BlockSpec((1, 1, step_rows), lambda i, be: (jnp.minimum(i + 1, n_steps - 1), 0, 0),
                         memory_space=pltpu.SMEM),
            pl.BlockSpec(memory_space=pl.ANY),
        ] + w_specs,
        out_specs=pl.BlockSpec((step_rows * ROW_TILE, LANES), lambda i, be: (i, 0)),
        scratch_shapes=[pltpu.VMEM(((MOE_BLOCK + 1) * ROW_TILE, LANES), F32)] * MOE_PAIR
        + [pltpu.SemaphoreType.DMA((MOE_PAIR,))],
    )
    return pl.pallas_call(
        _moe_kernel,
        grid_spec=grid_spec,
        out_shape=jax.ShapeDtypeStruct((n_slots * ROW_TILE, LANES), F32),
        compiler_params=_cparams(1, V7X_VMEM_LIMIT_BYTES),
        name="moe",
    )(block_e, tok3, tok3, u_rt, *w_args)


def _combine_kernel(dest_ref, y_ref, h_ref, gate_ref, gain_ref, o_ref, ybuf_ref, sem_ref, *, tm):
    def issue(n, carry):
        for k in range(TOP_K):
            pltpu.make_async_copy(
                _row_tile(y_ref, dest_ref[k, n]), _row_tile(ybuf_ref, k * tm + n), sem_ref
            ).start(priority=k % 2)
        return carry

    lax.fori_loop(0, tm, issue, 0, unroll=8)
    pltpu.make_async_copy(y_ref.at[pl.ds(0, TOP_K * tm * ROW_TILE)], ybuf_ref, sem_ref).wait()

    gate = gate_ref[...]
    h = h_ref[...]
    chunks = []
    for j in range(ROW_TILE):
        c = h[:, j * LANES:(j + 1) * LANES]
        for k in range(TOP_K):
            c = c + gate[:, k:k + 1] * _load_row_tiles(ybuf_ref, tm, base=k * tm, chunk=j)
        chunks.append(c)
    acc = jnp.concatenate(chunks, axis=-1)
    var = jnp.mean(acc * acc, axis=-1, keepdims=True)
    o_ref[...] = (acc * lax.rsqrt(var + NORM_EPS) * gain_ref[...]).astype(o_ref.dtype)


def _combine(dest_t, y_disp, h, gates, gain, tm=256):
    n, d = h.shape
    row = lambda i: (i, 0)
    fix = lambda i: (0, 0)
    return pl.pallas_call(
        functools.partial(_combine_kernel, tm=tm),
        grid=(n // tm,),
        in_specs=[
            pl.BlockSpec((TOP_K, tm), lambda i: (0, i), memory_space=pltpu.SMEM),
            pl.BlockSpec(memory_space=pl.ANY),
            pl.BlockSpec((tm, d), row),
            pl.BlockSpec((tm, 128), row),
            pl.BlockSpec((1, d), fix),
        ],
        out_specs=pl.BlockSpec((tm, d), row),
        out_shape=jax.ShapeDtypeStruct((n, d), F32),
        scratch_shapes=[pltpu.VMEM((TOP_K * tm * ROW_TILE, LANES), F32), pltpu.SemaphoreType.DMA(())],
        compiler_params=_cparams(1, V7X_VMEM_LIMIT_BYTES),
        name="combine",
    )(dest_t, y_disp, h, gates, gain)


def _routing_tables(counts, n_pairs):
    padded = ((counts + MOE_BLOCK - 1) // MOE_BLOCK) * MOE_BLOCK
    cum_pad = jnp.cumsum(padded)
    start_pad = (cum_pad - padded).astype(I32)
    n_slots = ((n_pairs + MOE_BLOCK - 1) // MOE_BLOCK) * MOE_BLOCK + N_EXPERTS * MOE_BLOCK
    nb = n_slots // MOE_BLOCK
    block_start = jnp.arange(nb, dtype=I32) * MOE_BLOCK
    block_e = jnp.minimum(jnp.sum(cum_pad[None, :] <= block_start[:, None], axis=1), N_EXPERTS - 1)
    return start_pad, block_e.astype(I32), n_slots


def kernel(x, w_in, w_out, hg_lb_logits, hg_norm_gain, sb_norm_gain, norm_mix_gain, norm_ffn_gain,
           w_router, b_router, w_gate, b_gate, w_up, b_up, w_down, b_down, norm_final_gain):
    b, s, d = x.shape
    n = b * s
    f = w_gate.shape[-1]
    assert w_in.shape[0] == 1 and hg_lb_logits.shape[0] == 2, "single-layer trunk only"
    x2 = x.reshape(n, d).astype(F32)
    r3 = lambda a: a.reshape(b, s, a.shape[-1])

    q, k, lf, v, g, sq, sk, sv = _in_proj(
        x2, norm_mix_gain[0].reshape(1, d), w_in[0].astype(BF16), hg_lb_logits.astype(F32))
    o_hg, (wg_bf, wu_bf, wd_bf) = _hgrn2(
        r3(q), r3(k), r3(lf), r3(v), r3(g), hg_norm_gain[0].reshape(1, HG_WIDTH),
        cast_f32=(w_gate[0], w_up[0], w_down[0]))
    o_sb = _stickbreak(r3(sq), r3(sk), r3(sv), sb_norm_gain[0].reshape(1, SB_WIDTH))
    h_mid, u, idx_t, rank_t, gates, cnt = _mix_router(
        o_hg.reshape(n, HG_WIDTH), o_sb.reshape(n, SB_WIDTH), x2, w_out[0].astype(BF16),
        norm_ffn_gain[0].reshape(1, d), w_router[0].T.astype(F32),
        b_router[0].reshape(N_EXPERTS, 1).astype(F32))
    counts = cnt[:, 0]
    start_pad, block_e, n_slots = _routing_tables(counts, n * TOP_K)
    dest_t = _dest(start_pad, idx_t, rank_t)
    slot_tok = _slot_token(counts, start_pad, dest_t, n_slots)
    y_disp = _moe(block_e, slot_tok, u,
                  wg_bf, b_gate[0].reshape(N_EXPERTS, 1, f),
                  wu_bf, b_up[0].reshape(N_EXPERTS, 1, f),
                  wd_bf, b_down[0].reshape(N_EXPERTS, 1, d))
    out = _combine(dest_t, y_disp, h_mid, gates, norm_final_gain.reshape(1, d))
    return out.reshape(b, s, d).astype(x.dtype)
```

```python
import functools

import jax
import jax.numpy as jnp
from jax import lax
from jax.experimental import pallas as pl
from jax.experimental.pallas import tpu as pltpu

F32 = jnp.float32
BF16 = jnp.bfloat16
I32 = jnp.int32

NORM_EPS = 1e-5
HG_HEADS = 4
HG_DK = 128
HG_WIDTH = HG_HEADS * HG_DK
HG_CHUNK = 64
HG_LEVELS = (32, 16, 8)
HG_DIAG = 8
SB_HEADS = 8
SB_DH = 64
SB_WIDTH = SB_HEADS * SB_DH
SB_BLOCK = 128
SB_STATIC_BLOCKS = 3
SB_SKIP_THRESHOLD = 104.0
N_EXPERTS = 32
TOP_K = 4
MOE_BLOCK = 256
SWIGLU_LIMIT = 7.0
SWIGLU_ALPHA = 1.702
NEG_BIG = -1e30

V7X_VMEM_LIMIT_BYTES = 56 * 1024 * 1024
ROW_TILE = 8
LANES = 128
D_MODEL = ROW_TILE * LANES


def _cparams(n_axes, vmem_bytes=None):
    return pltpu.CompilerParams(
        dimension_semantics=("arbitrary",) * n_axes,
        vmem_limit_bytes=vmem_bytes,
    )


def _sigmoid(x):
    return 1.0 / (1.0 + jnp.exp(-x))


def _split2(x):
    hi = x.astype(BF16)
    lo = (x - hi.astype(F32)).astype(BF16)
    return hi, lo


def _split3(x):
    hi = x.astype(BF16)
    r = x - hi.astype(F32)
    mid = r.astype(BF16)
    lo = (r - mid.astype(F32)).astype(BF16)
    return hi, mid, lo


def _dot(a, b):
    return jnp.dot(a, b, preferred_element_type=F32)


def _store_row_tiles(ref, x, base=0):
    rows = x.shape[0]
    for j in range(ROW_TILE):
        ref[pl.ds(base * ROW_TILE + j, rows, stride=ROW_TILE), :] = x[:, j * LANES:(j + 1) * LANES]


def _load_row_tiles(ref, rows, base=0, chunk=None):
    if chunk is not None:
        return ref[pl.ds(base * ROW_TILE + chunk, rows, stride=ROW_TILE), :]
    return jnp.concatenate(
        [ref[pl.ds(base * ROW_TILE + j, rows, stride=ROW_TILE), :] for j in range(ROW_TILE)], axis=-1)


def _dot_nt(a, b):
    return lax.dot_general(a, b, (((1,), (1,)), ((), ())), preferred_element_type=F32)


def _dot_tn(a, b):
    return lax.dot_general(a, b, (((0,), (0,)), ((), ())), preferred_element_type=F32)


def _in_proj_kernel(x_ref, gain_ref, w_ref, lbl_ref,
                    q_ref, k_ref, lf_ref, v_ref, g_ref, sq_ref, sk_ref, sv_ref):
    x = x_ref[...]
    var = jnp.mean(x * x, axis=-1, keepdims=True)
    u = (x * lax.rsqrt(var + NORM_EPS) * gain_ref[...]).astype(BF16)

    lbl = lbl_ref[...]
    mx = jnp.max(lbl, axis=0, keepdims=True)
    ex = jnp.exp(lbl - mx)
    lb = ex[0:1, :] / jnp.sum(ex, axis=0, keepdims=True)

    def seg(i):
        return _dot(u, w_ref[:, i * HG_WIDTH:(i + 1) * HG_WIDTH])

    hq = seg(0)
    q_ref[...] = hq * _sigmoid(hq)
    f_sig = _sigmoid(seg(1))
    lf_ref[...] = jnp.log(lb + (1.0 - lb) * f_sig)
    k_ref[...] = (1.0 - lb) * (1.0 - f_sig)
    v_ref[...] = seg(2)
    hg = seg(3)
    g_ref[...] = hg * _sigmoid(hg)
    sq_ref[...] = (seg(4) * (SB_DH ** -0.5)).astype(BF16)
    sk_ref[...] = seg(5).astype(BF16)
    sv_ref[...] = seg(6).astype(BF16)


def _in_proj(x2, gain, w_in_bf, lb_logits, tm=512):
    n, d = x2.shape
    cols = w_in_bf.shape[1]
    row = lambda i: (i, 0)
    fix = lambda i: (0, 0)
    o_f32 = jax.ShapeDtypeStruct((n, HG_WIDTH), F32)
    o_bf = jax.ShapeDtypeStruct((n, SB_WIDTH), BF16)
    return pl.pallas_call(
        _in_proj_kernel,
        grid=(n // tm,),
        in_specs=[
            pl.BlockSpec((tm, d), row),
            pl.BlockSpec((1, d), fix),
            pl.BlockSpec((d, cols), fix),
            pl.BlockSpec(lb_logits.shape, fix),
        ],
        out_specs=[pl.BlockSpec((tm, HG_WIDTH), row)] * 8,
        out_shape=[o_f32] * 5 + [o_bf] * 3,
        compiler_params=_cparams(1, V7X_VMEM_LIMIT_BYTES),
        name="in_proj",
    )(x2, gain, w_in_bf, lb_logits)


def _hgrn2_consts():
    c = HG_CHUNK
    t = jnp.arange(c)[:, None]
    s = jnp.arange(c)[None, :]
    mats = [(s <= t)]
    for lv in HG_LEVELS:
        ref = (t // (2 * lv)) * (2 * lv) + lv - 1
        mats.append(s <= ref)
    return jnp.concatenate(mats, axis=0).astype(BF16)


def _hgrn2_kernel(q_ref, k_ref, lf_ref, v_ref, g_ref, gain_ref, cmat_ref, *refs, ts, n_cast):
    c = HG_CHUNK
    w = HG_WIDTH
    cast_in, o_ref = refs[:n_cast], refs[n_cast]
    cast_out, st_ref = refs[n_cast + 1:2 * n_cast + 1], refs[2 * n_cast + 1]
    for src, dst in zip(cast_in, cast_out):
        dst[...] = src[...].astype(dst.dtype)

    @pl.when(pl.program_id(1) == 0)
    def _():
        st_ref[...] = jnp.zeros_like(st_ref)

    row_w = lax.broadcasted_iota(I32, (c, w), 0)
    row_c = lax.broadcasted_iota(I32, (c, c), 0)
    col_c = lax.broadcasted_iota(I32, (c, c), 1)
    row_d = lax.broadcasted_iota(I32, (HG_DIAG, w), 0)
    dk_bits = HG_DK.bit_length() - 1
    bd = ((lax.broadcasted_iota(I32, (w, w), 0) >> dk_bits)
          == (lax.broadcasted_iota(I32, (w, w), 1) >> dk_bits)).astype(BF16)
    cmat = cmat_ref[...]
    gain = gain_ref[...]

    def chunk(ci, carry):
        r0 = pl.multiple_of(ci * c, c)
        q = q_ref[0, pl.ds(r0, c), :]
        kk = k_ref[0, pl.ds(r0, c), :]
        lf = lf_ref[0, pl.ds(r0, c), :]
        v = v_ref[0, pl.ds(r0, c), :]
        g = g_ref[0, pl.ds(r0, c), :]

        lf_h, lf_l = _split2(lf)
        gg = _dot(cmat, lf_h) + _dot(cmat, lf_l)
        G = gg[0:c]

        scores = [jnp.zeros((c, c), F32) for _ in range(HG_HEADS)]
        for li, lv in enumerate(HG_LEVELS):
            gref = gg[(li + 1) * c:(li + 2) * c]
            is_q = (row_w & (2 * lv - 1)) >= lv
            e = jnp.exp(jnp.where(is_q, G - gref, gref - G))
            ql = jnp.where(is_q, q * e, 0.0).astype(BF16)
            kl = jnp.where(is_q, 0.0, kk * e).astype(BF16)
            grp_bits = (2 * lv).bit_length() - 1
            same = (row_c >> grp_bits) == (col_c >> grp_bits)
            for h in range(HG_HEADS):
                sl = slice(h * HG_DK, (h + 1) * HG_DK)
                scores[h] = scores[h] + jnp.where(same, _dot_nt(ql[:, sl], kl[:, sl]), 0.0)

        tiles = []
        for b in range(c // HG_DIAG):
            rs = slice(b * HG_DIAG, (b + 1) * HG_DIAG)
            gb, qb, kb = G[rs], q[rs], kk[rs]
            for s in range(HG_DIAG):
                gs = jnp.broadcast_to(gb[s:s + 1, :], (HG_DIAG, w))
                ks = jnp.broadcast_to(kb[s:s + 1, :], (HG_DIAG, w))
                e = jnp.exp(jnp.where(row_d >= s, gb - gs, NEG_BIG))
                tiles.append(qb * e * ks)
        p_all = jnp.concatenate(tiles, axis=0)
        r_all = _dot(p_all.astype(BF16), bd)
        o_blocks = []
        for b in range(c // HG_DIAG):
            vb = v[b * HG_DIAG:(b + 1) * HG_DIAG]
            ob = jnp.zeros((HG_DIAG, w), F32)
            for s in range(HG_DIAG):
                i0 = (b * HG_DIAG + s) * HG_DIAG
                vs = jnp.broadcast_to(vb[s:s + 1, :], (HG_DIAG, w))
                ob = ob + r_all[i0:i0 + HG_DIAG] * vs
            o_blocks.append(ob)
        o = jnp.concatenate(o_blocks, axis=0)

        qg = (q * jnp.exp(G)).astype(BF16)
        g_last = jnp.broadcast_to(G[c - 1:c, :], (c, w))
        kh = (kk * jnp.exp(g_last - G)).astype(BF16)
        dec = jnp.exp(G[c - 1:c, :])
        v_bf = v.astype(BF16)
        outs = []
        for h in range(HG_HEADS):
            sl = slice(h * HG_DK, (h + 1) * HG_DK)
            st = st_ref[h]
            oh = (o[:, sl]
                  + _dot(scores[h].astype(BF16), v_bf[:, sl])
                  + _dot_nt(qg[:, sl], st.astype(BF16)))
            st_ref[h] = st * dec[:, sl] + _dot_tn(v_bf[:, sl], kh[:, sl])
            var = jnp.mean(oh * oh, axis=-1, keepdims=True)
            outs.append(oh * lax.rsqrt(var + NORM_EPS))
        on = jnp.concatenate(outs, axis=-1) * gain * g
        o_ref[0, pl.ds(r0, c), :] = on.astype(o_ref.dtype)
        return carry

    lax.fori_loop(0, ts // c, chunk, 0)


CAST_BLOCK_BYTES_MAX = 2 * 1024 * 1024


def _hgrn2(q, k, lf, v, g, gain, cast_f32=(), ts=256):
    b, s, w = q.shape
    n_s = s // ts
    blk = lambda bi, si: (bi, si, 0)
    fix = lambda bi, si: (0, 0)
    cmat = _hgrn2_consts()
    n_steps = b * n_s
    flat = [a.reshape(-1, a.shape[-1]) for a in cast_f32]
    rows = [a.shape[0] // n_steps for a in flat]
    riding = all(a.shape[0] % n_steps == 0 and r % 16 == 0 and r * a.shape[1] * 4 <= CAST_BLOCK_BYTES_MAX
                 for a, r in zip(flat, rows))
    if not riding:
        flat, rows = [], []
    cast_specs = [pl.BlockSpec((r, a.shape[1]), lambda bi, si: (bi * n_s + si, 0)) for a, r in zip(flat, rows)]
    outs = pl.pallas_call(
        functools.partial(_hgrn2_kernel, ts=ts, n_cast=len(flat)),
        grid=(b, n_s),
        in_specs=[pl.BlockSpec((1, ts, w), blk)] * 5 + [
            pl.BlockSpec((1, w), fix),
            pl.BlockSpec(cmat.shape, fix),
        ] + cast_specs,
        out_specs=[pl.BlockSpec((1, ts, w), blk)] + cast_specs,
        out_shape=[jax.ShapeDtypeStruct((b, s, w), BF16)]
        + [jax.ShapeDtypeStruct(a.shape, BF16) for a in flat],
        scratch_shapes=[pltpu.VMEM((HG_HEADS, HG_DK, HG_DK), F32)],
        compiler_params=_cparams(2, V7X_VMEM_LIMIT_BYTES),
        name="hgrn2",
    )(q, k, lf, v, g, gain, cmat, *flat)
    if riding:
        casted = [o.reshape(a.shape) for o, a in zip(outs[1:], cast_f32)]
    else:
        casted = [a.astype(BF16) for a in cast_f32]
    return outs[0], casted


def _sb_kernel(q_ref, k0_ref, k1_ref, k2_ref, v0_ref, v1_ref, v2_ref, gain_ref, kall_ref, vall_ref,
               o_ref, acc_ref, out_ref, kbuf_ref, vbuf_ref, sem_ref):
    tb = SB_BLOCK
    bi = pl.program_id(0)
    qi = pl.program_id(1)
    n_pairs = SB_HEADS // 2
    pair_w = 2 * SB_DH

    acc_ref[...] = jnp.zeros_like(acc_ref)
    out_ref[...] = jnp.zeros_like(out_ref)

    t_io = lax.broadcasted_iota(I32, (SB_HEADS * tb, tb), 0) & (tb - 1)
    s_io = lax.broadcasted_iota(I32, (SB_HEADS * tb, tb), 1)
    causal = s_io < t_io
    u_row = lax.broadcasted_iota(I32, (tb, 2 * tb), 0)
    u_col = lax.broadcasted_iota(I32, (tb, 2 * tb), 1)
    um = jnp.where(jnp.logical_or(u_col >= tb, u_row > u_col), 1.0, 0.0).astype(BF16)
    lane = lax.broadcasted_iota(I32, (tb, pair_w), 1)
    lo_half = lane < SB_DH
    keeps = (lo_half, jnp.logical_not(lo_half))

    def process(blocks):
        pre = []
        for load_k, _, diag, wgt in blocks:
            zs = []
            for p in range(n_pairs):
                sl = slice(p * pair_w, (p + 1) * pair_w)
                q2 = q_ref[0, :, sl]
                k2 = load_k(sl)
                zero = jnp.zeros_like(q2)
                for half in range(2):
                    zs.append(_dot_nt(jnp.where(keeps[half], q2, zero), k2))
            z = jnp.concatenate(zs, axis=0)
            sp_full = jnp.maximum(z, 0.0) + jnp.log(1.0 + jnp.exp(-jnp.abs(z)))
            sp = jnp.where(causal, sp_full, 0.0) if diag else sp_full
            if wgt is not None:
                sp = sp * wgt
            sp_h, sp_l = _split2(sp)
            lt = _dot(sp_h, um) + _dot(sp_l, um)
            pre.append(((z - sp_full) - lt[:, :tb], lt[:, tb:]))
        acc = acc_ref[...]
        for (_, load_v, diag, wgt), (base, total) in zip(blocks, pre):
            a = jnp.exp(base - acc)
            if diag:
                a = jnp.where(causal, a, 0.0)
            if wgt is not None:
                a = a * wgt
            a = a.astype(BF16)
            acc = acc + total
            for p in range(n_pairs):
                sl = slice(p * pair_w, (p + 1) * pair_w)
                v2 = load_v(sl)
                zero = jnp.zeros_like(v2)
                o_pair = jnp.zeros((tb, pair_w), F32)
                for half in range(2):
                    h = 2 * p + half
                    o_pair = o_pair + _dot(a[h * tb:(h + 1) * tb], jnp.where(keeps[half], v2, zero))
                out_ref[:, sl] += o_pair
        acc_ref[...] = acc
        return jnp.min(acc)

    def blocked(ref):
        return lambda sl: ref[0, :, sl]

    def whole(ref):
        return lambda sl: ref[:, sl]

    on1 = jnp.where(qi >= 1, 1.0, 0.0)
    on2 = jnp.where(qi >= 2, 1.0, 0.0)
    m2 = process([
        (blocked(k0_ref), blocked(v0_ref), True, None),
        (blocked(k1_ref), blocked(v1_ref), False, on1),
        (blocked(k2_ref), blocked(v2_ref), False, on2),
    ])

    def cond(cr):
        j, m = cr
        return jnp.logical_and(j >= 0, m <= SB_SKIP_THRESHOLD)

    def body(cr):
        j, _ = cr
        r0 = pl.multiple_of(j * tb, tb)
        ck = pltpu.make_async_copy(kall_ref.at[bi, pl.ds(r0, tb), :], kbuf_ref, sem_ref.at[0])
        cv = pltpu.make_async_copy(vall_ref.at[bi, pl.ds(r0, tb), :], vbuf_ref, sem_ref.at[1])
        ck.start()
        cv.start()
        ck.wait()
        cv.wait()
        return j - 1, process([(whole(kbuf_ref), whole(vbuf_ref), False, None)])

    lax.while_loop(cond, body, (qi - SB_STATIC_BLOCKS, m2))

    o = out_ref[...]
    wd = SB_WIDTH
    dh_bits = SB_DH.bit_length() - 1
    bd = ((lax.broadcasted_iota(I32, (wd, wd), 0) >> dh_bits)
          == (lax.broadcasted_iota(I32, (wd, wd), 1) >> dh_bits)).astype(BF16)
    sq_h, sq_l = _split2(o * o)
    var = (_dot(sq_h, bd) + _dot(sq_l, bd)) * (1.0 / SB_DH)
    o_ref[0] = (o * lax.rsqrt(var + NORM_EPS) * gain_ref[...]).astype(o_ref.dtype)


def _stickbreak(sq, sk, sv, gain):
    b, s, w = sq.shape
    tb = SB_BLOCK
    cur = lambda bi, qi: (bi, qi, 0)
    prev1 = lambda bi, qi: (bi, jnp.maximum(qi - 1, 0), 0)
    prev2 = lambda bi, qi: (bi, jnp.maximum(qi - 2, 0), 0)
    fix = lambda bi, qi: (0, 0)
    blk = (1, tb, w)
    return pl.pallas_call(
        _sb_kernel,
        grid=(b, s // tb),
        in_specs=[
            pl.BlockSpec(blk, cur),
            pl.BlockSpec(blk, cur), pl.BlockSpec(blk, prev1), pl.BlockSpec(blk, prev2),
            pl.BlockSpec(blk, cur), pl.BlockSpec(blk, prev1), pl.BlockSpec(blk, prev2),
            pl.BlockSpec((1, w), fix),
            pl.BlockSpec(memory_space=pl.ANY),
            pl.BlockSpec(memory_space=pl.ANY),
        ],
        out_specs=pl.BlockSpec(blk, cur),
        out_shape=jax.ShapeDtypeStruct((b, s, w), BF16),
        scratch_shapes=[
            pltpu.VMEM((SB_HEADS * tb, tb), F32),
            pltpu.VMEM((tb, w), F32),
            pltpu.VMEM((tb, w), BF16),
            pltpu.VMEM((tb, w), BF16),
            pltpu.SemaphoreType.DMA((2,)),
        ],
        compiler_params=_cparams(2, V7X_VMEM_LIMIT_BYTES),
        name="stickbreak",
    )(sq, sk, sk, sk, sv, sv, sv, gain, sk, sv)


def _mix_router_kernel(ohg_ref, osb_ref, x_ref, wout_ref, gffn_ref, wrt_ref, br_ref,
                       h_ref, u_ref, idx_ref, rank_ref, gate_ref, cnt_ref, *, tm):
    h = (x_ref[...]
         + _dot(ohg_ref[...], wout_ref[0:HG_WIDTH, :])
         + _dot(osb_ref[...], wout_ref[HG_WIDTH:HG_WIDTH + SB_WIDTH, :]))
    h_ref[...] = h
    var = jnp.mean(h * h, axis=-1, keepdims=True)
    u = h * lax.rsqrt(var + NORM_EPS) * gffn_ref[...]
    _store_row_tiles(u_ref, u)

    u_h, u_l = _split2(u)
    w_h, w_l = _split2(wrt_ref[...])
    logits = _dot_nt(w_h, u_h) + _dot_nt(w_h, u_l) + _dot_nt(w_l, u_h) + br_ref[...]

    e_io = lax.broadcasted_iota(I32, (N_EXPERTS, tm), 0).astype(F32)
    vals = logits
    member = jnp.zeros((N_EXPERTS, tm), F32)
    top_v, top_i = [], []
    for _ in range(TOP_K):
        m = jnp.max(vals, axis=0, keepdims=True)
        idx = jnp.min(jnp.where(vals == m, e_io, float(N_EXPERTS)), axis=0, keepdims=True)
        sel = e_io == idx
        top_v.append(m)
        top_i.append(idx)
        member = member + jnp.where(sel, 1.0, 0.0)
        vals = jnp.where(sel, -jnp.inf, vals)

    ex = [jnp.exp(tv - top_v[0]) for tv in top_v]
    den = ex[0] + ex[1] + ex[2] + ex[3]
    gates = [e / den for e in ex]

    n_io = lax.broadcasted_iota(I32, (tm, tm), 0)
    m_io = lax.broadcasted_iota(I32, (tm, tm), 1)
    before = jnp.where(n_io < m_io, 1.0, 0.0).astype(BF16)
    cexcl = _dot(member.astype(BF16), before)
    cnt_ref[0] = jnp.broadcast_to(jnp.sum(member, axis=1, keepdims=True), cnt_ref.shape[1:]).astype(I32)

    idx_ref[...] = jnp.concatenate(top_i, axis=0).astype(I32)
    ranks = [jnp.sum(jnp.where(e_io == ti, cexcl, 0.0), axis=0, keepdims=True) for ti in top_i]
    rank_ref[...] = jnp.concatenate(ranks, axis=0).astype(I32)

    r_io = lax.broadcasted_iota(I32, (128, tm), 0)
    gfull = jnp.zeros((128, tm), F32)
    for kk_, gk in enumerate(gates):
        gfull = jnp.where(r_io == kk_, jnp.broadcast_to(gk, (128, tm)), gfull)
    gate_ref[...] = gfull.T


def _mix_router(ohg, osb, x2, w_out_bf, g_ffn, w_router_t, b_router_col, tm=512):
    n, d = x2.shape
    tm = min(tm, n)
    row = lambda i: (i, 0)
    col = lambda i: (0, i)
    fix = lambda i: (0, 0)
    return pl.pallas_call(
        functools.partial(_mix_router_kernel, tm=tm),
        grid=(n // tm,),
        in_specs=[
            pl.BlockSpec((tm, HG_WIDTH), row),
            pl.BlockSpec((tm, SB_WIDTH), row),
            pl.BlockSpec((tm, d), row),
            pl.BlockSpec(w_out_bf.shape, fix),
            pl.BlockSpec((1, d), fix),
            pl.BlockSpec(w_router_t.shape, fix),
            pl.BlockSpec(b_router_col.shape, fix),
        ],
        out_specs=[
            pl.BlockSpec((tm, d), row),
            pl.BlockSpec((tm * ROW_TILE, LANES), row),
            pl.BlockSpec((TOP_K, tm), col),
            pl.BlockSpec((TOP_K, tm), col),
            pl.BlockSpec((tm, 128), row),
            pl.BlockSpec((1, N_EXPERTS, 128), lambda i: (i, 0, 0)),
        ],
        out_shape=[
            jax.ShapeDtypeStruct((n, d), F32),
            jax.ShapeDtypeStruct((n * ROW_TILE, LANES), F32),
            jax.ShapeDtypeStruct((TOP_K, n), I32),
            jax.ShapeDtypeStruct((TOP_K, n), I32),
            jax.ShapeDtypeStruct((n, 128), F32),
            jax.ShapeDtypeStruct((n // tm, N_EXPERTS, 128), I32),
        ],
        compiler_params=_cparams(1, V7X_VMEM_LIMIT_BYTES),
        name="mix_router",
    )(ohg, osb, x2, w_out_bf, g_ffn, w_router_t, b_router_col)


def _run_sizes(limit):
    sizes = []
    s = 1
    while s <= limit:
        sizes.append(s)
        s *= 2
    return sizes[::-1]


def _row_tile(ref, row):
    return ref.at[pl.ds(pl.multiple_of(row * ROW_TILE, ROW_TILE), ROW_TILE)]


def _row_run(ref, row, size):
    return ref.at[pl.ds(pl.multiple_of(row * ROW_TILE, ROW_TILE), size * ROW_TILE)]


def _run_copies(mt_ref, t, tm, make_copy):
    for e in range(N_EXPERTS):
        m = mt_ref[t, e]
        for size in _run_sizes(tm):
            @pl.when((m & size) != 0)
            def _(size=size, m=m, e=e):
                make_copy(e, m & ~(2 * size - 1), size).start()


def _dispatch_kernel(mt_ref, ot_ref, dt_ref, cnt_ref, start_ref, nused_ref,
                     idx_ref, rank_ref, u_ref, zero_ref, pos_ref, xd_ref, sorted_ref, sem_ref, zsem_ref,
                     *, tm):
    t = pl.program_id(0)
    last = pl.num_programs(0) - 1
    n_pairs = TOP_K * tm
    slot = t % 2
    base = slot * n_pairs

    def drain(s):
        pltpu.make_async_copy(
            _row_run(sorted_ref, 0, n_pairs), _row_run(xd_ref, 0, n_pairs), sem_ref.at[s]).wait()

    idx = idx_ref[...]
    pos = jnp.zeros(idx.shape, I32)
    for e in range(N_EXPERTS):
        pos = jnp.where(idx == e, ot_ref[t, e], pos)
    pos = pos + rank_ref[...]

    r_io = lax.broadcasted_iota(I32, (LANES, tm), 0)
    pfull = jnp.zeros((LANES, tm), F32)
    for k in range(TOP_K):
        pfull = jnp.where(r_io == k, jnp.broadcast_to(pos[k:k + 1, :].astype(F32), (LANES, tm)), pfull)
    pos_ref[...] = pfull.T

    j_io = lax.broadcasted_iota(I32, (n_pairs, tm), 0)
    onehot = jnp.zeros((n_pairs, tm), F32)
    for k in range(TOP_K):
        onehot = onehot + jnp.where(j_io == pos[k:k + 1, :], 1.0, 0.0)
    u = _load_row_tiles(u_ref, tm).astype(BF16)
    srt = _dot(onehot.astype(BF16), u)

    @pl.when(t >= 2)
    def _():
        drain(slot)

    _store_row_tiles(sorted_ref, srt, base=base)
    _run_copies(mt_ref, t, tm, lambda e, done, size: pltpu.make_async_copy(
        _row_run(sorted_ref, base + ot_ref[t, e] + done, size),
        _row_run(xd_ref, dt_ref[t, e] + done, size), sem_ref.at[slot]))

    @pl.when(t == last)
    def _():
        drain(slot)

        @pl.when(t >= 1)
        def _():
            drain(1 - slot)

    @pl.when(t == 0)
    def _():
        def per_expert(e, carry):
            cnt = cnt_ref[e]
            pad = ((cnt + MOE_BLOCK - 1) // MOE_BLOCK) * MOE_BLOCK - cnt
            first = start_ref[e] + cnt
            for size in _run_sizes(MOE_BLOCK // 2):
                @pl.when((pad & size) != 0)
                def _(size=size):
                    done = pad & ~(2 * size - 1)
                    cp = pltpu.make_async_copy(
                        _row_run(zero_ref, 0, size), _row_run(xd_ref, first + done, size), zsem_ref)
                    cp.start()
                    cp.wait()
            return carry

        lax.fori_loop(0, N_EXPERTS, per_expert, 0)

        def zblock(blk, carry):
            cp = pltpu.make_async_copy(zero_ref, _row_run(xd_ref, blk * MOE_BLOCK, MOE_BLOCK), zsem_ref)
            cp.start()
            cp.wait()
            return carry

        lax.fori_loop(nused_ref[0], xd_ref.shape[0] // (MOE_BLOCK * ROW_TILE), zblock, 0)


def _dispatch(tabs, idx_t, lrank_t, u_rt, n_slots, tm):
    n = idx_t.shape[1]
    zero_blk = jnp.zeros((MOE_BLOCK * ROW_TILE, LANES), F32)
    col = lambda t, *_: (0, t)
    grid_spec = pltpu.PrefetchScalarGridSpec(
        num_scalar_prefetch=6,
        grid=(n // tm,),
        in_specs=[
            pl.BlockSpec((TOP_K, tm), col),
            pl.BlockSpec((TOP_K, tm), col),
            pl.BlockSpec((tm * ROW_TILE, LANES), lambda t, *_: (t, 0)),
            pl.BlockSpec(memory_space=pl.ANY),
        ],
        out_specs=[
            pl.BlockSpec((tm, LANES), lambda t, *_: (t, 0)),
            pl.BlockSpec(memory_space=pl.ANY),
        ],
        scratch_shapes=[
            pltpu.VMEM((2 * TOP_K * tm * ROW_TILE, LANES), F32),
            pltpu.SemaphoreType.DMA((2,)),
            pltpu.SemaphoreType.DMA(()),
        ],
    )
    return pl.pallas_call(
        functools.partial(_dispatch_kernel, tm=tm),
        grid_spec=grid_spec,
        out_shape=[
            jax.ShapeDtypeStruct((n, LANES), F32),
            jax.ShapeDtypeStruct((n_slots * ROW_TILE, LANES), F32),
        ],
        compiler_params=_cparams(1, V7X_VMEM_LIMIT_BYTES),
        name="dispatch",
    )(*tabs, idx_t, lrank_t, u_rt, zero_blk)


MOE_PAIR = 2


def _moe_kernel(be_ref, sb_ref, ns_ref, x_ref, *refs):
    n_w = 6 * MOE_PAIR
    w_refs, y_ref = refs[:n_w], refs[n_w]
    i = pl.program_id(0)

    @pl.when(i < ns_ref[0])
    def _():
        for half in range(MOE_PAIR):
            wg_ref, bg_ref, wu_ref, bu_ref, wd_ref, bd_ref = w_refs[6 * half:6 * half + 6]
            x = _load_row_tiles(x_ref, MOE_BLOCK, base=half * MOE_BLOCK).astype(BF16)
            hg = _dot(x, wg_ref[0]) + bg_ref[0]
            hu = _dot(x, wu_ref[0]) + bu_ref[0]
            hg = jnp.minimum(hg, SWIGLU_LIMIT)
            hu = jnp.clip(hu, -SWIGLU_LIMIT, SWIGLU_LIMIT)
            glu = hg * _sigmoid(SWIGLU_ALPHA * hg)
            act = ((hu + 1.0) * glu).astype(BF16)
            _store_row_tiles(y_ref, _dot(act, wd_ref[0]) + bd_ref[0], base=half * MOE_BLOCK)

    @pl.when(i >= ns_ref[0])
    def _():
        y_ref[...] = jnp.zeros_like(y_ref)


def _moe(block_e, step_blk, n_steps_used, x_disp, wg, bg, wu, bu, wd, bd):
    d, f = wg.shape[1], wg.shape[2]
    n_slots = x_disp.shape[0] // ROW_TILE
    step_rows = MOE_PAIR * MOE_BLOCK
    n_steps = n_slots // step_rows
    w_specs, w_args = [], []
    for half in range(MOE_PAIR):
        wmap = lambda i, be, sb, ns, half=half: (be[MOE_PAIR * sb[i] + half], 0, 0)
        w_specs += [pl.BlockSpec((1, d, f), wmap), pl.BlockSpec((1, 1, f), wmap),
                    pl.BlockSpec((1, d, f), wmap), pl.BlockSpec((1, 1, f), wmap),
                    pl.BlockSpec((1, f, d), wmap), pl.BlockSpec((1, 1, d), wmap)]
        w_args += [wg, bg, wu, bu, wd, bd]
    grid_spec = pltpu.PrefetchScalarGridSpec(
        num_scalar_prefetch=3,
        grid=(n_steps,),
        in_specs=[pl.BlockSpec((step_rows * ROW_TILE, LANES), lambda i, be, sb, ns: (sb[i], 0))] + w_specs,
        out_specs=pl.BlockSpec((step_rows * ROW_TILE, LANES), lambda i, be, sb, ns: (i, 0)),
    )
    return pl.pallas_call(
        _moe_kernel,
        grid_spec=grid_spec,
        out_shape=jax.ShapeDtypeStruct((n_slots * ROW_TILE, LANES), F32),
        compiler_params=_cparams(1, V7X_VMEM_LIMIT_BYTES),
        name="moe",
    )(block_e, step_blk, n_steps_used, x_disp, *w_args)


def _combine_kernel(mt_ref, ot_ref, dt_ref, y_ref, pos_ref, h_ref, gate_ref, gain_ref, o_ref,
                    ybuf_ref, sem_ref, *, tm):
    t = pl.program_id(0)
    n_tiles = pl.num_programs(0)
    n_pairs = TOP_K * tm
    slot = t % 2

    def fetch(tt, s):
        _run_copies(mt_ref, tt, tm, lambda e, done, size: pltpu.make_async_copy(
            _row_run(y_ref, dt_ref[tt, e] + done, size),
            _row_run(ybuf_ref, s * n_pairs + ot_ref[tt, e] + done, size), sem_ref.at[s]))

    @pl.when(t == 0)
    def _():
        fetch(0, 0)

    @pl.when(t + 1 < n_tiles)
    def _():
        fetch(t + 1, 1 - slot)

    pltpu.make_async_copy(
        _row_run(y_ref, 0, n_pairs), _row_run(ybuf_ref, 0, n_pairs), sem_ref.at[slot]).wait()

    gate = gate_ref[...]
    pos = pos_ref[...]
    j_io = lax.broadcasted_iota(I32, (tm, n_pairs), 1).astype(F32)
    wmat = jnp.zeros((tm, n_pairs), F32)
    for k in range(TOP_K):
        wmat = wmat + jnp.where(j_io == pos[:, k:k + 1], gate[:, k:k + 1], 0.0)
    ys = _load_row_tiles(ybuf_ref, n_pairs, base=slot * n_pairs).astype(BF16)
    acc = h_ref[...] + _dot(wmat.astype(BF16), ys)
    var = jnp.mean(acc * acc, axis=-1, keepdims=True)
    o_ref[...] = (acc * lax.rsqrt(var + NORM_EPS) * gain_ref[...]).astype(o_ref.dtype)


def _combine(tabs, y_disp, pos_c, h, gates, gain, tm):
    n, d = h.shape
    row = lambda t, *_: (t, 0)
    fix = lambda t, *_: (0, 0)
    grid_spec = pltpu.PrefetchScalarGridSpec(
        num_scalar_prefetch=3,
        grid=(n // tm,),
        in_specs=[
            pl.BlockSpec(memory_space=pl.ANY),
            pl.BlockSpec((tm, LANES), row),
            pl.BlockSpec((tm, d), row),
            pl.BlockSpec((tm, LANES), row),
            pl.BlockSpec((1, d), fix),
        ],
        out_specs=pl.BlockSpec((tm, d), row),
        scratch_shapes=[
            pltpu.VMEM((2 * TOP_K * tm * ROW_TILE, LANES), F32),
            pltpu.SemaphoreType.DMA((2,)),
        ],
    )
    return pl.pallas_call(
        functools.partial(_combine_kernel, tm=tm),
        grid_spec=grid_spec,
        out_shape=jax.ShapeDtypeStruct((n, d), F32),
        compiler_params=_cparams(1, V7X_VMEM_LIMIT_BYTES),
        name="combine",
    )(*tabs[:3], y_disp, pos_c, h, gates, gain)


def _routing_tables(tile_cnt, n_pairs):
    counts = jnp.sum(tile_cnt, axis=0)
    padded = ((counts + MOE_BLOCK - 1) // MOE_BLOCK) * MOE_BLOCK
    cum_pad = jnp.cumsum(padded)
    start_pad = (cum_pad - padded).astype(I32)
    n_slots = ((n_pairs + MOE_BLOCK - 1) // MOE_BLOCK) * MOE_BLOCK + N_EXPERTS * MOE_BLOCK
    n_slots = ((n_slots + MOE_PAIR * MOE_BLOCK - 1) // (MOE_PAIR * MOE_BLOCK)) * (MOE_PAIR * MOE_BLOCK)
    nb = n_slots // MOE_BLOCK
    block_start = jnp.arange(nb, dtype=I32) * MOE_BLOCK
    block_e = jnp.minimum(jnp.sum(cum_pad[None, :] <= block_start[:, None], axis=1), N_EXPERTS - 1)
    n_used = (cum_pad[-1] // MOE_BLOCK).astype(I32)
    n_steps_used = (n_used + MOE_PAIR - 1) // MOE_PAIR
    step_blk = jnp.minimum(jnp.arange(nb // MOE_PAIR, dtype=I32), n_steps_used - 1)
    before_tile = jnp.cumsum(tile_cnt, axis=0) - tile_cnt
    run_dst = (start_pad[None, :] + before_tile).astype(I32)
    run_src = (jnp.cumsum(tile_cnt, axis=1) - tile_cnt).astype(I32)
    tabs = (tile_cnt.astype(I32), run_src, run_dst, counts.astype(I32), start_pad, n_used.reshape(1))
    return tabs, block_e.astype(I32), step_blk, n_steps_used.reshape(1).astype(I32), n_slots


def kernel(x, w_in, w_out, hg_lb_logits, hg_norm_gain, sb_norm_gain, norm_mix_gain, norm_ffn_gain,
           w_router, b_router, w_gate, b_gate, w_up, b_up, w_down, b_down, norm_final_gain):
    b, s, d = x.shape
    n = b * s
    f = w_gate.shape[-1]
    assert w_in.shape[0] == 1 and hg_lb_logits.shape[0] == 2, "single-layer trunk only"
    x2 = x.reshape(n, d).astype(F32)
    r3 = lambda a: a.reshape(b, s, a.shape[-1])

    q, k, lf, v, g, sq, sk, sv = _in_proj(
        x2, norm_mix_gain[0].reshape(1, d), w_in[0].astype(BF16), hg_lb_logits.astype(F32))
    o_hg, (wg_bf, wu_bf, wd_bf) = _hgrn2(
        r3(q), r3(k), r3(lf), r3(v), r3(g), hg_norm_gain[0].reshape(1, HG_WIDTH),
        cast_f32=(w_gate[0], w_up[0], w_down[0]))
    o_sb = _stickbreak(r3(sq), r3(sk), r3(sv), sb_norm_gain[0].reshape(1, SB_WIDTH))
    h_mid, u, idx_t, lrank_t, gates, cnt = _mix_router(
        o_hg.reshape(n, HG_WIDTH), o_sb.reshape(n, SB_WIDTH), x2, w_out[0].astype(BF16),
        norm_ffn_gain[0].reshape(1, d), w_router[0].T.astype(F32),
        b_router[0].reshape(N_EXPERTS, 1).astype(F32))
    tile_cnt = cnt[:, :, 0]
    tabs, block_e, step_blk, n_steps_used, n_slots = _routing_tables(tile_cnt, n * TOP_K)
    tm = n // tile_cnt.shape[0]
    pos_c, x_disp = _dispatch(tabs, idx_t, lrank_t, u, n_slots, tm)
    y_disp = _moe(block_e, step_blk, n_steps_used, x_disp,
                  wg_bf, b_gate[0].reshape(N_EXPERTS, 1, f),
                  wu_bf, b_up[0].reshape(N_EXPERTS, 1, f),
                  wd_bf, b_down[0].reshape(N_EXPERTS, 1, d))
    out = _combine(tabs, y_disp, pos_c, h_mid, gates, norm_final_gain.reshape(1, d), tm)
    return out.reshape(b, s, d).astype(x.dtype)
```

```python
import functools

import jax
import jax.numpy as jnp
from jax import lax
from jax.experimental import pallas as pl
from jax.experimental.pallas import tpu as pltpu

F32 = jnp.float32
BF16 = jnp.bfloat16
I32 = jnp.int32

NORM_EPS = 1e-5
HG_HEADS = 4
HG_DK = 128
HG_WIDTH = HG_HEADS * HG_DK
HG_CHUNK = 64
HG_LEVELS = (32, 16, 8)
HG_DIAG = 8
SB_HEADS = 8
SB_DH = 64
SB_WIDTH = SB_HEADS * SB_DH
SB_BLOCK = 128
SB_STATIC_BLOCKS = 3
SB_SKIP_THRESHOLD = 104.0
N_EXPERTS = 32
TOP_K = 4
MOE_BLOCK = 256
SWIGLU_LIMIT = 7.0
SWIGLU_ALPHA = 1.702
NEG_BIG = -1e30

V7X_VMEM_LIMIT_BYTES = 56 * 1024 * 1024
ROW_TILE = 8
LANES = 128
D_MODEL = ROW_TILE * LANES


def _cparams(n_axes, vmem_bytes=None):
    return pltpu.CompilerParams(
        dimension_semantics=("arbitrary",) * n_axes,
        vmem_limit_bytes=vmem_bytes,
    )


def _sigmoid(x):
    return 1.0 / (1.0 + jnp.exp(-x))


def _split2(x):
    hi = x.astype(BF16)
    lo = (x - hi.astype(F32)).astype(BF16)
    return hi, lo


def _split3(x):
    hi = x.astype(BF16)
    r = x - hi.astype(F32)
    mid = r.astype(BF16)
    lo = (r - mid.astype(F32)).astype(BF16)
    return hi, mid, lo


def _dot(a, b):
    return jnp.dot(a, b, preferred_element_type=F32)


def _store_row_tiles(ref, x, base=0):
    rows = x.shape[0]
    for j in range(ROW_TILE):
        ref[pl.ds(base * ROW_TILE + j, rows, stride=ROW_TILE), :] = x[:, j * LANES:(j + 1) * LANES]


def _load_row_tiles(ref, rows, base=0, chunk=None):
    if chunk is not None:
        return ref[pl.ds(base * ROW_TILE + chunk, rows, stride=ROW_TILE), :]
    return jnp.concatenate(
        [ref[pl.ds(base * ROW_TILE + j, rows, stride=ROW_TILE), :] for j in range(ROW_TILE)], axis=-1)


def _dot_nt(a, b):
    return lax.dot_general(a, b, (((1,), (1,)), ((), ())), preferred_element_type=F32)


def _dot_tn(a, b):
    return lax.dot_general(a, b, (((0,), (0,)), ((), ())), preferred_element_type=F32)


def _in_proj_kernel(x_ref, gain_ref, w_ref, lbl_ref,
                    q_ref, k_ref, lf_ref, v_ref, g_ref, sq_ref, sk_ref, sv_ref):
    x = x_ref[...]
    var = jnp.mean(x * x, axis=-1, keepdims=True)
    u = (x * lax.rsqrt(var + NORM_EPS) * gain_ref[...]).astype(BF16)

    lbl = lbl_ref[...]
    mx = jnp.max(lbl, axis=0, keepdims=True)
    ex = jnp.exp(lbl - mx)
    lb = ex[0:1, :] / jnp.sum(ex, axis=0, keepdims=True)

    def seg(i):
        return _dot(u, w_ref[:, i * HG_WIDTH:(i + 1) * HG_WIDTH])

    hq = seg(0)
    q_ref[...] = hq * _sigmoid(hq)
    f_sig = _sigmoid(seg(1))
    lf_ref[...] = jnp.log(lb + (1.0 - lb) * f_sig)
    k_ref[...] = (1.0 - lb) * (1.0 - f_sig)
    v_ref[...] = seg(2)
    hg = seg(3)
    g_ref[...] = hg * _sigmoid(hg)
    sq_ref[...] = (seg(4) * (SB_DH ** -0.5)).astype(BF16)
    sk_ref[...] = seg(5).astype(BF16)
    sv_ref[...] = seg(6).astype(BF16)


def _in_proj(x2, gain, w_in_bf, lb_logits, tm=512):
    n, d = x2.shape
    cols = w_in_bf.shape[1]
    row = lambda i: (i, 0)
    fix = lambda i: (0, 0)
    o_f32 = jax.ShapeDtypeStruct((n, HG_WIDTH), F32)
    o_bf = jax.ShapeDtypeStruct((n, SB_WIDTH), BF16)
    return pl.pallas_call(
        _in_proj_kernel,
        grid=(n // tm,),
        in_specs=[
            pl.BlockSpec((tm, d), row),
            pl.BlockSpec((1, d), fix),
            pl.BlockSpec((d, cols), fix),
            pl.BlockSpec(lb_logits.shape, fix),
        ],
        out_specs=[pl.BlockSpec((tm, HG_WIDTH), row)] * 8,
        out_shape=[o_f32] * 5 + [o_bf] * 3,
        compiler_params=_cparams(1, V7X_VMEM_LIMIT_BYTES),
        name="in_proj",
    )(x2, gain, w_in_bf, lb_logits)


def _hgrn2_consts():
    c = HG_CHUNK
    t = jnp.arange(c)[:, None]
    s = jnp.arange(c)[None, :]
    mats = [(s <= t)]
    for lv in HG_LEVELS:
        ref = (t // (2 * lv)) * (2 * lv) + lv - 1
        mats.append(s <= ref)
    return jnp.concatenate(mats, axis=0).astype(BF16)


def _hgrn2_kernel(q_ref, k_ref, lf_ref, v_ref, g_ref, gain_ref, cmat_ref, *refs, ts, n_cast):
    c = HG_CHUNK
    w = HG_WIDTH
    cast_in, o_ref = refs[:n_cast], refs[n_cast]
    cast_out, st_ref = refs[n_cast + 1:2 * n_cast + 1], refs[2 * n_cast + 1]
    for src, dst in zip(cast_in, cast_out):
        dst[...] = src[...].astype(dst.dtype)

    @pl.when(pl.program_id(1) == 0)
    def _():
        st_ref[...] = jnp.zeros_like(st_ref)

    row_w = lax.broadcasted_iota(I32, (c, w), 0)
    row_c = lax.broadcasted_iota(I32, (c, c), 0)
    col_c = lax.broadcasted_iota(I32, (c, c), 1)
    row_d = lax.broadcasted_iota(I32, (HG_DIAG, w), 0)
    dk_bits = HG_DK.bit_length() - 1
    bd = ((lax.broadcasted_iota(I32, (w, w), 0) >> dk_bits)
          == (lax.broadcasted_iota(I32, (w, w), 1) >> dk_bits)).astype(BF16)
    cmat = cmat_ref[...]
    gain = gain_ref[...]

    def chunk(ci, carry):
        r0 = pl.multiple_of(ci * c, c)
        q = q_ref[0, pl.ds(r0, c), :]
        kk = k_ref[0, pl.ds(r0, c), :]
        lf = lf_ref[0, pl.ds(r0, c), :]
        v = v_ref[0, pl.ds(r0, c), :]
        g = g_ref[0, pl.ds(r0, c), :]

        lf_h, lf_l = _split2(lf)
        gg = _dot(cmat, lf_h) + _dot(cmat, lf_l)
        G = gg[0:c]

        scores = [jnp.zeros((c, c), F32) for _ in range(HG_HEADS)]
        for li, lv in enumerate(HG_LEVELS):
            gref = gg[(li + 1) * c:(li + 2) * c]
            is_q = (row_w & (2 * lv - 1)) >= lv
            e = jnp.exp(jnp.where(is_q, G - gref, gref - G))
            ql = jnp.where(is_q, q * e, 0.0).astype(BF16)
            kl = jnp.where(is_q, 0.0, kk * e).astype(BF16)
            grp_bits = (2 * lv).bit_length() - 1
            same = (row_c >> grp_bits) == (col_c >> grp_bits)
            for h in range(HG_HEADS):
                sl = slice(h * HG_DK, (h + 1) * HG_DK)
                scores[h] = scores[h] + jnp.where(same, _dot_nt(ql[:, sl], kl[:, sl]), 0.0)

        tiles = []
        for b in range(c // HG_DIAG):
            rs = slice(b * HG_DIAG, (b + 1) * HG_DIAG)
            gb, qb, kb = G[rs], q[rs], kk[rs]
            for s in range(HG_DIAG):
                gs = jnp.broadcast_to(gb[s:s + 1, :], (HG_DIAG, w))
                ks = jnp.broadcast_to(kb[s:s + 1, :], (HG_DIAG, w))
                e = jnp.exp(jnp.where(row_d >= s, gb - gs, NEG_BIG))
                tiles.append(qb * e * ks)
        p_all = jnp.concatenate(tiles, axis=0)
        r_all = _dot(p_all.astype(BF16), bd)
        o_blocks = []
        for b in range(c // HG_DIAG):
            vb = v[b * HG_DIAG:(b + 1) * HG_DIAG]
            ob = jnp.zeros((HG_DIAG, w), F32)
            for s in range(HG_DIAG):
                i0 = (b * HG_DIAG + s) * HG_DIAG
                vs = jnp.broadcast_to(vb[s:s + 1, :], (HG_DIAG, w))
                ob = ob + r_all[i0:i0 + HG_DIAG] * vs
            o_blocks.append(ob)
        o = jnp.concatenate(o_blocks, axis=0)

        qg = (q * jnp.exp(G)).astype(BF16)
        g_last = jnp.broadcast_to(G[c - 1:c, :], (c, w))
        kh = (kk * jnp.exp(g_last - G)).astype(BF16)
        dec = jnp.exp(G[c - 1:c, :])
        v_bf = v.astype(BF16)
        outs = []
        for h in range(HG_HEADS):
            sl = slice(h * HG_DK, (h + 1) * HG_DK)
            st = st_ref[h]
            oh = (o[:, sl]
                  + _dot(scores[h].astype(BF16), v_bf[:, sl])
                  + _dot_nt(qg[:, sl], st.astype(BF16)))
            st_ref[h] = st * dec[:, sl] + _dot_tn(v_bf[:, sl], kh[:, sl])
            var = jnp.mean(oh * oh, axis=-1, keepdims=True)
            outs.append(oh * lax.rsqrt(var + NORM_EPS))
        on = jnp.concatenate(outs, axis=-1) * gain * g
        o_ref[0, pl.ds(r0, c), :] = on.astype(o_ref.dtype)
        return carry

    lax.fori_loop(0, ts // c, chunk, 0)


CAST_BLOCK_BYTES_MAX = 2 * 1024 * 1024


def _hgrn2(q, k, lf, v, g, gain, cast_f32=(), ts=256):
    b, s, w = q.shape
    n_s = s // ts
    blk = lambda bi, si: (bi, si, 0)
    fix = lambda bi, si: (0, 0)
    cmat = _hgrn2_consts()
    n_steps = b * n_s
    flat = [a.reshape(-1, a.shape[-1]) for a in cast_f32]
    rows = [a.shape[0] // n_steps for a in flat]
    riding = all(a.shape[0] % n_steps == 0 and r % 16 == 0 and r * a.shape[1] * 4 <= CAST_BLOCK_BYTES_MAX
                 for a, r in zip(flat, rows))
    if not riding:
        flat, rows = [], []
    cast_specs = [pl.BlockSpec((r, a.shape[1]), lambda bi, si: (bi * n_s + si, 0)) for a, r in zip(flat, rows)]
    outs = pl.pallas_call(
        functools.partial(_hgrn2_kernel, ts=ts, n_cast=len(flat)),
        grid=(b, n_s),
        in_specs=[pl.BlockSpec((1, ts, w), blk)] * 5 + [
            pl.BlockSpec((1, w), fix),
            pl.BlockSpec(cmat.shape, fix),
        ] + cast_specs,
        out_specs=[pl.BlockSpec((1, ts, w), blk)] + cast_specs,
        out_shape=[jax.ShapeDtypeStruct((b, s, w), BF16)]
        + [jax.ShapeDtypeStruct(a.shape, BF16) for a in flat],
        scratch_shapes=[pltpu.VMEM((HG_HEADS, HG_DK, HG_DK), F32)],
        compiler_params=_cparams(2, V7X_VMEM_LIMIT_BYTES),
        name="hgrn2",
    )(q, k, lf, v, g, gain, cmat, *flat)
    if riding:
        casted = [o.reshape(a.shape) for o, a in zip(outs[1:], cast_f32)]
    else:
        casted = [a.astype(BF16) for a in cast_f32]
    return outs[0], casted


def _sb_kernel(q_ref, k0_ref, k1_ref, k2_ref, v0_ref, v1_ref, v2_ref, gain_ref, kall_ref, vall_ref,
               o_ref, acc_ref, out_ref, kbuf_ref, vbuf_ref, sem_ref):
    tb = SB_BLOCK
    bi = pl.program_id(0)
    qi = pl.program_id(1)
    n_pairs = SB_HEADS // 2
    pair_w = 2 * SB_DH

    acc_ref[...] = jnp.zeros_like(acc_ref)
    out_ref[...] = jnp.zeros_like(out_ref)

    t_io = lax.broadcasted_iota(I32, (SB_HEADS * tb, tb), 0) & (tb - 1)
    s_io = lax.broadcasted_iota(I32, (SB_HEADS * tb, tb), 1)
    causal = s_io < t_io
    u_row = lax.broadcasted_iota(I32, (tb, 2 * tb), 0)
    u_col = lax.broadcasted_iota(I32, (tb, 2 * tb), 1)
    um = jnp.where(jnp.logical_or(u_col >= tb, u_row > u_col), 1.0, 0.0).astype(BF16)
    lane = lax.broadcasted_iota(I32, (tb, pair_w), 1)
    lo_half = lane < SB_DH
    keeps = (lo_half, jnp.logical_not(lo_half))

    def process(blocks):
        pre = []
        for load_k, _, diag, wgt in blocks:
            zs = []
            for p in range(n_pairs):
                sl = slice(p * pair_w, (p + 1) * pair_w)
                q2 = q_ref[0, :, sl]
                k2 = load_k(sl)
                zero = jnp.zeros_like(q2)
                for half in range(2):
                    zs.append(_dot_nt(jnp.where(keeps[half], q2, zero), k2))
            z = jnp.concatenate(zs, axis=0)
            sp_full = jnp.maximum(z, 0.0) + jnp.log(1.0 + jnp.exp(-jnp.abs(z)))
            sp = jnp.where(causal, sp_full, 0.0) if diag else sp_full
            if wgt is not None:
                sp = sp * wgt
            sp_h, sp_l = _split2(sp)
            lt = _dot(sp_h, um) + _dot(sp_l, um)
            pre.append(((z - sp_full) - lt[:, :tb], lt[:, tb:]))
        acc = acc_ref[...]
        for (_, load_v, diag, wgt), (base, total) in zip(blocks, pre):
            a = jnp.exp(base - acc)
            if diag:
                a = jnp.where(causal, a, 0.0)
            if wgt is not None:
                a = a * wgt
            a = a.astype(BF16)
            acc = acc + total
            for p in range(n_pairs):
                sl = slice(p * pair_w, (p + 1) * pair_w)
                v2 = load_v(sl)
                zero = jnp.zeros_like(v2)
                o_pair = jnp.zeros((tb, pair_w), F32)
                for half in range(2):
                    h = 2 * p + half
                    o_pair = o_pair + _dot(a[h * tb:(h + 1) * tb], jnp.where(keeps[half], v2, zero))
                out_ref[:, sl] += o_pair
        acc_ref[...] = acc
        return jnp.min(acc)

    def blocked(ref):
        return lambda sl: ref[0, :, sl]

    def whole(ref):
        return lambda sl: ref[:, sl]

    on1 = jnp.where(qi >= 1, 1.0, 0.0)
    on2 = jnp.where(qi >= 2, 1.0, 0.0)
    m2 = process([
        (blocked(k0_ref), blocked(v0_ref), True, None),
        (blocked(k1_ref), blocked(v1_ref), False, on1),
        (blocked(k2_ref), blocked(v2_ref), False, on2),
    ])

    def cond(cr):
        j, m = cr
        return jnp.logical_and(j >= 0, m <= SB_SKIP_THRESHOLD)

    def body(cr):
        j, _ = cr
        r0 = pl.multiple_of(j * tb, tb)
        ck = pltpu.make_async_copy(kall_ref.at[bi, pl.ds(r0, tb), :], kbuf_ref, sem_ref.at[0])
        cv = pltpu.make_async_copy(vall_ref.at[bi, pl.ds(r0, tb), :], vbuf_ref, sem_ref.at[1])
        ck.start()
        cv.start()
        ck.wait()
        cv.wait()
        return j - 1, process([(whole(kbuf_ref), whole(vbuf_ref), False, None)])

    lax.while_loop(cond, body, (qi - SB_STATIC_BLOCKS, m2))

    o = out_ref[...]
    wd = SB_WIDTH
    dh_bits = SB_DH.bit_length() - 1
    bd = ((lax.broadcasted_iota(I32, (wd, wd), 0) >> dh_bits)
          == (lax.broadcasted_iota(I32, (wd, wd), 1) >> dh_bits)).astype(BF16)
    sq_h, sq_l = _split2(o * o)
    var = (_dot(sq_h, bd) + _dot(sq_l, bd)) * (1.0 / SB_DH)
    o_ref[0] = (o * lax.rsqrt(var + NORM_EPS) * gain_ref[...]).astype(o_ref.dtype)


def _stickbreak(sq, sk, sv, gain):
    b, s, w = sq.shape
    tb = SB_BLOCK
    cur = lambda bi, qi: (bi, qi, 0)
    prev1 = lambda bi, qi: (bi, jnp.maximum(qi - 1, 0), 0)
    prev2 = lambda bi, qi: (bi, jnp.maximum(qi - 2, 0), 0)
    fix = lambda bi, qi: (0, 0)
    blk = (1, tb, w)
    return pl.pallas_call(
        _sb_kernel,
        grid=(b, s // tb),
        in_specs=[
            pl.BlockSpec(blk, cur),
            pl.BlockSpec(blk, cur), pl.BlockSpec(blk, prev1), pl.BlockSpec(blk, prev2),
            pl.BlockSpec(blk, cur), pl.BlockSpec(blk, prev1), pl.BlockSpec(blk, prev2),
            pl.BlockSpec((1, w), fix),
            pl.BlockSpec(memory_space=pl.ANY),
            pl.BlockSpec(memory_space=pl.ANY),
        ],
        out_specs=pl.BlockSpec(blk, cur),
        out_shape=jax.ShapeDtypeStruct((b, s, w), BF16),
        scratch_shapes=[
            pltpu.VMEM((SB_HEADS * tb, tb), F32),
            pltpu.VMEM((tb, w), F32),
            pltpu.VMEM((tb, w), BF16),
            pltpu.VMEM((tb, w), BF16),
            pltpu.SemaphoreType.DMA((2,)),
        ],
        compiler_params=_cparams(2, V7X_VMEM_LIMIT_BYTES),
        name="stickbreak",
    )(sq, sk, sk, sk, sv, sv, sv, gain, sk, sv)


def _mix_router_kernel(ohg_ref, osb_ref, x_ref, wout_ref, gffn_ref, wrt_ref, br_ref,
                       h_ref, xs_ref, gate_ref, pos_ref, cnt_ref, *, tm):
    h = (x_ref[...]
         + _dot(ohg_ref[...], wout_ref[0:HG_WIDTH, :])
         + _dot(osb_ref[...], wout_ref[HG_WIDTH:HG_WIDTH + SB_WIDTH, :]))
    h_ref[...] = h
    var = jnp.mean(h * h, axis=-1, keepdims=True)
    u = h * lax.rsqrt(var + NORM_EPS) * gffn_ref[...]

    u_h, u_l = _split2(u)
    w_h, w_l = _split2(wrt_ref[...])
    logits = _dot_nt(w_h, u_h) + _dot_nt(w_h, u_l) + _dot_nt(w_l, u_h) + br_ref[...]

    e_io = lax.broadcasted_iota(I32, (N_EXPERTS, tm), 0).astype(F32)
    vals = logits
    member = jnp.zeros((N_EXPERTS, tm), F32)
    top_v, top_i = [], []
    for _ in range(TOP_K):
        m = jnp.max(vals, axis=0, keepdims=True)
        idx = jnp.min(jnp.where(vals == m, e_io, float(N_EXPERTS)), axis=0, keepdims=True)
        sel = e_io == idx
        top_v.append(m)
        top_i.append(idx)
        member = member + jnp.where(sel, 1.0, 0.0)
        vals = jnp.where(sel, -jnp.inf, vals)

    ex = [jnp.exp(tv - top_v[0]) for tv in top_v]
    den = ex[0] + ex[1] + ex[2] + ex[3]
    gates = [e / den for e in ex]

    n_io = lax.broadcasted_iota(I32, (tm, tm), 0)
    m_io = lax.broadcasted_iota(I32, (tm, tm), 1)
    before = jnp.where(n_io < m_io, 1.0, 0.0).astype(BF16)
    cexcl = _dot(member.astype(BF16), before)
    cnt = jnp.sum(member, axis=1, keepdims=True)
    cnt_ref[0] = jnp.broadcast_to(cnt, cnt_ref.shape[1:]).astype(I32)

    ee_r = lax.broadcasted_iota(I32, (N_EXPERTS, N_EXPERTS), 0)
    ee_c = lax.broadcasted_iota(I32, (N_EXPERTS, N_EXPERTS), 1)
    lower = jnp.where(ee_c < ee_r, 1.0, 0.0).astype(BF16)
    c_h, c_l = _split2(jnp.broadcast_to(cnt, (N_EXPERTS, LANES)))
    run_start = (_dot(lower, c_h) + _dot(lower, c_l))[:, 0:1]
    where_in_tile = cexcl + run_start
    pos = [jnp.sum(jnp.where(e_io == ti, where_in_tile, 0.0), axis=0, keepdims=True) for ti in top_i]

    n_pairs = TOP_K * tm
    j_io = lax.broadcasted_iota(I32, (n_pairs, tm), 0).astype(F32)
    onehot = jnp.zeros((n_pairs, tm), F32)
    for pk in pos:
        onehot = onehot + jnp.where(j_io == pk, 1.0, 0.0)
    _store_row_tiles(xs_ref, _dot(onehot.astype(BF16), u.astype(BF16)))

    r_io = lax.broadcasted_iota(I32, (LANES, tm), 0)
    gfull = jnp.zeros((LANES, tm), F32)
    pfull = jnp.zeros((LANES, tm), F32)
    for kk_ in range(TOP_K):
        gfull = jnp.where(r_io == kk_, jnp.broadcast_to(gates[kk_], (LANES, tm)), gfull)
        pfull = jnp.where(r_io == kk_, jnp.broadcast_to(pos[kk_], (LANES, tm)), pfull)
    gate_ref[...] = gfull.T
    pos_ref[...] = pfull.T


def _mix_router(ohg, osb, x2, w_out_bf, g_ffn, w_router_t, b_router_col, tm=512):
    n, d = x2.shape
    tm = min(tm, n)
    row = lambda i: (i, 0)
    col = lambda i: (0, i)
    fix = lambda i: (0, 0)
    return pl.pallas_call(
        functools.partial(_mix_router_kernel, tm=tm),
        grid=(n // tm,),
        in_specs=[
            pl.BlockSpec((tm, HG_WIDTH), row),
            pl.BlockSpec((tm, SB_WIDTH), row),
            pl.BlockSpec((tm, d), row),
            pl.BlockSpec(w_out_bf.shape, fix),
            pl.BlockSpec((1, d), fix),
            pl.BlockSpec(w_router_t.shape, fix),
            pl.BlockSpec(b_router_col.shape, fix),
        ],
        out_specs=[
            pl.BlockSpec((tm, d), row),
            pl.BlockSpec((TOP_K * tm * ROW_TILE, LANES), row),
            pl.BlockSpec((tm, LANES), row),
            pl.BlockSpec((tm, LANES), row),
            pl.BlockSpec((1, N_EXPERTS, LANES), lambda i: (i, 0, 0)),
        ],
        out_shape=[
            jax.ShapeDtypeStruct((n, d), F32),
            jax.ShapeDtypeStruct((TOP_K * n * ROW_TILE, LANES), F32),
            jax.ShapeDtypeStruct((n, LANES), F32),
            jax.ShapeDtypeStruct((n, LANES), F32),
            jax.ShapeDtypeStruct((n // tm, N_EXPERTS, LANES), I32),
        ],
        compiler_params=_cparams(1, V7X_VMEM_LIMIT_BYTES),
        name="mix_router",
    )(ohg, osb, x2, w_out_bf, g_ffn, w_router_t, b_router_col)


def _run_sizes(limit):
    sizes = []
    s = 1
    while s <= limit:
        sizes.append(s)
        s *= 2
    return sizes[::-1]


def _row_run(ref, row, size):
    return ref.at[pl.ds(pl.multiple_of(row * ROW_TILE, ROW_TILE), size * ROW_TILE)]


def _run_copies(mt_ref, t, tm, make_copy):
    for e in range(N_EXPERTS):
        m = mt_ref[t, e]
        for size in _run_sizes(tm):
            @pl.when((m & size) != 0)
            def _(size=size, m=m, e=e):
                make_copy(e, m & ~(2 * size - 1), size).start()


MOE_PAIR = 2


def _moe_kernel(be_ref, sb_ref, ns_ref, tlo_ref, thi_ref, nv_ref, mt_ref, ot_ref, dt_ref,
                xs_ref, *refs, tile_rows):
    n_w = 6 * MOE_PAIR
    w_refs, y_ref = refs[:n_w], refs[n_w]
    bufs, sem_ref = refs[n_w + 1:n_w + 1 + MOE_PAIR], refs[n_w + 1 + MOE_PAIR]
    i = pl.program_id(0)
    n_used_steps = ns_ref[0]
    sizes = _run_sizes(MOE_BLOCK)

    def gather(b, half):
        e = be_ref[b]
        first_slot = b * MOE_BLOCK

        def per_tile(t, carry):
            run0 = dt_ref[t, e]
            lo = jnp.maximum(run0, first_slot)
            hi = jnp.minimum(run0 + mt_ref[t, e], first_slot + MOE_BLOCK)
            length = jnp.maximum(hi - lo, 0)
            src = t * tile_rows + ot_ref[t, e] + (lo - run0)
            dst = lo - first_slot
            for size in sizes:
                @pl.when((length & size) != 0)
                def _(size=size):
                    done = length & ~(2 * size - 1)
                    pltpu.make_async_copy(
                        _row_run(xs_ref, src + done, size), _row_run(bufs[half], dst + done, size),
                        sem_ref.at[half]).start()
            return carry

        lax.fori_loop(tlo_ref[b], thi_ref[b] + 1, per_tile, 0)

    def wait(b, half):
        valid = nv_ref[b]
        for size in sizes:
            @pl.when((valid & size) != 0)
            def _(size=size):
                pltpu.make_async_copy(
                    _row_run(xs_ref, 0, size), _row_run(bufs[half], 0, size), sem_ref.at[half]).wait()

    @pl.when(i == 0)
    def _():
        for half in range(MOE_PAIR):
            bufs[half][...] = jnp.zeros_like(bufs[half])
            gather(half, half)

    @pl.when(i < n_used_steps)
    def _():
        for half in range(MOE_PAIR):
            wg_ref, bg_ref, wu_ref, bu_ref, wd_ref, bd_ref = w_refs[6 * half:6 * half + 6]
            wait(MOE_PAIR * i + half, half)
            x = _load_row_tiles(bufs[half], MOE_BLOCK).astype(BF16)

            @pl.when(i + 1 < n_used_steps)
            def _(half=half):
                gather(MOE_PAIR * (i + 1) + half, half)

            hg = _dot(x, wg_ref[0]) + bg_ref[0]
            hu = _dot(x, wu_ref[0]) + bu_ref[0]
            hg = jnp.minimum(hg, SWIGLU_LIMIT)
            hu = jnp.clip(hu, -SWIGLU_LIMIT, SWIGLU_LIMIT)
            glu = hg * _sigmoid(SWIGLU_ALPHA * hg)
            act = ((hu + 1.0) * glu).astype(BF16)
            _store_row_tiles(y_ref, _dot(act, wd_ref[0]) + bd_ref[0], base=half * MOE_BLOCK)

    @pl.when(i >= n_used_steps)
    def _():
        y_ref[...] = jnp.zeros_like(y_ref)


def _moe(sched, xs, n_slots, tile_rows, wg, bg, wu, bu, wd, bd):
    d, f = wg.shape[1], wg.shape[2]
    step_rows = MOE_PAIR * MOE_BLOCK
    n_steps = n_slots // step_rows
    w_specs, w_args = [], []
    for half in range(MOE_PAIR):
        wmap = lambda i, be, sb, *_, half=half: (be[MOE_PAIR * sb[i] + half], 0, 0)
        w_specs += [pl.BlockSpec((1, d, f), wmap), pl.BlockSpec((1, 1, f), wmap),
                    pl.BlockSpec((1, d, f), wmap), pl.BlockSpec((1, 1, f), wmap),
                    pl.BlockSpec((1, f, d), wmap), pl.BlockSpec((1, 1, d), wmap)]
        w_args += [wg, bg, wu, bu, wd, bd]
    grid_spec = pltpu.PrefetchScalarGridSpec(
        num_scalar_prefetch=len(sched),
        grid=(n_steps,),
        in_specs=[pl.BlockSpec(memory_space=pl.ANY)] + w_specs,
        out_specs=pl.BlockSpec((step_rows * ROW_TILE, LANES), lambda i, *_: (i, 0)),
        scratch_shapes=[pltpu.VMEM((MOE_BLOCK * ROW_TILE, LANES), F32)] * MOE_PAIR
        + [pltpu.SemaphoreType.DMA((MOE_PAIR,))],
    )
    return pl.pallas_call(
        functools.partial(_moe_kernel, tile_rows=tile_rows),
        grid_spec=grid_spec,
        out_shape=jax.ShapeDtypeStruct((n_slots * ROW_TILE, LANES), F32),
        compiler_params=_cparams(1, V7X_VMEM_LIMIT_BYTES),
        name="moe",
    )(*sched, xs, *w_args)


def _combine_kernel(mt_ref, ot_ref, dt_ref, y_ref, pos_ref, h_ref, gate_ref, gain_ref, o_ref,
                    ybuf_ref, sem_ref, *, tm):
    t = pl.program_id(0)
    n_tiles = pl.num_programs(0)
    n_pairs = TOP_K * tm
    slot = t % 2

    def fetch(tt, s):
        _run_copies(mt_ref, tt, tm, lambda e, done, size: pltpu.make_async_copy(
            _row_run(y_ref, dt_ref[tt, e] + done, size),
            _row_run(ybuf_ref, s * n_pairs + ot_ref[tt, e] + done, size), sem_ref.at[s]))

    @pl.when(t == 0)
    def _():
        fetch(0, 0)

    @pl.when(t + 1 < n_tiles)
    def _():
        fetch(t + 1, 1 - slot)

    pltpu.make_async_copy(
        _row_run(y_ref, 0, n_pairs), _row_run(ybuf_ref, 0, n_pairs), sem_ref.at[slot]).wait()

    gate = gate_ref[...]
    pos = pos_ref[...]
    j_io = lax.broadcasted_iota(I32, (tm, n_pairs), 1).astype(F32)
    wmat = jnp.zeros((tm, n_pairs), F32)
    for k in range(TOP_K):
        wmat = wmat + jnp.where(j_io == pos[:, k:k + 1], gate[:, k:k + 1], 0.0)
    ys = _load_row_tiles(ybuf_ref, n_pairs, base=slot * n_pairs).astype(BF16)
    acc = h_ref[...] + _dot(wmat.astype(BF16), ys)
    var = jnp.mean(acc * acc, axis=-1, keepdims=True)
    o_ref[...] = (acc * lax.rsqrt(var + NORM_EPS) * gain_ref[...]).astype(o_ref.dtype)


def _combine(tabs, y_disp, pos_c, h, gates, gain, tm):
    n, d = h.shape
    row = lambda t, *_: (t, 0)
    fix = lambda t, *_: (0, 0)
    grid_spec = pltpu.PrefetchScalarGridSpec(
        num_scalar_prefetch=3,
        grid=(n // tm,),
        in_specs=[
            pl.BlockSpec(memory_space=pl.ANY),
            pl.BlockSpec((tm, LANES), row),
            pl.BlockSpec((tm, d), row),
            pl.BlockSpec((tm, LANES), row),
            pl.BlockSpec((1, d), fix),
        ],
        out_specs=pl.BlockSpec((tm, d), row),
        scratch_shapes=[
            pltpu.VMEM((2 * TOP_K * tm * ROW_TILE, LANES), F32),
            pltpu.SemaphoreType.DMA((2,)),
        ],
    )
    return pl.pallas_call(
        functools.partial(_combine_kernel, tm=tm),
        grid_spec=grid_spec,
        out_shape=jax.ShapeDtypeStruct((n, d), F32),
        compiler_params=_cparams(1, V7X_VMEM_LIMIT_BYTES),
        name="combine",
    )(*tabs[:3], y_disp, pos_c, h, gates, gain)


def _routing_tables(tile_cnt, n_pairs):
    counts = jnp.sum(tile_cnt, axis=0)
    padded = ((counts + MOE_BLOCK - 1) // MOE_BLOCK) * MOE_BLOCK
    cum_pad = jnp.cumsum(padded)
    start_pad = (cum_pad - padded).astype(I32)
    n_slots = ((n_pairs + MOE_BLOCK - 1) // MOE_BLOCK) * MOE_BLOCK + N_EXPERTS * MOE_BLOCK
    n_slots = ((n_slots + MOE_PAIR * MOE_BLOCK - 1) // (MOE_PAIR * MOE_BLOCK)) * (MOE_PAIR * MOE_BLOCK)
    nb = n_slots // MOE_BLOCK
    block_start = jnp.arange(nb, dtype=I32) * MOE_BLOCK
    block_e = jnp.minimum(jnp.sum(cum_pad[None, :] <= block_start[:, None], axis=1), N_EXPERTS - 1)
    block_e = block_e.astype(I32)
    n_used = (cum_pad[-1] // MOE_BLOCK).astype(I32)
    n_steps_used = (n_used + MOE_PAIR - 1) // MOE_PAIR
    step_blk = jnp.minimum(jnp.arange(nb // MOE_PAIR, dtype=I32), n_steps_used - 1)
    before_tile = jnp.cumsum(tile_cnt, axis=0) - tile_cnt
    run_dst = (start_pad[None, :] + before_tile).astype(I32)
    run_src = (jnp.cumsum(tile_cnt, axis=1) - tile_cnt).astype(I32)
    dst_b = run_dst[:, block_e]
    end_b = dst_b + tile_cnt[:, block_e]
    tile_lo = jnp.sum(end_b <= block_start[None, :], axis=0).astype(I32)
    tile_hi = (jnp.sum(dst_b < block_start[None, :] + MOE_BLOCK, axis=0) - 1).astype(I32)
    used = jnp.arange(nb, dtype=I32) < n_used
    n_valid = jnp.clip((start_pad + counts)[block_e] - block_start, 0, MOE_BLOCK)
    n_valid = jnp.where(used, n_valid, 0).astype(I32)
    tile_hi = jnp.where(used, tile_hi, tile_lo - 1)
    run_tabs = (tile_cnt.astype(I32), run_src, run_dst)
    sched = (block_e, step_blk, n_steps_used.reshape(1).astype(I32), tile_lo, tile_hi, n_valid) + run_tabs
    return run_tabs, sched, n_slots


def kernel(x, w_in, w_out, hg_lb_logits, hg_norm_gain, sb_norm_gain, norm_mix_gain, norm_ffn_gain,
           w_router, b_router, w_gate, b_gate, w_up, b_up, w_down, b_down, norm_final_gain):
    b, s, d = x.shape
    n = b * s
    f = w_gate.shape[-1]
    assert w_in.shape[0] == 1 and hg_lb_logits.shape[0] == 2, "single-layer trunk only"
    x2 = x.reshape(n, d).astype(F32)
    r3 = lambda a: a.reshape(b, s, a.shape[-1])

    q, k, lf, v, g, sq, sk, sv = _in_proj(
        x2, norm_mix_gain[0].reshape(1, d), w_in[0].astype(BF16), hg_lb_logits.astype(F32))
    o_hg, (wg_bf, wu_bf, wd_bf) = _hgrn2(
        r3(q), r3(k), r3(lf), r3(v), r3(g), hg_norm_gain[0].reshape(1, HG_WIDTH),
        cast_f32=(w_gate[0], w_up[0], w_down[0]))
    o_sb = _stickbreak(r3(sq), r3(sk), r3(sv), sb_norm_gain[0].reshape(1, SB_WIDTH))
    h_mid, xs, gates, pos_c, cnt = _mix_router(
        o_hg.reshape(n, HG_WIDTH), o_sb.reshape(n, SB_WIDTH), x2, w_out[0].astype(BF16),
        norm_ffn_gain[0].reshape(1, d), w_router[0].T.astype(F32),
        b_router[0].reshape(N_EXPERTS, 1).astype(F32))
    tile_cnt = cnt[:, :, 0]
    run_tabs, sched, n_slots = _routing_tables(tile_cnt, n * TOP_K)
    tm = n // tile_cnt.shape[0]
    y_disp = _moe(sched, xs, n_slots, TOP_K * tm,
                  wg_bf, b_gate[0].reshape(N_EXPERTS, 1, f),
                  wu_bf, b_up[0].reshape(N_EXPERTS, 1, f),
                  wd_bf, b_down[0].reshape(N_EXPERTS, 1, d))
    out = _combine(run_tabs, y_disp, pos_c, h_mid, gates, norm_final_gain.reshape(1, d), tm)
    return out.reshape(b, s, d).astype(x.dtype)
```

```python
import functools

import jax
import jax.numpy as jnp
from jax import lax
from jax.experimental import pallas as pl
from jax.experimental.pallas import tpu as pltpu

F32 = jnp.float32
BF16 = jnp.bfloat16
I32 = jnp.int32

NORM_EPS = 1e-5
HG_HEADS = 4
HG_DK = 128
HG_WIDTH = HG_HEADS * HG_DK
HG_CHUNK = 64
HG_LEVELS = (32, 16, 8)
HG_DIAG = 8
SB_HEADS = 8
SB_DH = 64
SB_WIDTH = SB_HEADS * SB_DH
SB_BLOCK = 128
SB_STATIC_BLOCKS = 3
SB_SKIP_THRESHOLD = 104.0
N_EXPERTS = 32
TOP_K = 4
MOE_BLOCK = 256
SWIGLU_LIMIT = 7.0
SWIGLU_ALPHA = 1.702
NEG_BIG = -1e30

V7X_VMEM_LIMIT_BYTES = 56 * 1024 * 1024
ROW_TILE = 8
LANES = 128
D_MODEL = ROW_TILE * LANES


def _cparams(n_axes, vmem_bytes=None):
    return pltpu.CompilerParams(
        dimension_semantics=("arbitrary",) * n_axes,
        vmem_limit_bytes=vmem_bytes,
    )


def _sigmoid(x):
    return 1.0 / (1.0 + jnp.exp(-x))


def _split2(x):
    hi = x.astype(BF16)
    lo = (x - hi.astype(F32)).astype(BF16)
    return hi, lo


def _split3(x):
    hi = x.astype(BF16)
    r = x - hi.astype(F32)
    mid = r.astype(BF16)
    lo = (r - mid.astype(F32)).astype(BF16)
    return hi, mid, lo


def _dot(a, b):
    return jnp.dot(a, b, preferred_element_type=F32)


def _store_row_tiles(ref, x, base=0):
    rows = x.shape[0]
    for j in range(ROW_TILE):
        ref[pl.ds(base * ROW_TILE + j, rows, stride=ROW_TILE), :] = x[:, j * LANES:(j + 1) * LANES]


def _load_row_tiles(ref, rows, base=0, chunk=None):
    if chunk is not None:
        return ref[pl.ds(base * ROW_TILE + chunk, rows, stride=ROW_TILE), :]
    return jnp.concatenate(
        [ref[pl.ds(base * ROW_TILE + j, rows, stride=ROW_TILE), :] for j in range(ROW_TILE)], axis=-1)


def _dot_nt(a, b):
    return lax.dot_general(a, b, (((1,), (1,)), ((), ())), preferred_element_type=F32)


def _dot_tn(a, b):
    return lax.dot_general(a, b, (((0,), (0,)), ((), ())), preferred_element_type=F32)


def _in_proj_kernel(x_ref, gain_ref, w_ref, lbl_ref,
                    q_ref, k_ref, lf_ref, v_ref, g_ref, sq_ref, sk_ref, sv_ref):
    x = x_ref[...]
    var = jnp.mean(x * x, axis=-1, keepdims=True)
    u = (x * lax.rsqrt(var + NORM_EPS) * gain_ref[...]).astype(BF16)

    lbl = lbl_ref[...]
    mx = jnp.max(lbl, axis=0, keepdims=True)
    ex = jnp.exp(lbl - mx)
    lb = ex[0:1, :] / jnp.sum(ex, axis=0, keepdims=True)

    def seg(i):
        return _dot(u, w_ref[:, i * HG_WIDTH:(i + 1) * HG_WIDTH])

    hq = seg(0)
    q_ref[...] = hq * _sigmoid(hq)
    f_sig = _sigmoid(seg(1))
    lf_ref[...] = jnp.log(lb + (1.0 - lb) * f_sig)
    k_ref[...] = (1.0 - lb) * (1.0 - f_sig)
    v_ref[...] = seg(2)
    hg = seg(3)
    g_ref[...] = hg * _sigmoid(hg)
    sq_ref[...] = (seg(4) * (SB_DH ** -0.5)).astype(BF16)
    sk_ref[...] = seg(5).astype(BF16)
    sv_ref[...] = seg(6).astype(BF16)


def _in_proj(x2, gain, w_in_bf, lb_logits, tm=512):
    n, d = x2.shape
    cols = w_in_bf.shape[1]
    row = lambda i: (i, 0)
    fix = lambda i: (0, 0)
    o_f32 = jax.ShapeDtypeStruct((n, HG_WIDTH), F32)
    o_bf = jax.ShapeDtypeStruct((n, SB_WIDTH), BF16)
    return pl.pallas_call(
        _in_proj_kernel,
        grid=(n // tm,),
        in_specs=[
            pl.BlockSpec((tm, d), row),
            pl.BlockSpec((1, d), fix),
            pl.BlockSpec((d, cols), fix),
            pl.BlockSpec(lb_logits.shape, fix),
        ],
        out_specs=[pl.BlockSpec((tm, HG_WIDTH), row)] * 8,
        out_shape=[o_f32] * 5 + [o_bf] * 3,
        compiler_params=_cparams(1, V7X_VMEM_LIMIT_BYTES),
        name="in_proj",
    )(x2, gain, w_in_bf, lb_logits)


def _hgrn2_consts():
    c = HG_CHUNK
    t = jnp.arange(c)[:, None]
    s = jnp.arange(c)[None, :]
    mats = [(s <= t)]
    for lv in HG_LEVELS:
        ref = (t // (2 * lv)) * (2 * lv) + lv - 1
        mats.append(s <= ref)
    return jnp.concatenate(mats, axis=0).astype(BF16)


def _hgrn2_kernel(q_ref, k_ref, lf_ref, v_ref, g_ref, gain_ref, cmat_ref, *refs, ts, n_cast):
    c = HG_CHUNK
    w = HG_WIDTH
    cast_in, o_ref = refs[:n_cast], refs[n_cast]
    cast_out, st_ref = refs[n_cast + 1:2 * n_cast + 1], refs[2 * n_cast + 1]
    for src, dst in zip(cast_in, cast_out):
        dst[...] = src[...].astype(dst.dtype)

    @pl.when(pl.program_id(1) == 0)
    def _():
        st_ref[...] = jnp.zeros_like(st_ref)

    row_w = lax.broadcasted_iota(I32, (c, w), 0)
    row_c = lax.broadcasted_iota(I32, (c, c), 0)
    col_c = lax.broadcasted_iota(I32, (c, c), 1)
    row_d = lax.broadcasted_iota(I32, (HG_DIAG, w), 0)
    dk_bits = HG_DK.bit_length() - 1
    bd = ((lax.broadcasted_iota(I32, (w, w), 0) >> dk_bits)
          == (lax.broadcasted_iota(I32, (w, w), 1) >> dk_bits)).astype(BF16)
    cmat = cmat_ref[...]
    gain = gain_ref[...]

    def chunk(ci, carry):
        r0 = pl.multiple_of(ci * c, c)
        q = q_ref[0, pl.ds(r0, c), :]
        kk = k_ref[0, pl.ds(r0, c), :]
        lf = lf_ref[0, pl.ds(r0, c), :]
        v = v_ref[0, pl.ds(r0, c), :]
        g = g_ref[0, pl.ds(r0, c), :]

        lf_h, lf_l = _split2(lf)
        gg = _dot(cmat, lf_h) + _dot(cmat, lf_l)
        G = gg[0:c]

        scores = [jnp.zeros((c, c), F32) for _ in range(HG_HEADS)]
        for li, lv in enumerate(HG_LEVELS):
            gref = gg[(li + 1) * c:(li + 2) * c]
            is_q = (row_w & (2 * lv - 1)) >= lv
            e = jnp.exp(jnp.where(is_q, G - gref, gref - G))
            ql = jnp.where(is_q, q * e, 0.0).astype(BF16)
            kl = jnp.where(is_q, 0.0, kk * e).astype(BF16)
            grp_bits = (2 * lv).bit_length() - 1
            same = (row_c >> grp_bits) == (col_c >> grp_bits)
            for h in range(HG_HEADS):
                sl = slice(h * HG_DK, (h + 1) * HG_DK)
                scores[h] = scores[h] + jnp.where(same, _dot_nt(ql[:, sl], kl[:, sl]), 0.0)

        tiles = []
        for b in range(c // HG_DIAG):
            rs = slice(b * HG_DIAG, (b + 1) * HG_DIAG)
            gb, qb, kb = G[rs], q[rs], kk[rs]
            for s in range(HG_DIAG):
                gs = jnp.broadcast_to(gb[s:s + 1, :], (HG_DIAG, w))
                ks = jnp.broadcast_to(kb[s:s + 1, :], (HG_DIAG, w))
                e = jnp.exp(jnp.where(row_d >= s, gb - gs, NEG_BIG))
                tiles.append(qb * e * ks)
        p_all = jnp.concatenate(tiles, axis=0)
        r_all = _dot(p_all.astype(BF16), bd)
        o_blocks = []
        for b in range(c // HG_DIAG):
            vb = v[b * HG_DIAG:(b + 1) * HG_DIAG]
            ob = jnp.zeros((HG_DIAG, w), F32)
            for s in range(HG_DIAG):
                i0 = (b * HG_DIAG + s) * HG_DIAG
                vs = jnp.broadcast_to(vb[s:s + 1, :], (HG_DIAG, w))
                ob = ob + r_all[i0:i0 + HG_DIAG] * vs
            o_blocks.append(ob)
        o = jnp.concatenate(o_blocks, axis=0)

        qg = (q * jnp.exp(G)).astype(BF16)
        g_last = jnp.broadcast_to(G[c - 1:c, :], (c, w))
        kh = (kk * jnp.exp(g_last - G)).astype(BF16)
        dec = jnp.exp(G[c - 1:c, :])
        v_bf = v.astype(BF16)
        outs = []
        for h in range(HG_HEADS):
            sl = slice(h * HG_DK, (h + 1) * HG_DK)
            st = st_ref[h]
            oh = (o[:, sl]
                  + _dot(scores[h].astype(BF16), v_bf[:, sl])
                  + _dot_nt(qg[:, sl], st.astype(BF16)))
            st_ref[h] = st * dec[:, sl] + _dot_tn(v_bf[:, sl], kh[:, sl])
            var = jnp.mean(oh * oh, axis=-1, keepdims=True)
            outs.append(oh * lax.rsqrt(var + NORM_EPS))
        on = jnp.concatenate(outs, axis=-1) * gain * g
        o_ref[0, pl.ds(r0, c), :] = on.astype(o_ref.dtype)
        return carry

    lax.fori_loop(0, ts // c, chunk, 0, unroll=True)


CAST_BLOCK_BYTES_MAX = 2 * 1024 * 1024


def _hgrn2(q, k, lf, v, g, gain, cast_f32=(), ts=256):
    b, s, w = q.shape
    n_s = s // ts
    blk = lambda bi, si: (bi, si, 0)
    fix = lambda bi, si: (0, 0)
    cmat = _hgrn2_consts()
    n_steps = b * n_s
    flat = [a.reshape(-1, a.shape[-1]) for a in cast_f32]
    rows = [a.shape[0] // n_steps for a in flat]
    riding = all(a.shape[0] % n_steps == 0 and r % 16 == 0 and r * a.shape[1] * 4 <= CAST_BLOCK_BYTES_MAX
                 for a, r in zip(flat, rows))
    if not riding:
        flat, rows = [], []
    cast_specs = [pl.BlockSpec((r, a.shape[1]), lambda bi, si: (bi * n_s + si, 0)) for a, r in zip(flat, rows)]
    outs = pl.pallas_call(
        functools.partial(_hgrn2_kernel, ts=ts, n_cast=len(flat)),
        grid=(b, n_s),
        in_specs=[pl.BlockSpec((1, ts, w), blk)] * 5 + [
            pl.BlockSpec((1, w), fix),
            pl.BlockSpec(cmat.shape, fix),
        ] + cast_specs,
        out_specs=[pl.BlockSpec((1, ts, w), blk)] + cast_specs,
        out_shape=[jax.ShapeDtypeStruct((b, s, w), BF16)]
        + [jax.ShapeDtypeStruct(a.shape, BF16) for a in flat],
        scratch_shapes=[pltpu.VMEM((HG_HEADS, HG_DK, HG_DK), F32)],
        compiler_params=_cparams(2, V7X_VMEM_LIMIT_BYTES),
        name="hgrn2",
    )(q, k, lf, v, g, gain, cmat, *flat)
    if riding:
        casted = [o.reshape(a.shape) for o, a in zip(outs[1:], cast_f32)]
    else:
        casted = [a.astype(BF16) for a in cast_f32]
    return outs[0], casted


def _sb_kernel(q_ref, k0_ref, k1_ref, k2_ref, v0_ref, v1_ref, v2_ref, gain_ref, kall_ref, vall_ref,
               o_ref, acc_ref, out_ref, kbuf_ref, vbuf_ref, sem_ref):
    tb = SB_BLOCK
    bi = pl.program_id(0)
    qi = pl.program_id(1)
    n_pairs = SB_HEADS // 2
    pair_w = 2 * SB_DH

    acc_ref[...] = jnp.zeros_like(acc_ref)
    out_ref[...] = jnp.zeros_like(out_ref)

    t_io = lax.broadcasted_iota(I32, (SB_HEADS * tb, tb), 0) & (tb - 1)
    s_io = lax.broadcasted_iota(I32, (SB_HEADS * tb, tb), 1)
    causal = s_io < t_io
    u_row = lax.broadcasted_iota(I32, (tb, 2 * tb), 0)
    u_col = lax.broadcasted_iota(I32, (tb, 2 * tb), 1)
    um = jnp.where(jnp.logical_or(u_col >= tb, u_row > u_col), 1.0, 0.0).astype(BF16)
    lane = lax.broadcasted_iota(I32, (tb, pair_w), 1)
    lo_half = lane < SB_DH
    keeps = (lo_half, jnp.logical_not(lo_half))

    def process(blocks):
        pre = []
        for load_k, _, diag, wgt in blocks:
            zs = []
            for p in range(n_pairs):
                sl = slice(p * pair_w, (p + 1) * pair_w)
                q2 = q_ref[0, :, sl]
                k2 = load_k(sl)
                zero = jnp.zeros_like(q2)
                for half in range(2):
                    zs.append(_dot_nt(jnp.where(keeps[half], q2, zero), k2))
            z = jnp.concatenate(zs, axis=0)
            sp_full = jnp.maximum(z, 0.0) + jnp.log(1.0 + jnp.exp(-jnp.abs(z)))
            sp = jnp.where(causal, sp_full, 0.0) if diag else sp_full
            if wgt is not None:
                sp = sp * wgt
            lt = _dot(sp.astype(BF16), um)
            pre.append(((z - sp_full) - lt[:, :tb], lt[:, tb:]))
        acc = acc_ref[...]
        for (_, load_v, diag, wgt), (base, total) in zip(blocks, pre):
            a = jnp.exp(base - acc)
            if diag:
                a = jnp.where(causal, a, 0.0)
            if wgt is not None:
                a = a * wgt
            a = a.astype(BF16)
            acc = acc + total
            for p in range(n_pairs):
                sl = slice(p * pair_w, (p + 1) * pair_w)
                v2 = load_v(sl)
                zero = jnp.zeros_like(v2)
                o_pair = jnp.zeros((tb, pair_w), F32)
                for half in range(2):
                    h = 2 * p + half
                    o_pair = o_pair + _dot(a[h * tb:(h + 1) * tb], jnp.where(keeps[half], v2, zero))
                out_ref[:, sl] += o_pair
        acc_ref[...] = acc
        return jnp.min(acc)

    def blocked(ref):
        return lambda sl: ref[0, :, sl]

    def whole(ref):
        return lambda sl: ref[:, sl]

    on1 = jnp.where(qi >= 1, 1.0, 0.0)
    on2 = jnp.where(qi >= 2, 1.0, 0.0)
    m2 = process([
        (blocked(k0_ref), blocked(v0_ref), True, None),
        (blocked(k1_ref), blocked(v1_ref), False, on1),
        (blocked(k2_ref), blocked(v2_ref), False, on2),
    ])

    def cond(cr):
        j, m = cr
        return jnp.logical_and(j >= 0, m <= SB_SKIP_THRESHOLD)

    def body(cr):
        j, _ = cr
        r0 = pl.multiple_of(j * tb, tb)
        ck = pltpu.make_async_copy(kall_ref.at[bi, pl.ds(r0, tb), :], kbuf_ref, sem_ref.at[0])
        cv = pltpu.make_async_copy(vall_ref.at[bi, pl.ds(r0, tb), :], vbuf_ref, sem_ref.at[1])
        ck.start()
        cv.start()
        ck.wait()
        cv.wait()
        return j - 1, process([(whole(kbuf_ref), whole(vbuf_ref), False, None)])

    lax.while_loop(cond, body, (qi - SB_STATIC_BLOCKS, m2))

    o = out_ref[...]
    wd = SB_WIDTH
    dh_bits = SB_DH.bit_length() - 1
    bd = ((lax.broadcasted_iota(I32, (wd, wd), 0) >> dh_bits)
          == (lax.broadcasted_iota(I32, (wd, wd), 1) >> dh_bits)).astype(BF16)
    sq_h, sq_l = _split2(o * o)
    var = (_dot(sq_h, bd) + _dot(sq_l, bd)) * (1.0 / SB_DH)
    o_ref[0] = (o * lax.rsqrt(var + NORM_EPS) * gain_ref[...]).astype(o_ref.dtype)


def _stickbreak(sq, sk, sv, gain):
    b, s, w = sq.shape
    tb = SB_BLOCK
    cur = lambda bi, qi: (bi, qi, 0)
    prev1 = lambda bi, qi: (bi, jnp.maximum(qi - 1, 0), 0)
    prev2 = lambda bi, qi: (bi, jnp.maximum(qi - 2, 0), 0)
    fix = lambda bi, qi: (0, 0)
    blk = (1, tb, w)
    return pl.pallas_call(
        _sb_kernel,
        grid=(b, s // tb),
        in_specs=[
            pl.BlockSpec(blk, cur),
            pl.BlockSpec(blk, cur), pl.BlockSpec(blk, prev1), pl.BlockSpec(blk, prev2),
            pl.BlockSpec(blk, cur), pl.BlockSpec(blk, prev1), pl.BlockSpec(blk, prev2),
            pl.BlockSpec((1, w), fix),
            pl.BlockSpec(memory_space=pl.ANY),
            pl.BlockSpec(memory_space=pl.ANY),
        ],
        out_specs=pl.BlockSpec(blk, cur),
        out_shape=jax.ShapeDtypeStruct((b, s, w), BF16),
        scratch_shapes=[
            pltpu.VMEM((SB_HEADS * tb, tb), F32),
            pltpu.VMEM((tb, w), F32),
            pltpu.VMEM((tb, w), BF16),
            pltpu.VMEM((tb, w), BF16),
            pltpu.SemaphoreType.DMA((2,)),
        ],
        compiler_params=_cparams(2, V7X_VMEM_LIMIT_BYTES),
        name="stickbreak",
    )(sq, sk, sk, sk, sv, sv, sv, gain, sk, sv)


def _mix_router_kernel(ohg_ref, osb_ref, x_ref, wout_ref, gffn_ref, wrt_ref, br_ref,
                       h_ref, xs_ref, gate_ref, pos_ref, cnt_ref, *, tm):
    h = (x_ref[...]
         + _dot(ohg_ref[...], wout_ref[0:HG_WIDTH, :])
         + _dot(osb_ref[...], wout_ref[HG_WIDTH:HG_WIDTH + SB_WIDTH, :]))
    h_ref[...] = h
    var = jnp.mean(h * h, axis=-1, keepdims=True)
    u = h * lax.rsqrt(var + NORM_EPS) * gffn_ref[...]

    u_h, u_l = _split2(u)
    w_h, w_l = _split2(wrt_ref[...])
    logits = _dot_nt(w_h, u_h) + _dot_nt(w_h, u_l) + _dot_nt(w_l, u_h) + br_ref[...]

    e_io = lax.broadcasted_iota(I32, (N_EXPERTS, tm), 0).astype(F32)
    vals = logits
    member = jnp.zeros((N_EXPERTS, tm), F32)
    top_v, top_i = [], []
    for _ in range(TOP_K):
        m = jnp.max(vals, axis=0, keepdims=True)
        idx = jnp.min(jnp.where(vals == m, e_io, float(N_EXPERTS)), axis=0, keepdims=True)
        sel = e_io == idx
        top_v.append(m)
        top_i.append(idx)
        member = member + jnp.where(sel, 1.0, 0.0)
        vals = jnp.where(sel, -jnp.inf, vals)

    ex = [jnp.exp(tv - top_v[0]) for tv in top_v]
    den = ex[0] + ex[1] + ex[2] + ex[3]
    gates = [e / den for e in ex]

    n_io = lax.broadcasted_iota(I32, (tm, tm), 0)
    m_io = lax.broadcasted_iota(I32, (tm, tm), 1)
    before = jnp.where(n_io < m_io, 1.0, 0.0).astype(BF16)
    cexcl = _dot(member.astype(BF16), before)
    cnt = jnp.sum(member, axis=1, keepdims=True)
    cnt_ref[0] = jnp.broadcast_to(cnt, cnt_ref.shape[1:]).astype(I32)

    ee_r = lax.broadcasted_iota(I32, (N_EXPERTS, N_EXPERTS), 0)
    ee_c = lax.broadcasted_iota(I32, (N_EXPERTS, N_EXPERTS), 1)
    lower = jnp.where(ee_c < ee_r, 1.0, 0.0).astype(BF16)
    c_h, c_l = _split2(jnp.broadcast_to(cnt, (N_EXPERTS, LANES)))
    run_start = (_dot(lower, c_h) + _dot(lower, c_l))[:, 0:1]
    where_in_tile = cexcl + run_start
    pos = [jnp.sum(jnp.where(e_io == ti, where_in_tile, 0.0), axis=0, keepdims=True) for ti in top_i]

    n_pairs = TOP_K * tm
    j_io = lax.broadcasted_iota(I32, (n_pairs, tm), 0).astype(F32)
    hit = j_io == pos[0]
    for pk in pos[1:]:
        hit = jnp.logical_or(hit, j_io == pk)
    onehot = jnp.where(hit, 1.0, 0.0).astype(BF16)
    _store_row_tiles(xs_ref, _dot(onehot, u.astype(BF16)))

    r_io = lax.broadcasted_iota(I32, (LANES, tm), 0)
    gfull = jnp.zeros((LANES, tm), F32)
    pfull = jnp.zeros((LANES, tm), F32)
    for kk_ in range(TOP_K):
        gfull = jnp.where(r_io == kk_, jnp.broadcast_to(gates[kk_], (LANES, tm)), gfull)
        pfull = jnp.where(r_io == kk_, jnp.broadcast_to(pos[kk_], (LANES, tm)), pfull)
    gate_ref[...] = gfull.T
    pos_ref[...] = pfull.T


def _mix_router(ohg, osb, x2, w_out_bf, g_ffn, w_router_t, b_router_col, tm=512):
    n, d = x2.shape
    tm = min(tm, n)
    row = lambda i: (i, 0)
    col = lambda i: (0, i)
    fix = lambda i: (0, 0)
    return pl.pallas_call(
        functools.partial(_mix_router_kernel, tm=tm),
        grid=(n // tm,),
        in_specs=[
            pl.BlockSpec((tm, HG_WIDTH), row),
            pl.BlockSpec((tm, SB_WIDTH), row),
            pl.BlockSpec((tm, d), row),
            pl.BlockSpec(w_out_bf.shape, fix),
            pl.BlockSpec((1, d), fix),
            pl.BlockSpec(w_router_t.shape, fix),
            pl.BlockSpec(b_router_col.shape, fix),
        ],
        out_specs=[
            pl.BlockSpec((tm, d), row),
            pl.BlockSpec((TOP_K * tm * ROW_TILE, LANES), row),
            pl.BlockSpec((tm, LANES), row),
            pl.BlockSpec((tm, LANES), row),
            pl.BlockSpec((1, N_EXPERTS, LANES), lambda i: (i, 0, 0)),
        ],
        out_shape=[
            jax.ShapeDtypeStruct((n, d), F32),
            jax.ShapeDtypeStruct((TOP_K * n * ROW_TILE, LANES), F32),
            jax.ShapeDtypeStruct((n, LANES), F32),
            jax.ShapeDtypeStruct((n, LANES), F32),
            jax.ShapeDtypeStruct((n // tm, N_EXPERTS, LANES), I32),
        ],
        compiler_params=_cparams(1, V7X_VMEM_LIMIT_BYTES),
        name="mix_router",
    )(ohg, osb, x2, w_out_bf, g_ffn, w_router_t, b_router_col)


def _run_sizes(limit):
    sizes = []
    s = 1
    while s <= limit:
        sizes.append(s)
        s *= 2
    return sizes[::-1]


def _row_run(ref, row, size):
    return ref.at[pl.ds(pl.multiple_of(row * ROW_TILE, ROW_TILE), size * ROW_TILE)]


def _run_copies(mt_ref, t, tm, make_copy):
    for e in range(N_EXPERTS):
        m = mt_ref[t, e]
        for size in _run_sizes(tm):
            @pl.when((m & size) != 0)
            def _(size=size, m=m, e=e):
                make_copy(e, m & ~(2 * size - 1), size).start()


MOE_PAIR = 2


def _moe_kernel(be_ref, sb_ref, ns_ref, tlo_ref, thi_ref, nv_ref, mt_ref, ot_ref, dt_ref,
                xs_ref, *refs, tile_rows):
    n_w = 6 * MOE_PAIR
    w_refs, y_ref = refs[:n_w], refs[n_w]
    bufs, sem_ref = refs[n_w + 1:n_w + 1 + MOE_PAIR], refs[n_w + 1 + MOE_PAIR]
    i = pl.program_id(0)
    n_used_steps = ns_ref[0]
    sizes = _run_sizes(MOE_BLOCK)

    def gather(b, half):
        e = be_ref[b]
        first_slot = b * MOE_BLOCK

        def per_tile(t, carry):
            run0 = dt_ref[t, e]
            lo = jnp.maximum(run0, first_slot)
            hi = jnp.minimum(run0 + mt_ref[t, e], first_slot + MOE_BLOCK)
            length = jnp.maximum(hi - lo, 0)
            src = t * tile_rows + ot_ref[t, e] + (lo - run0)
            dst = lo - first_slot
            for size in sizes:
                @pl.when((length & size) != 0)
                def _(size=size):
                    done = length & ~(2 * size - 1)
                    pltpu.make_async_copy(
                        _row_run(xs_ref, src + done, size), _row_run(bufs[half], dst + done, size),
                        sem_ref.at[half]).start()
            return carry

        lax.fori_loop(tlo_ref[b], thi_ref[b] + 1, per_tile, 0)

    def wait(b, half):
        valid = nv_ref[b]
        for size in sizes:
            @pl.when((valid & size) != 0)
            def _(size=size):
                pltpu.make_async_copy(
                    _row_run(xs_ref, 0, size), _row_run(bufs[half], 0, size), sem_ref.at[half]).wait()

    @pl.when(i == 0)
    def _():
        for half in range(MOE_PAIR):
            bufs[half][...] = jnp.zeros_like(bufs[half])
            gather(half, half)

    @pl.when(i < n_used_steps)
    def _():
        for half in range(MOE_PAIR):
            wg_ref, bg_ref, wu_ref, bu_ref, wd_ref, bd_ref = w_refs[6 * half:6 * half + 6]
            wait(MOE_PAIR * i + half, half)
            x = _load_row_tiles(bufs[half], MOE_BLOCK).astype(BF16)

            @pl.when(i + 1 < n_used_steps)
            def _(half=half):
                gather(MOE_PAIR * (i + 1) + half, half)

            hg = _dot(x, wg_ref[0]) + bg_ref[0]
            hu = _dot(x, wu_ref[0]) + bu_ref[0]
            hg = jnp.minimum(hg, SWIGLU_LIMIT)
            hu = jnp.clip(hu, -SWIGLU_LIMIT, SWIGLU_LIMIT)
            glu = hg * _sigmoid(SWIGLU_ALPHA * hg)
            act = ((hu + 1.0) * glu).astype(BF16)
            _store_row_tiles(y_ref, _dot(act, wd_ref[0]) + bd_ref[0], base=half * MOE_BLOCK)

    @pl.when(i >= n_used_steps)
    def _():
        y_ref[...] = jnp.zeros_like(y_ref)


def _moe(sched, xs, n_slots, tile_rows, wg, bg, wu, bu, wd, bd):
    d, f = wg.shape[1], wg.shape[2]
    step_rows = MOE_PAIR * MOE_BLOCK
    n_steps = n_slots // step_rows
    w_specs, w_args = [], []
    for half in range(MOE_PAIR):
        wmap = lambda i, be, sb, *_, half=half: (be[MOE_PAIR * sb[i] + half], 0, 0)
        w_specs += [pl.BlockSpec((1, d, f), wmap), pl.BlockSpec((1, 1, f), wmap),
                    pl.BlockSpec((1, d, f), wmap), pl.BlockSpec((1, 1, f), wmap),
                    pl.BlockSpec((1, f, d), wmap), pl.BlockSpec((1, 1, d), wmap)]
        w_args += [wg, bg, wu, bu, wd, bd]
    grid_spec = pltpu.PrefetchScalarGridSpec(
        num_scalar_prefetch=len(sched),
        grid=(n_steps,),
        in_specs=[pl.BlockSpec(memory_space=pl.ANY)] + w_specs,
        out_specs=pl.BlockSpec((step_rows * ROW_TILE, LANES), lambda i, *_: (i, 0)),
        scratch_shapes=[pltpu.VMEM((MOE_BLOCK * ROW_TILE, LANES), F32)] * MOE_PAIR
        + [pltpu.SemaphoreType.DMA((MOE_PAIR,))],
    )
    return pl.pallas_call(
        functools.partial(_moe_kernel, tile_rows=tile_rows),
        grid_spec=grid_spec,
        out_shape=jax.ShapeDtypeStruct((n_slots * ROW_TILE, LANES), F32),
        compiler_params=_cparams(1, V7X_VMEM_LIMIT_BYTES),
        name="moe",
    )(*sched, xs, *w_args)


def _combine_kernel(mt_ref, ot_ref, dt_ref, y_ref, pos_ref, h_ref, gate_ref, gain_ref, o_ref,
                    ybuf_ref, sem_ref, *, tm):
    t = pl.program_id(0)
    n_tiles = pl.num_programs(0)
    n_pairs = TOP_K * tm
    slot = t % 2

    def fetch(tt, s):
        _run_copies(mt_ref, tt, tm, lambda e, done, size: pltpu.make_async_copy(
            _row_run(y_ref, dt_ref[tt, e] + done, size),
            _row_run(ybuf_ref, s * n_pairs + ot_ref[tt, e] + done, size), sem_ref.at[s]))

    @pl.when(t == 0)
    def _():
        fetch(0, 0)

    @pl.when(t + 1 < n_tiles)
    def _():
        fetch(t + 1, 1 - slot)

    pltpu.make_async_copy(
        _row_run(y_ref, 0, n_pairs), _row_run(ybuf_ref, 0, n_pairs), sem_ref.at[slot]).wait()

    gate = gate_ref[...]
    pos = pos_ref[...]
    j_io = lax.broadcasted_iota(I32, (tm, n_pairs), 1).astype(F32)
    wmat = jnp.zeros((tm, n_pairs), F32)
    for k in range(TOP_K):
        wmat = wmat + jnp.where(j_io == pos[:, k:k + 1], gate[:, k:k + 1], 0.0)
    ys = _load_row_tiles(ybuf_ref, n_pairs, base=slot * n_pairs).astype(BF16)
    acc = h_ref[...] + _dot(wmat.astype(BF16), ys)
    var = jnp.mean(acc * acc, axis=-1, keepdims=True)
    o_ref[...] = (acc * lax.rsqrt(var + NORM_EPS) * gain_ref[...]).astype(o_ref.dtype)


def _combine(tabs, y_disp, pos_c, h, gates, gain, tm):
    n, d = h.shape
    row = lambda t, *_: (t, 0)
    fix = lambda t, *_: (0, 0)
    grid_spec = pltpu.PrefetchScalarGridSpec(
        num_scalar_prefetch=3,
        grid=(n // tm,),
        in_specs=[
            pl.BlockSpec(memory_space=pl.ANY),
            pl.BlockSpec((tm, LANES), row),
            pl.BlockSpec((tm, d), row),
            pl.BlockSpec((tm, LANES), row),
            pl.BlockSpec((1, d), fix),
        ],
        out_specs=pl.BlockSpec((tm, d), row),
        scratch_shapes=[
            pltpu.VMEM((2 * TOP_K * tm * ROW_TILE, LANES), F32),
            pltpu.SemaphoreType.DMA((2,)),
        ],
    )
    return pl.pallas_call(
        functools.partial(_combine_kernel, tm=tm),
        grid_spec=grid_spec,
        out_shape=jax.ShapeDtypeStruct((n, d), F32),
        compiler_params=_cparams(1, V7X_VMEM_LIMIT_BYTES),
        name="combine",
    )(*tabs[:3], y_disp, pos_c, h, gates, gain)


def _routing_tables(tile_cnt, n_pairs):
    counts = jnp.sum(tile_cnt, axis=0)
    padded = ((counts + MOE_BLOCK - 1) // MOE_BLOCK) * MOE_BLOCK
    cum_pad = jnp.cumsum(padded)
    start_pad = (cum_pad - padded).astype(I32)
    n_slots = ((n_pairs + MOE_BLOCK - 1) // MOE_BLOCK) * MOE_BLOCK + N_EXPERTS * MOE_BLOCK
    n_slots = ((n_slots + MOE_PAIR * MOE_BLOCK - 1) // (MOE_PAIR * MOE_BLOCK)) * (MOE_PAIR * MOE_BLOCK)
    nb = n_slots // MOE_BLOCK
    block_start = jnp.arange(nb, dtype=I32) * MOE_BLOCK
    block_e = jnp.minimum(jnp.sum(cum_pad[None, :] <= block_start[:, None], axis=1), N_EXPERTS - 1)
    block_e = block_e.astype(I32)
    n_used = (cum_pad[-1] // MOE_BLOCK).astype(I32)
    n_steps_used = (n_used + MOE_PAIR - 1) // MOE_PAIR
    step_blk = jnp.minimum(jnp.arange(nb // MOE_PAIR, dtype=I32), n_steps_used - 1)
    before_tile = jnp.cumsum(tile_cnt, axis=0) - tile_cnt
    run_dst = (start_pad[None, :] + before_tile).astype(I32)
    run_src = (jnp.cumsum(tile_cnt, axis=1) - tile_cnt).astype(I32)
    dst_b = run_dst[:, block_e]
    end_b = dst_b + tile_cnt[:, block_e]
    tile_lo = jnp.sum(end_b <= block_start[None, :], axis=0).astype(I32)
    tile_hi = (jnp.sum(dst_b < block_start[None, :] + MOE_BLOCK, axis=0) - 1).astype(I32)
    used = jnp.arange(nb, dtype=I32) < n_used
    n_valid = jnp.clip((start_pad + counts)[block_e] - block_start, 0, MOE_BLOCK)
    n_valid = jnp.where(used, n_valid, 0).astype(I32)
    tile_hi = jnp.where(used, tile_hi, tile_lo - 1)
    run_tabs = (tile_cnt.astype(I32), run_src, run_dst)
    sched = (block_e, step_blk, n_steps_used.reshape(1).astype(I32), tile_lo, tile_hi, n_valid) + run_tabs
    return run_tabs, sched, n_slots


def kernel(x, w_in, w_out, hg_lb_logits, hg_norm_gain, sb_norm_gain, norm_mix_gain, norm_ffn_gain,
           w_router, b_router, w_gate, b_gate, w_up, b_up, w_down, b_down, norm_final_gain):
    b, s, d = x.shape
    n = b * s
    f = w_gate.shape[-1]
    assert w_in.shape[0] == 1 and hg_lb_logits.shape[0] == 2, "single-layer trunk only"
    x2 = x.reshape(n, d).astype(F32)
    r3 = lambda a: a.reshape(b, s, a.shape[-1])

    q, k, lf, v, g, sq, sk, sv = _in_proj(
        x2, norm_mix_gain[0].reshape(1, d), w_in[0].astype(BF16), hg_lb_logits.astype(F32))
    o_hg, (wg_bf, wu_bf, wd_bf) = _hgrn2(
        r3(q), r3(k), r3(lf), r3(v), r3(g), hg_norm_gain[0].reshape(1, HG_WIDTH),
        cast_f32=(w_gate[0], w_up[0], w_down[0]))
    o_sb = _stickbreak(r3(sq), r3(sk), r3(sv), sb_norm_gain[0].reshape(1, SB_WIDTH))
    h_mid, xs, gates, pos_c, cnt = _mix_router(
        o_hg.reshape(n, HG_WIDTH), o_sb.reshape(n, SB_WIDTH), x2, w_out[0].astype(BF16),
        norm_ffn_gain[0].reshape(1, d), w_router[0].T.astype(F32),
        b_router[0].reshape(N_EXPERTS, 1).astype(F32))
    tile_cnt = cnt[:, :, 0]
    run_tabs, sched, n_slots = _routing_tables(tile_cnt, n * TOP_K)
    tm = n // tile_cnt.shape[0]
    y_disp = _moe(sched, xs, n_slots, TOP_K * tm,
                  wg_bf, b_gate[0].reshape(N_EXPERTS, 1, f),
                  wu_bf, b_up[0].reshape(N_EXPERTS, 1, f),
                  wd_bf, b_down[0].reshape(N_EXPERTS, 1, d))
    out = _combine(run_tabs, y_disp, pos_c, h_mid, gates, norm_final_gain.reshape(1, d), tm)
    return out.reshape(b, s, d).astype(x.dtype)
```

```python
import functools

import jax
import jax.numpy as jnp
from jax import lax
from jax.experimental import pallas as pl
from jax.experimental.pallas import tpu as pltpu

F32 = jnp.float32
BF16 = jnp.bfloat16
I32 = jnp.int32

NORM_EPS = 1e-5
HG_HEADS = 4
HG_DK = 128
HG_WIDTH = HG_HEADS * HG_DK
HG_CHUNK = 64
HG_LEVELS = (32, 16, 8)
HG_DIAG = 8
HG_FACTORED_MAX_EXPONENT = 60.0
SB_HEADS = 8
SB_DH = 64
SB_WIDTH = SB_HEADS * SB_DH
SB_BLOCK = 128
SB_STATIC_BLOCKS = 3
SB_SKIP_THRESHOLD = 104.0
N_EXPERTS = 32
TOP_K = 4
MOE_BLOCK = 256
SWIGLU_LIMIT = 7.0
SWIGLU_ALPHA = 1.702
NEG_BIG = -1e30

V7X_VMEM_LIMIT_BYTES = 56 * 1024 * 1024
ROW_TILE = 8
LANES = 128
D_MODEL = ROW_TILE * LANES


def _cparams(n_axes, vmem_bytes=None):
    return pltpu.CompilerParams(
        dimension_semantics=("arbitrary",) * n_axes,
        vmem_limit_bytes=vmem_bytes,
    )


def _sigmoid(x):
    return 1.0 / (1.0 + jnp.exp(-x))


def _split2(x):
    hi = x.astype(BF16)
    lo = (x - hi.astype(F32)).astype(BF16)
    return hi, lo


def _split3(x):
    hi = x.astype(BF16)
    r = x - hi.astype(F32)
    mid = r.astype(BF16)
    lo = (r - mid.astype(F32)).astype(BF16)
    return hi, mid, lo


def _dot(a, b):
    return jnp.dot(a, b, preferred_element_type=F32)


def _store_row_tiles(ref, x, base=0):
    rows = x.shape[0]
    for j in range(ROW_TILE):
        ref[pl.ds(base * ROW_TILE + j, rows, stride=ROW_TILE), :] = x[:, j * LANES:(j + 1) * LANES]


def _load_row_tiles(ref, rows, base=0, chunk=None):
    if chunk is not None:
        return ref[pl.ds(base * ROW_TILE + chunk, rows, stride=ROW_TILE), :]
    return jnp.concatenate(
        [ref[pl.ds(base * ROW_TILE + j, rows, stride=ROW_TILE), :] for j in range(ROW_TILE)], axis=-1)


def _dot_nt(a, b):
    return lax.dot_general(a, b, (((1,), (1,)), ((), ())), preferred_element_type=F32)


def _dot_tn(a, b):
    return lax.dot_general(a, b, (((0,), (0,)), ((), ())), preferred_element_type=F32)


def _in_proj_kernel(x_ref, gain_ref, w_ref, lbl_ref,
                    q_ref, k_ref, lf_ref, v_ref, g_ref, sq_ref, sk_ref, sv_ref):
    x = x_ref[...]
    var = jnp.mean(x * x, axis=-1, keepdims=True)
    u = (x * lax.rsqrt(var + NORM_EPS) * gain_ref[...]).astype(BF16)

    lbl = lbl_ref[...]
    mx = jnp.max(lbl, axis=0, keepdims=True)
    ex = jnp.exp(lbl - mx)
    lb = ex[0:1, :] / jnp.sum(ex, axis=0, keepdims=True)

    def seg(i):
        return _dot(u, w_ref[:, i * HG_WIDTH:(i + 1) * HG_WIDTH])

    hq = seg(0)
    q_ref[...] = hq * _sigmoid(hq)
    f_sig = _sigmoid(seg(1))
    lf_ref[...] = jnp.log(lb + (1.0 - lb) * f_sig)
    k_ref[...] = (1.0 - lb) * (1.0 - f_sig)
    v_ref[...] = seg(2)
    hg = seg(3)
    g_ref[...] = hg * _sigmoid(hg)
    sq_ref[...] = (seg(4) * (SB_DH ** -0.5)).astype(BF16)
    sk_ref[...] = seg(5).astype(BF16)
    sv_ref[...] = seg(6).astype(BF16)


def _in_proj(x2, gain, w_in_bf, lb_logits, tm=512):
    n, d = x2.shape
    cols = w_in_bf.shape[1]
    row = lambda i: (i, 0)
    fix = lambda i: (0, 0)
    o_f32 = jax.ShapeDtypeStruct((n, HG_WIDTH), F32)
    o_bf = jax.ShapeDtypeStruct((n, SB_WIDTH), BF16)
    return pl.pallas_call(
        _in_proj_kernel,
        grid=(n // tm,),
        in_specs=[
            pl.BlockSpec((tm, d), row),
            pl.BlockSpec((1, d), fix),
            pl.BlockSpec((d, cols), fix),
            pl.BlockSpec(lb_logits.shape, fix),
        ],
        out_specs=[pl.BlockSpec((tm, HG_WIDTH), row)] * 8,
        out_shape=[o_f32] * 5 + [o_bf] * 3,
        compiler_params=_cparams(1, V7X_VMEM_LIMIT_BYTES),
        name="in_proj",
    )(x2, gain, w_in_bf, lb_logits)


def _hgrn2_consts():
    c = HG_CHUNK
    t = jnp.arange(c)[:, None]
    s = jnp.arange(c)[None, :]
    mats = [(s <= t)]
    for lv in HG_LEVELS:
        ref = (t // (2 * lv)) * (2 * lv) + lv - 1
        mats.append(s <= ref)
    mats.append(s <= (t // HG_DIAG) * HG_DIAG)
    return jnp.concatenate(mats, axis=0).astype(BF16)


def _hgrn2_kernel(q_ref, k_ref, lf_ref, v_ref, g_ref, gain_ref, cmat_ref, *refs, ts, n_cast):
    c = HG_CHUNK
    w = HG_WIDTH
    cast_in, o_ref = refs[:n_cast], refs[n_cast]
    cast_out, st_ref = refs[n_cast + 1:2 * n_cast + 1], refs[2 * n_cast + 1]
    for src, dst in zip(cast_in, cast_out):
        dst[...] = src[...].astype(dst.dtype)

    @pl.when(pl.program_id(1) == 0)
    def _():
        st_ref[...] = jnp.zeros_like(st_ref)

    row_w = lax.broadcasted_iota(I32, (c, w), 0)
    row_c = lax.broadcasted_iota(I32, (c, c), 0)
    col_c = lax.broadcasted_iota(I32, (c, c), 1)
    row_d = lax.broadcasted_iota(I32, (HG_DIAG, w), 0)
    dk_bits = HG_DK.bit_length() - 1
    bd = ((lax.broadcasted_iota(I32, (w, w), 0) >> dk_bits)
          == (lax.broadcasted_iota(I32, (w, w), 1) >> dk_bits)).astype(BF16)
    cmat = cmat_ref[...]
    gain = gain_ref[...]

    diag_mask = jnp.logical_and((row_c >> 3) == (col_c >> 3), col_c <= row_c)

    def chunk(ci, carry, factored_diag):
        r0 = pl.multiple_of(ci * c, c)
        q = q_ref[0, pl.ds(r0, c), :]
        kk = k_ref[0, pl.ds(r0, c), :]
        lf = lf_ref[0, pl.ds(r0, c), :]
        v = v_ref[0, pl.ds(r0, c), :]
        g = g_ref[0, pl.ds(r0, c), :]

        lf_h, lf_l = _split2(lf)
        gg = _dot(cmat, lf_h) + _dot(cmat, lf_l)
        G = gg[0:c]

        scores = [jnp.zeros((c, c), F32) for _ in range(HG_HEADS)]
        for li, lv in enumerate(HG_LEVELS):
            gref = gg[(li + 1) * c:(li + 2) * c]
            is_q = (row_w & (2 * lv - 1)) >= lv
            e = jnp.exp(jnp.where(is_q, G - gref, gref - G))
            ql = jnp.where(is_q, q * e, 0.0).astype(BF16)
            kl = jnp.where(is_q, 0.0, kk * e).astype(BF16)
            grp_bits = (2 * lv).bit_length() - 1
            same = (row_c >> grp_bits) == (col_c >> grp_bits)
            for h in range(HG_HEADS):
                sl = slice(h * HG_DK, (h + 1) * HG_DK)
                scores[h] = scores[h] + jnp.where(same, _dot_nt(ql[:, sl], kl[:, sl]), 0.0)

        if factored_diag:
            gref = gg[(len(HG_LEVELS) + 1) * c:(len(HG_LEVELS) + 2) * c]
            qd = (q * jnp.exp(G - gref)).astype(BF16)
            kd = (kk * jnp.exp(gref - G)).astype(BF16)
            for h in range(HG_HEADS):
                sl = slice(h * HG_DK, (h + 1) * HG_DK)
                scores[h] = scores[h] + jnp.where(diag_mask, _dot_nt(qd[:, sl], kd[:, sl]), 0.0)
            o = jnp.zeros((c, w), F32)
        else:
            tiles = []
            for b in range(c // HG_DIAG):
                rs = slice(b * HG_DIAG, (b + 1) * HG_DIAG)
                gb, qb, kb = G[rs], q[rs], kk[rs]
                for s in range(HG_DIAG):
                    gs = jnp.broadcast_to(gb[s:s + 1, :], (HG_DIAG, w))
                    ks = jnp.broadcast_to(kb[s:s + 1, :], (HG_DIAG, w))
                    e = jnp.exp(jnp.where(row_d >= s, gb - gs, NEG_BIG))
                    tiles.append(qb * e * ks)
            p_all = jnp.concatenate(tiles, axis=0)
            r_all = _dot(p_all.astype(BF16), bd)
            o_blocks = []
            for b in range(c // HG_DIAG):
                vb = v[b * HG_DIAG:(b + 1) * HG_DIAG]
                ob = jnp.zeros((HG_DIAG, w), F32)
                for s in range(HG_DIAG):
                    i0 = (b * HG_DIAG + s) * HG_DIAG
                    vs = jnp.broadcast_to(vb[s:s + 1, :], (HG_DIAG, w))
                    ob = ob + r_all[i0:i0 + HG_DIAG] * vs
                o_blocks.append(ob)
            o = jnp.concatenate(o_blocks, axis=0)

        qg = (q * jnp.exp(G)).astype(BF16)
        g_last = jnp.broadcast_to(G[c - 1:c, :], (c, w))
        kh = (kk * jnp.exp(g_last - G)).astype(BF16)
        dec = jnp.exp(G[c - 1:c, :])
        v_bf = v.astype(BF16)
        outs = []
        for h in range(HG_HEADS):
            sl = slice(h * HG_DK, (h + 1) * HG_DK)
            st = st_ref[h]
            oh = (o[:, sl]
                  + _dot(scores[h].astype(BF16), v_bf[:, sl])
                  + _dot_nt(qg[:, sl], st.astype(BF16)))
            st_ref[h] = st * dec[:, sl] + _dot_tn(v_bf[:, sl], kh[:, sl])
            var = jnp.mean(oh * oh, axis=-1, keepdims=True)
            outs.append(oh * lax.rsqrt(var + NORM_EPS))
        on = jnp.concatenate(outs, axis=-1) * gain * g
        o_ref[0, pl.ds(r0, c), :] = on.astype(o_ref.dtype)
        return carry

    worst = jnp.max(-lf_ref[0]) * (HG_DIAG - 1)

    def run(factored_diag):
        lax.fori_loop(0, ts // c, functools.partial(chunk, factored_diag=factored_diag), 0, unroll=True)

    lax.cond(worst < HG_FACTORED_MAX_EXPONENT, lambda: run(True), lambda: run(False))


CAST_BLOCK_BYTES_MAX = 2 * 1024 * 1024


def _hgrn2(q, k, lf, v, g, gain, cast_f32=(), ts=256):
    b, s, w = q.shape
    n_s = s // ts
    blk = lambda bi, si: (bi, si, 0)
    fix = lambda bi, si: (0, 0)
    cmat = _hgrn2_consts()
    n_steps = b * n_s
    flat = [a.reshape(-1, a.shape[-1]) for a in cast_f32]
    rows = [a.shape[0] // n_steps for a in flat]
    riding = all(a.shape[0] % n_steps == 0 and r % 16 == 0 and r * a.shape[1] * 4 <= CAST_BLOCK_BYTES_MAX
                 for a, r in zip(flat, rows))
    if not riding:
        flat, rows = [], []
    cast_specs = [pl.BlockSpec((r, a.shape[1]), lambda bi, si: (bi * n_s + si, 0)) for a, r in zip(flat, rows)]
    outs = pl.pallas_call(
        functools.partial(_hgrn2_kernel, ts=ts, n_cast=len(flat)),
        grid=(b, n_s),
        in_specs=[pl.BlockSpec((1, ts, w), blk)] * 5 + [
            pl.BlockSpec((1, w), fix),
            pl.BlockSpec(cmat.shape, fix),
        ] + cast_specs,
        out_specs=[pl.BlockSpec((1, ts, w), blk)] + cast_specs,
        out_shape=[jax.ShapeDtypeStruct((b, s, w), BF16)]
        + [jax.ShapeDtypeStruct(a.shape, BF16) for a in flat],
        scratch_shapes=[pltpu.VMEM((HG_HEADS, HG_DK, HG_DK), F32)],
        compiler_params=_cparams(2, V7X_VMEM_LIMIT_BYTES),
        name="hgrn2",
    )(q, k, lf, v, g, gain, cmat, *flat)
    if riding:
        casted = [o.reshape(a.shape) for o, a in zip(outs[1:], cast_f32)]
    else:
        casted = [a.astype(BF16) for a in cast_f32]
    return outs[0], casted


def _sb_kernel(q_ref, k0_ref, k1_ref, k2_ref, v0_ref, v1_ref, v2_ref, gain_ref, kall_ref, vall_ref,
               o_ref, acc_ref, out_ref, kbuf_ref, vbuf_ref, sem_ref):
    tb = SB_BLOCK
    bi = pl.program_id(0)
    qi = pl.program_id(1)
    n_pairs = SB_HEADS // 2
    pair_w = 2 * SB_DH

    acc_ref[...] = jnp.zeros_like(acc_ref)
    out_ref[...] = jnp.zeros_like(out_ref)

    t_io = lax.broadcasted_iota(I32, (SB_HEADS * tb, tb), 0) & (tb - 1)
    s_io = lax.broadcasted_iota(I32, (SB_HEADS * tb, tb), 1)
    causal = s_io < t_io
    u_row = lax.broadcasted_iota(I32, (tb, 2 * tb), 0)
    u_col = lax.broadcasted_iota(I32, (tb, 2 * tb), 1)
    um = jnp.where(jnp.logical_or(u_col >= tb, u_row > u_col), 1.0, 0.0).astype(BF16)
    lane = lax.broadcasted_iota(I32, (tb, pair_w), 1)
    lo_half = lane < SB_DH
    keeps = (lo_half, jnp.logical_not(lo_half))

    def process(blocks):
        pre = []
        for load_k, _, diag, wgt in blocks:
            zs = []
            for p in range(n_pairs):
                sl = slice(p * pair_w, (p + 1) * pair_w)
                q2 = q_ref[0, :, sl]
                k2 = load_k(sl)
                zero = jnp.zeros_like(q2)
                for half in range(2):
                    zs.append(_dot_nt(jnp.where(keeps[half], q2, zero), k2))
            z = jnp.concatenate(zs, axis=0)
            sp_full = jnp.maximum(z, 0.0) + jnp.log(1.0 + jnp.exp(-jnp.abs(z)))
            sp = jnp.where(causal, sp_full, 0.0) if diag else sp_full
            if wgt is not None:
                sp = sp * wgt
            lt = _dot(sp.astype(BF16), um)
            pre.append(((z - sp_full) - lt[:, :tb], lt[:, tb:]))
        acc = acc_ref[...]
        for (_, load_v, diag, wgt), (base, total) in zip(blocks, pre):
            a = jnp.exp(base - acc)
            if diag:
                a = jnp.where(causal, a, 0.0)
            if wgt is not None:
                a = a * wgt
            a = a.astype(BF16)
            acc = acc + total
            for p in range(n_pairs):
                sl = slice(p * pair_w, (p + 1) * pair_w)
                v2 = load_v(sl)
                zero = jnp.zeros_like(v2)
                o_pair = jnp.zeros((tb, pair_w), F32)
                for half in range(2):
                    h = 2 * p + half
                    o_pair = o_pair + _dot(a[h * tb:(h + 1) * tb], jnp.where(keeps[half], v2, zero))
                out_ref[:, sl] += o_pair
        acc_ref[...] = acc
        return jnp.min(acc)

    def blocked(ref):
        return lambda sl: ref[0, :, sl]

    def whole(ref):
        return lambda sl: ref[:, sl]

    on1 = jnp.where(qi >= 1, 1.0, 0.0)
    on2 = jnp.where(qi >= 2, 1.0, 0.0)
    m2 = process([
        (blocked(k0_ref), blocked(v0_ref), True, None),
        (blocked(k1_ref), blocked(v1_ref), False, on1),
        (blocked(k2_ref), blocked(v2_ref), False, on2),
    ])

    def cond(cr):
        j, m = cr
        return jnp.logical_and(j >= 0, m <= SB_SKIP_THRESHOLD)

    def body(cr):
        j, _ = cr
        r0 = pl.multiple_of(j * tb, tb)
        ck = pltpu.make_async_copy(kall_ref.at[bi, pl.ds(r0, tb), :], kbuf_ref, sem_ref.at[0])
        cv = pltpu.make_async_copy(vall_ref.at[bi, pl.ds(r0, tb), :], vbuf_ref, sem_ref.at[1])
        ck.start()
        cv.start()
        ck.wait()
        cv.wait()
        return j - 1, process([(whole(kbuf_ref), whole(vbuf_ref), False, None)])

    lax.while_loop(cond, body, (qi - SB_STATIC_BLOCKS, m2))

    o = out_ref[...]
    wd = SB_WIDTH
    dh_bits = SB_DH.bit_length() - 1
    bd = ((lax.broadcasted_iota(I32, (wd, wd), 0) >> dh_bits)
          == (lax.broadcasted_iota(I32, (wd, wd), 1) >> dh_bits)).astype(BF16)
    sq_h, sq_l = _split2(o * o)
    var = (_dot(sq_h, bd) + _dot(sq_l, bd)) * (1.0 / SB_DH)
    o_ref[0] = (o * lax.rsqrt(var + NORM_EPS) * gain_ref[...]).astype(o_ref.dtype)


def _stickbreak(sq, sk, sv, gain):
    b, s, w = sq.shape
    tb = SB_BLOCK
    cur = lambda bi, qi: (bi, qi, 0)
    prev1 = lambda bi, qi: (bi, jnp.maximum(qi - 1, 0), 0)
    prev2 = lambda bi, qi: (bi, jnp.maximum(qi - 2, 0), 0)
    fix = lambda bi, qi: (0, 0)
    blk = (1, tb, w)
    return pl.pallas_call(
        _sb_kernel,
        grid=(b, s // tb),
        in_specs=[
            pl.BlockSpec(blk, cur),
            pl.BlockSpec(blk, cur), pl.BlockSpec(blk, prev1), pl.BlockSpec(blk, prev2),
            pl.BlockSpec(blk, cur), pl.BlockSpec(blk, prev1), pl.BlockSpec(blk, prev2),
            pl.BlockSpec((1, w), fix),
            pl.BlockSpec(memory_space=pl.ANY),
            pl.BlockSpec(memory_space=pl.ANY),
        ],
        out_specs=pl.BlockSpec(blk, cur),
        out_shape=jax.ShapeDtypeStruct((b, s, w), BF16),
        scratch_shapes=[
            pltpu.VMEM((SB_HEADS * tb, tb), F32),
            pltpu.VMEM((tb, w), F32),
            pltpu.VMEM((tb, w), BF16),
            pltpu.VMEM((tb, w), BF16),
            pltpu.SemaphoreType.DMA((2,)),
        ],
        compiler_params=_cparams(2, V7X_VMEM_LIMIT_BYTES),
        name="stickbreak",
    )(sq, sk, sk, sk, sv, sv, sv, gain, sk, sv)


def _mix_router_kernel(ohg_ref, osb_ref, x_ref, wout_ref, gffn_ref, wrt_ref, br_ref,
                       h_ref, xs_ref, gate_ref, pos_ref, cnt_ref, *, tm):
    h = (x_ref[...]
         + _dot(ohg_ref[...], wout_ref[0:HG_WIDTH, :])
         + _dot(osb_ref[...], wout_ref[HG_WIDTH:HG_WIDTH + SB_WIDTH, :]))
    h_ref[...] = h
    var = jnp.mean(h * h, axis=-1, keepdims=True)
    u = h * lax.rsqrt(var + NORM_EPS) * gffn_ref[...]

    u_h, u_l = _split2(u)
    w_h, w_l = _split2(wrt_ref[...])
    logits = _dot_nt(w_h, u_h) + _dot_nt(w_h, u_l) + _dot_nt(w_l, u_h) + br_ref[...]

    e_io = lax.broadcasted_iota(I32, (N_EXPERTS, tm), 0).astype(F32)
    vals = logits
    member = jnp.zeros((N_EXPERTS, tm), F32)
    top_v, top_i = [], []
    for _ in range(TOP_K):
        m = jnp.max(vals, axis=0, keepdims=True)
        idx = jnp.min(jnp.where(vals == m, e_io, float(N_EXPERTS)), axis=0, keepdims=True)
        sel = e_io == idx
        top_v.append(m)
        top_i.append(idx)
        member = member + jnp.where(sel, 1.0, 0.0)
        vals = jnp.where(sel, -jnp.inf, vals)

    ex = [jnp.exp(tv - top_v[0]) for tv in top_v]
    den = ex[0] + ex[1] + ex[2] + ex[3]
    gates = [e / den for e in ex]

    n_io = lax.broadcasted_iota(I32, (tm, tm), 0)
    m_io = lax.broadcasted_iota(I32, (tm, tm), 1)
    before = jnp.where(n_io < m_io, 1.0, 0.0).astype(BF16)
    cexcl = _dot(member.astype(BF16), before)
    cnt = jnp.sum(member, axis=1, keepdims=True)
    cnt_ref[0] = jnp.broadcast_to(cnt, cnt_ref.shape[1:]).astype(I32)

    ee_r = lax.broadcasted_iota(I32, (N_EXPERTS, N_EXPERTS), 0)
    ee_c = lax.broadcasted_iota(I32, (N_EXPERTS, N_EXPERTS), 1)
    lower = jnp.where(ee_c < ee_r, 1.0, 0.0).astype(BF16)
    c_h, c_l = _split2(jnp.broadcast_to(cnt, (N_EXPERTS, LANES)))
    run_start = (_dot(lower, c_h) + _dot(lower, c_l))[:, 0:1]
    where_in_tile = cexcl + run_start
    pos = [jnp.sum(jnp.where(e_io == ti, where_in_tile, 0.0), axis=0, keepdims=True) for ti in top_i]

    n_pairs = TOP_K * tm
    j_io = lax.broadcasted_iota(I32, (n_pairs, tm), 0).astype(F32)
    hit = j_io == pos[0]
    for pk in pos[1:]:
        hit = jnp.logical_or(hit, j_io == pk)
    onehot = jnp.where(hit, 1.0, 0.0).astype(BF16)
    _store_row_tiles(xs_ref, _dot(onehot, u.astype(BF16)))

    r_io = lax.broadcasted_iota(I32, (LANES, tm), 0)
    gfull = jnp.zeros((LANES, tm), F32)
    pfull = jnp.zeros((LANES, tm), F32)
    for kk_ in range(TOP_K):
        gfull = jnp.where(r_io == kk_, jnp.broadcast_to(gates[kk_], (LANES, tm)), gfull)
        pfull = jnp.where(r_io == kk_, jnp.broadcast_to(pos[kk_], (LANES, tm)), pfull)
    gate_ref[...] = gfull.T
    pos_ref[...] = pfull.T


def _mix_router(ohg, osb, x2, w_out_bf, g_ffn, w_router_t, b_router_col, tm=512):
    n, d = x2.shape
    tm = min(tm, n)
    row = lambda i: (i, 0)
    col = lambda i: (0, i)
    fix = lambda i: (0, 0)
    return pl.pallas_call(
        functools.partial(_mix_router_kernel, tm=tm),
        grid=(n // tm,),
        in_specs=[
            pl.BlockSpec((tm, HG_WIDTH), row),
            pl.BlockSpec((tm, SB_WIDTH), row),
            pl.BlockSpec((tm, d), row),
            pl.BlockSpec(w_out_bf.shape, fix),
            pl.BlockSpec((1, d), fix),
            pl.BlockSpec(w_router_t.shape, fix),
            pl.BlockSpec(b_router_col.shape, fix),
        ],
        out_specs=[
            pl.BlockSpec((tm, d), row),
            pl.BlockSpec((TOP_K * tm * ROW_TILE, LANES), row),
            pl.BlockSpec((tm, LANES), row),
            pl.BlockSpec((tm, LANES), row),
            pl.BlockSpec((1, N_EXPERTS, LANES), lambda i: (i, 0, 0)),
        ],
        out_shape=[
            jax.ShapeDtypeStruct((n, d), F32),
            jax.ShapeDtypeStruct((TOP_K * n * ROW_TILE, LANES), F32),
            jax.ShapeDtypeStruct((n, LANES), F32),
            jax.ShapeDtypeStruct((n, LANES), F32),
            jax.ShapeDtypeStruct((n // tm, N_EXPERTS, LANES), I32),
        ],
        compiler_params=_cparams(1, V7X_VMEM_LIMIT_BYTES),
        name="mix_router",
    )(ohg, osb, x2, w_out_bf, g_ffn, w_router_t, b_router_col)


def _run_sizes(limit):
    sizes = []
    s = 1
    while s <= limit:
        sizes.append(s)
        s *= 2
    return sizes[::-1]


def _row_run(ref, row, size):
    return ref.at[pl.ds(pl.multiple_of(row * ROW_TILE, ROW_TILE), size * ROW_TILE)]


def _run_copies(mt_ref, t, tm, make_copy):
    for e in range(N_EXPERTS):
        m = mt_ref[t, e]
        for size in _run_sizes(tm):
            @pl.when((m & size) != 0)
            def _(size=size, m=m, e=e):
                make_copy(e, m & ~(2 * size - 1), size).start()


MOE_PAIR = 2


def _moe_kernel(be_ref, sb_ref, ns_ref, tlo_ref, thi_ref, nv_ref, mt_ref, ot_ref, dt_ref,
                xs_ref, *refs, tile_rows):
    n_w = 6 * MOE_PAIR
    w_refs, y_ref = refs[:n_w], refs[n_w]
    bufs, sem_ref = refs[n_w + 1:n_w + 1 + MOE_PAIR], refs[n_w + 1 + MOE_PAIR]
    i = pl.program_id(0)
    n_used_steps = ns_ref[0]
    sizes = _run_sizes(MOE_BLOCK)

    def gather(b, half):
        e = be_ref[b]
        first_slot = b * MOE_BLOCK

        def per_tile(t, carry):
            run0 = dt_ref[t, e]
            lo = jnp.maximum(run0, first_slot)
            hi = jnp.minimum(run0 + mt_ref[t, e], first_slot + MOE_BLOCK)
            length = jnp.maximum(hi - lo, 0)
            src = t * tile_rows + ot_ref[t, e] + (lo - run0)
            dst = lo - first_slot
            for size in sizes:
                @pl.when((length & size) != 0)
                def _(size=size):
                    done = length & ~(2 * size - 1)
                    pltpu.make_async_copy(
                        _row_run(xs_ref, src + done, size), _row_run(bufs[half], dst + done, size),
                        sem_ref.at[half]).start()
            return carry

        lax.fori_loop(tlo_ref[b], thi_ref[b] + 1, per_tile, 0)

    def wait(b, half):
        valid = nv_ref[b]
        for size in sizes:
            @pl.when((valid & size) != 0)
            def _(size=size):
                pltpu.make_async_copy(
                    _row_run(xs_ref, 0, size), _row_run(bufs[half], 0, size), sem_ref.at[half]).wait()

    @pl.when(i == 0)
    def _():
        for half in range(MOE_PAIR):
            bufs[half][...] = jnp.zeros_like(bufs[half])
            gather(half, half)

    @pl.when(i < n_used_steps)
    def _():
        for half in range(MOE_PAIR):
            wg_ref, bg_ref, wu_ref, bu_ref, wd_ref, bd_ref = w_refs[6 * half:6 * half + 6]
            wait(MOE_PAIR * i + half, half)
            x = _load_row_tiles(bufs[half], MOE_BLOCK).astype(BF16)

            @pl.when(i + 1 < n_used_steps)
            def _(half=half):
                gather(MOE_PAIR * (i + 1) + half, half)

            hg = _dot(x, wg_ref[0]) + bg_ref[0]
            hu = _dot(x, wu_ref[0]) + bu_ref[0]
            hg = jnp.minimum(hg, SWIGLU_LIMIT)
            hu = jnp.clip(hu, -SWIGLU_LIMIT, SWIGLU_LIMIT)
            glu = hg * _sigmoid(SWIGLU_ALPHA * hg)
            act = ((hu + 1.0) * glu).astype(BF16)
            _store_row_tiles(y_ref, _dot(act, wd_ref[0]) + bd_ref[0], base=half * MOE_BLOCK)

    @pl.when(i >= n_used_steps)
    def _():
        y_ref[...] = jnp.zeros_like(y_ref)


def _moe(sched, xs, n_slots, tile_rows, wg, bg, wu, bu, wd, bd):
    d, f = wg.shape[1], wg.shape[2]
    step_rows = MOE_PAIR * MOE_BLOCK
    n_steps = n_slots // step_rows
    w_specs, w_args = [], []
    for half in range(MOE_PAIR):
        wmap = lambda i, be, sb, *_, half=half: (be[MOE_PAIR * sb[i] + half], 0, 0)
        w_specs += [pl.BlockSpec((1, d, f), wmap), pl.BlockSpec((1, 1, f), wmap),
                    pl.BlockSpec((1, d, f), wmap), pl.BlockSpec((1, 1, f), wmap),
                    pl.BlockSpec((1, f, d), wmap), pl.BlockSpec((1, 1, d), wmap)]
        w_args += [wg, bg, wu, bu, wd, bd]
    grid_spec = pltpu.PrefetchScalarGridSpec(
        num_scalar_prefetch=len(sched),
        grid=(n_steps,),
        in_specs=[pl.BlockSpec(memory_space=pl.ANY)] + w_specs,
        out_specs=pl.BlockSpec((step_rows * ROW_TILE, LANES), lambda i, *_: (i, 0)),
        scratch_shapes=[pltpu.VMEM((MOE_BLOCK * ROW_TILE, LANES), F32)] * MOE_PAIR
        + [pltpu.SemaphoreType.DMA((MOE_PAIR,))],
    )
    return pl.pallas_call(
        functools.partial(_moe_kernel, tile_rows=tile_rows),
        grid_spec=grid_spec,
        out_shape=jax.ShapeDtypeStruct((n_slots * ROW_TILE, LANES), F32),
        compiler_params=_cparams(1, V7X_VMEM_LIMIT_BYTES),
        name="moe",
    )(*sched, xs, *w_args)


def _combine_kernel(mt_ref, ot_ref, dt_ref, y_ref, pos_ref, h_ref, gate_ref, gain_ref, o_ref,
                    ybuf_ref, sem_ref, *, tm):
    t = pl.program_id(0)
    n_tiles = pl.num_programs(0)
    n_pairs = TOP_K * tm
    slot = t % 2

    def fetch(tt, s):
        _run_copies(mt_ref, tt, tm, lambda e, done, size: pltpu.make_async_copy(
            _row_run(y_ref, dt_ref[tt, e] + done, size),
            _row_run(ybuf_ref, s * n_pairs + ot_ref[tt, e] + done, size), sem_ref.at[s]))

    @pl.when(t == 0)
    def _():
        fetch(0, 0)

    @pl.when(t + 1 < n_tiles)
    def _():
        fetch(t + 1, 1 - slot)

    pltpu.make_async_copy(
        _row_run(y_ref, 0, n_pairs), _row_run(ybuf_ref, 0, n_pairs), sem_ref.at[slot]).wait()

    gate = gate_ref[...]
    pos = pos_ref[...]
    j_io = lax.broadcasted_iota(I32, (tm, n_pairs), 1).astype(F32)
    wmat = jnp.zeros((tm, n_pairs), F32)
    for k in range(TOP_K):
        wmat = wmat + jnp.where(j_io == pos[:, k:k + 1], gate[:, k:k + 1], 0.0)
    ys = _load_row_tiles(ybuf_ref, n_pairs, base=slot * n_pairs).astype(BF16)
    acc = h_ref[...] + _dot(wmat.astype(BF16), ys)
    var = jnp.mean(acc * acc, axis=-1, keepdims=True)
    o_ref[...] = (acc * lax.rsqrt(var + NORM_EPS) * gain_ref[...]).astype(o_ref.dtype)


def _combine(tabs, y_disp, pos_c, h, gates, gain, tm):
    n, d = h.shape
    row = lambda t, *_: (t, 0)
    fix = lambda t, *_: (0, 0)
    grid_spec = pltpu.PrefetchScalarGridSpec(
        num_scalar_prefetch=3,
        grid=(n // tm,),
        in_specs=[
            pl.BlockSpec(memory_space=pl.ANY),
            pl.BlockSpec((tm, LANES), row),
            pl.BlockSpec((tm, d), row),
            pl.BlockSpec((tm, LANES), row),
            pl.BlockSpec((1, d), fix),
        ],
        out_specs=pl.BlockSpec((tm, d), row),
        scratch_shapes=[
            pltpu.VMEM((2 * TOP_K * tm * ROW_TILE, LANES), F32),
            pltpu.SemaphoreType.DMA((2,)),
        ],
    )
    return pl.pallas_call(
        functools.partial(_combine_kernel, tm=tm),
        grid_spec=grid_spec,
        out_shape=jax.ShapeDtypeStruct((n, d), F32),
        compiler_params=_cparams(1, V7X_VMEM_LIMIT_BYTES),
        name="combine",
    )(*tabs[:3], y_disp, pos_c, h, gates, gain)


def _routing_tables(tile_cnt, n_pairs):
    counts = jnp.sum(tile_cnt, axis=0)
    padded = ((counts + MOE_BLOCK - 1) // MOE_BLOCK) * MOE_BLOCK
    cum_pad = jnp.cumsum(padded)
    start_pad = (cum_pad - padded).astype(I32)
    n_slots = ((n_pairs + MOE_BLOCK - 1) // MOE_BLOCK) * MOE_BLOCK + N_EXPERTS * MOE_BLOCK
    n_slots = ((n_slots + MOE_PAIR * MOE_BLOCK - 1) // (MOE_PAIR * MOE_BLOCK)) * (MOE_PAIR * MOE_BLOCK)
    nb = n_slots // MOE_BLOCK
    block_start = jnp.arange(nb, dtype=I32) * MOE_BLOCK
    block_e = jnp.minimum(jnp.sum(cum_pad[None, :] <= block_start[:, None], axis=1), N_EXPERTS - 1)
    block_e = block_e.astype(I32)
    n_used = (cum_pad[-1] // MOE_BLOCK).astype(I32)
    n_steps_used = (n_used + MOE_PAIR - 1) // MOE_PAIR
    step_blk = jnp.minimum(jnp.arange(nb // MOE_PAIR, dtype=I32), n_steps_used - 1)
    before_tile = jnp.cumsum(tile_cnt, axis=0) - tile_cnt
    run_dst = (start_pad[None, :] + before_tile).astype(I32)
    run_src = (jnp.cumsum(tile_cnt, axis=1) - tile_cnt).astype(I32)
    dst_b = run_dst[:, block_e]
    end_b = dst_b + tile_cnt[:, block_e]
    tile_lo = jnp.sum(end_b <= block_start[None, :], axis=0).astype(I32)
    tile_hi = (jnp.sum(dst_b < block_start[None, :] + MOE_BLOCK, axis=0) - 1).astype(I32)
    used = jnp.arange(nb, dtype=I32) < n_used
    n_valid = jnp.clip((start_pad + counts)[block_e] - block_start, 0, MOE_BLOCK)
    n_valid = jnp.where(used, n_valid, 0).astype(I32)
    tile_hi = jnp.where(used, tile_hi, tile_lo - 1)
    run_tabs = (tile_cnt.astype(I32), run_src, run_dst)
    sched = (block_e, step_blk, n_steps_used.reshape(1).astype(I32), tile_lo, tile_hi, n_valid) + run_tabs
    return run_tabs, sched, n_slots


def kernel(x, w_in, w_out, hg_lb_logits, hg_norm_gain, sb_norm_gain, norm_mix_gain, norm_ffn_gain,
           w_router, b_router, w_gate, b_gate, w_up, b_up, w_down, b_down, norm_final_gain):
    b, s, d = x.shape
    n = b * s
    f = w_gate.shape[-1]
    assert w_in.shape[0] == 1 and hg_lb_logits.shape[0] == 2, "single-layer trunk only"
    x2 = x.reshape(n, d).astype(F32)
    r3 = lambda a: a.reshape(b, s, a.shape[-1])

    q, k, lf, v, g, sq, sk, sv = _in_proj(
        x2, norm_mix_gain[0].reshape(1, d), w_in[0].astype(BF16), hg_lb_logits.astype(F32))
    o_hg, (wg_bf, wu_bf, wd_bf) = _hgrn2(
        r3(q), r3(k), r3(lf), r3(v), r3(g), hg_norm_gain[0].reshape(1, HG_WIDTH),
        cast_f32=(w_gate[0], w_up[0], w_down[0]))
    o_sb = _stickbreak(r3(sq), r3(sk), r3(sv), sb_norm_gain[0].reshape(1, SB_WIDTH))
    h_mid, xs, gates, pos_c, cnt = _mix_router(
        o_hg.reshape(n, HG_WIDTH), o_sb.reshape(n, SB_WIDTH), x2, w_out[0].astype(BF16),
        norm_ffn_gain[0].reshape(1, d), w_router[0].T.astype(F32),
        b_router[0].reshape(N_EXPERTS, 1).astype(F32))
    tile_cnt = cnt[:, :, 0]
    run_tabs, sched, n_slots = _routing_tables(tile_cnt, n * TOP_K)
    tm = n // tile_cnt.shape[0]
    y_disp = _moe(sched, xs, n_slots, TOP_K * tm,
                  wg_bf, b_gate[0].reshape(N_EXPERTS, 1, f),
                  wu_bf, b_up[0].reshape(N_EXPERTS, 1, f),
                  wd_bf, b_down[0].reshape(N_EXPERTS, 1, d))
    out = _combine(run_tabs, y_disp, pos_c, h_mid, gates, norm_final_gain.reshape(1, d), tm)
    return out.reshape(b, s, d).astype(x.dtype)
```

```python
import functools

import jax
import jax.numpy as jnp
from jax import lax
from jax.experimental import pallas as pl
from jax.experimental.pallas import tpu as pltpu

F32 = jnp.float32
BF16 = jnp.bfloat16
I32 = jnp.int32

NORM_EPS = 1e-5
HG_HEADS = 4
HG_DK = 128
HG_WIDTH = HG_HEADS * HG_DK
HG_CHUNK = 64
HG_LEVELS = (32, 16, 8)
HG_DIAG = 8
HG_FACTORED_MAX_EXPONENT = 60.0
SB_HEADS = 8
SB_DH = 64
SB_WIDTH = SB_HEADS * SB_DH
SB_BLOCK = 128
SB_STATIC_BLOCKS = 3
SB_SKIP_THRESHOLD = 104.0
N_EXPERTS = 32
TOP_K = 4
MOE_BLOCK = 256
SWIGLU_LIMIT = 7.0
SWIGLU_ALPHA = 1.702
NEG_BIG = -1e30

V7X_VMEM_LIMIT_BYTES = 56 * 1024 * 1024
LANES = 128
RUN_ALIGN = 8


def _sorted_rows(tm):
    return TOP_K * tm + N_EXPERTS * RUN_ALIGN


def _cparams(n_axes, vmem_bytes=None):
    return pltpu.CompilerParams(
        dimension_semantics=("arbitrary",) * n_axes,
        vmem_limit_bytes=vmem_bytes,
    )


def _sigmoid(x):
    return 1.0 / (1.0 + jnp.exp(-x))


def _split2(x):
    hi = x.astype(BF16)
    lo = (x - hi.astype(F32)).astype(BF16)
    return hi, lo


def _split3(x):
    hi = x.astype(BF16)
    r = x - hi.astype(F32)
    mid = r.astype(BF16)
    lo = (r - mid.astype(F32)).astype(BF16)
    return hi, mid, lo


def _dot(a, b):
    return jnp.dot(a, b, preferred_element_type=F32)


def _dot_nt(a, b):
    return lax.dot_general(a, b, (((1,), (1,)), ((), ())), preferred_element_type=F32)


def _dot_tn(a, b):
    return lax.dot_general(a, b, (((0,), (0,)), ((), ())), preferred_element_type=F32)


def _in_proj_kernel(x_ref, gain_ref, w_ref, lbl_ref,
                    q_ref, k_ref, lf_ref, v_ref, g_ref, sq_ref, sk_ref, sv_ref):
    x = x_ref[...]
    var = jnp.mean(x * x, axis=-1, keepdims=True)
    u = (x * lax.rsqrt(var + NORM_EPS) * gain_ref[...]).astype(BF16)

    lbl = lbl_ref[...]
    mx = jnp.max(lbl, axis=0, keepdims=True)
    ex = jnp.exp(lbl - mx)
    lb = ex[0:1, :] / jnp.sum(ex, axis=0, keepdims=True)

    def seg(i):
        return _dot(u, w_ref[:, i * HG_WIDTH:(i + 1) * HG_WIDTH])

    hq = seg(0)
    q_ref[...] = hq * _sigmoid(hq)
    f_sig = _sigmoid(seg(1))
    lf_ref[...] = jnp.log(lb + (1.0 - lb) * f_sig)
    k_ref[...] = (1.0 - lb) * (1.0 - f_sig)
    v_ref[...] = seg(2)
    hg = seg(3)
    g_ref[...] = hg * _sigmoid(hg)
    sq_ref[...] = (seg(4) * (SB_DH ** -0.5)).astype(BF16)
    sk_ref[...] = seg(5).astype(BF16)
    sv_ref[...] = seg(6).astype(BF16)


def _in_proj(x2, gain, w_in_bf, lb_logits, tm=512):
    n, d = x2.shape
    cols = w_in_bf.shape[1]
    row = lambda i: (i, 0)
    fix = lambda i: (0, 0)
    o_f32 = jax.ShapeDtypeStruct((n, HG_WIDTH), F32)
    o_bf = jax.ShapeDtypeStruct((n, SB_WIDTH), BF16)
    return pl.pallas_call(
        _in_proj_kernel,
        grid=(n // tm,),
        in_specs=[
            pl.BlockSpec((tm, d), row),
            pl.BlockSpec((1, d), fix),
            pl.BlockSpec((d, cols), fix),
            pl.BlockSpec(lb_logits.shape, fix),
        ],
        out_specs=[pl.BlockSpec((tm, HG_WIDTH), row)] * 8,
        out_shape=[o_f32] * 5 + [o_bf] * 3,
        compiler_params=_cparams(1, V7X_VMEM_LIMIT_BYTES),
        name="in_proj",
    )(x2, gain, w_in_bf, lb_logits)


def _hgrn2_consts():
    c = HG_CHUNK
    t = jnp.arange(c)[:, None]
    s = jnp.arange(c)[None, :]
    mats = [(s <= t)]
    for lv in HG_LEVELS:
        ref = (t // (2 * lv)) * (2 * lv) + lv - 1
        mats.append(s <= ref)
    mats.append(s <= (t // HG_DIAG) * HG_DIAG)
    return jnp.concatenate(mats, axis=0).astype(BF16)


def _hgrn2_kernel(q_ref, k_ref, lf_ref, v_ref, g_ref, gain_ref, cmat_ref, *refs, ts, n_cast):
    c = HG_CHUNK
    w = HG_WIDTH
    cast_in, o_ref = refs[:n_cast], refs[n_cast]
    cast_out, st_ref = refs[n_cast + 1:2 * n_cast + 1], refs[2 * n_cast + 1]
    for src, dst in zip(cast_in, cast_out):
        dst[...] = src[...].astype(dst.dtype)

    @pl.when(pl.program_id(1) == 0)
    def _():
        st_ref[...] = jnp.zeros_like(st_ref)

    row_w = lax.broadcasted_iota(I32, (c, w), 0)
    row_c = lax.broadcasted_iota(I32, (c, c), 0)
    col_c = lax.broadcasted_iota(I32, (c, c), 1)
    row_d = lax.broadcasted_iota(I32, (HG_DIAG, w), 0)
    dk_bits = HG_DK.bit_length() - 1
    bd = ((lax.broadcasted_iota(I32, (w, w), 0) >> dk_bits)
          == (lax.broadcasted_iota(I32, (w, w), 1) >> dk_bits)).astype(BF16)
    cmat = cmat_ref[...]
    gain = gain_ref[...]

    diag_mask = jnp.logical_and((row_c >> 3) == (col_c >> 3), col_c <= row_c)

    def chunk(ci, carry, factored_diag):
        r0 = pl.multiple_of(ci * c, c)
        q = q_ref[0, pl.ds(r0, c), :]
        kk = k_ref[0, pl.ds(r0, c), :]
        lf = lf_ref[0, pl.ds(r0, c), :]
        v = v_ref[0, pl.ds(r0, c), :]
        g = g_ref[0, pl.ds(r0, c), :]

        lf_h, lf_l = _split2(lf)
        gg = _dot(cmat, lf_h) + _dot(cmat, lf_l)
        G = gg[0:c]

        scores = [jnp.zeros((c, c), F32) for _ in range(HG_HEADS)]
        for li, lv in enumerate(HG_LEVELS):
            gref = gg[(li + 1) * c:(li + 2) * c]
            is_q = (row_w & (2 * lv - 1)) >= lv
            e = jnp.exp(jnp.where(is_q, G - gref, gref - G))
            ql = jnp.where(is_q, q * e, 0.0).astype(BF16)
            kl = jnp.where(is_q, 0.0, kk * e).astype(BF16)
            grp_bits = (2 * lv).bit_length() - 1
            same = (row_c >> grp_bits) == (col_c >> grp_bits)
            for h in range(HG_HEADS):
                sl = slice(h * HG_DK, (h + 1) * HG_DK)
                scores[h] = scores[h] + jnp.where(same, _dot_nt(ql[:, sl], kl[:, sl]), 0.0)

        if factored_diag:
            gref = gg[(len(HG_LEVELS) + 1) * c:(len(HG_LEVELS) + 2) * c]
            qd = (q * jnp.exp(G - gref)).astype(BF16)
            kd = (kk * jnp.exp(gref - G)).astype(BF16)
            for h in range(HG_HEADS):
                sl = slice(h * HG_DK, (h + 1) * HG_DK)
                scores[h] = scores[h] + jnp.where(diag_mask, _dot_nt(qd[:, sl], kd[:, sl]), 0.0)
            o = jnp.zeros((c, w), F32)
        else:
            tiles = []
            for b in range(c // HG_DIAG):
                rs = slice(b * HG_DIAG, (b + 1) * HG_DIAG)
                gb, qb, kb = G[rs], q[rs], kk[rs]
                for s in range(HG_DIAG):
                    gs = jnp.broadcast_to(gb[s:s + 1, :], (HG_DIAG, w))
                    ks = jnp.broadcast_to(kb[s:s + 1, :], (HG_DIAG, w))
                    e = jnp.exp(jnp.where(row_d >= s, gb - gs, NEG_BIG))
                    tiles.append(qb * e * ks)
            p_all = jnp.concatenate(tiles, axis=0)
            r_all = _dot(p_all.astype(BF16), bd)
            o_blocks = []
            for b in range(c // HG_DIAG):
                vb = v[b * HG_DIAG:(b + 1) * HG_DIAG]
                ob = jnp.zeros((HG_DIAG, w), F32)
                for s in range(HG_DIAG):
                    i0 = (b * HG_DIAG + s) * HG_DIAG
                    vs = jnp.broadcast_to(vb[s:s + 1, :], (HG_DIAG, w))
                    ob = ob + r_all[i0:i0 + HG_DIAG] * vs
                o_blocks.append(ob)
            o = jnp.concatenate(o_blocks, axis=0)

        qg = (q * jnp.exp(G)).astype(BF16)
        g_last = jnp.broadcast_to(G[c - 1:c, :], (c, w))
        kh = (kk * jnp.exp(g_last - G)).astype(BF16)
        dec = jnp.exp(G[c - 1:c, :])
        v_bf = v.astype(BF16)
        outs = []
        for h in range(HG_HEADS):
            sl = slice(h * HG_DK, (h + 1) * HG_DK)
            st = st_ref[h]
            oh = (o[:, sl]
                  + _dot(scores[h].astype(BF16), v_bf[:, sl])
                  + _dot_nt(qg[:, sl], st.astype(BF16)))
            st_ref[h] = st * dec[:, sl] + _dot_tn(v_bf[:, sl], kh[:, sl])
            var = jnp.mean(oh * oh, axis=-1, keepdims=True)
            outs.append(oh * lax.rsqrt(var + NORM_EPS))
        on = jnp.concatenate(outs, axis=-1) * gain * g
        o_ref[0, pl.ds(r0, c), :] = on.astype(o_ref.dtype)
        return carry

    worst = jnp.max(-lf_ref[0]) * (HG_DIAG - 1)

    def run(factored_diag):
        lax.fori_loop(0, ts // c, functools.partial(chunk, factored_diag=factored_diag), 0, unroll=True)

    lax.cond(worst < HG_FACTORED_MAX_EXPONENT, lambda: run(True), lambda: run(False))


CAST_BLOCK_BYTES_MAX = 2 * 1024 * 1024


def _hgrn2(q, k, lf, v, g, gain, cast_f32=(), ts=256):
    b, s, w = q.shape
    n_s = s // ts
    blk = lambda bi, si: (bi, si, 0)
    fix = lambda bi, si: (0, 0)
    cmat = _hgrn2_consts()
    n_steps = b * n_s
    flat = [a.reshape(-1, a.shape[-1]) for a in cast_f32]
    rows = [a.shape[0] // n_steps for a in flat]
    riding = all(a.shape[0] % n_steps == 0 and r % 16 == 0 and r * a.shape[1] * 4 <= CAST_BLOCK_BYTES_MAX
                 for a, r in zip(flat, rows))
    if not riding:
        flat, rows = [], []
    cast_specs = [pl.BlockSpec((r, a.shape[1]), lambda bi, si: (bi * n_s + si, 0)) for a, r in zip(flat, rows)]
    outs = pl.pallas_call(
        functools.partial(_hgrn2_kernel, ts=ts, n_cast=len(flat)),
        grid=(b, n_s),
        in_specs=[pl.BlockSpec((1, ts, w), blk)] * 5 + [
            pl.BlockSpec((1, w), fix),
            pl.BlockSpec(cmat.shape, fix),
        ] + cast_specs,
        out_specs=[pl.BlockSpec((1, ts, w), blk)] + cast_specs,
        out_shape=[jax.ShapeDtypeStruct((b, s, w), BF16)]
        + [jax.ShapeDtypeStruct(a.shape, BF16) for a in flat],
        scratch_shapes=[pltpu.VMEM((HG_HEADS, HG_DK, HG_DK), F32)],
        compiler_params=_cparams(2, V7X_VMEM_LIMIT_BYTES),
        name="hgrn2",
    )(q, k, lf, v, g, gain, cmat, *flat)
    if riding:
        casted = [o.reshape(a.shape) for o, a in zip(outs[1:], cast_f32)]
    else:
        casted = [a.astype(BF16) for a in cast_f32]
    return outs[0], casted


def _sb_kernel(q_ref, k0_ref, k1_ref, k2_ref, v0_ref, v1_ref, v2_ref, gain_ref, kall_ref, vall_ref,
               o_ref, acc_ref, out_ref, kbuf_ref, vbuf_ref, sem_ref):
    tb = SB_BLOCK
    bi = pl.program_id(0)
    qi = pl.program_id(1)
    n_pairs = SB_HEADS // 2
    pair_w = 2 * SB_DH

    acc_ref[...] = jnp.zeros_like(acc_ref)
    out_ref[...] = jnp.zeros_like(out_ref)

    t_io = lax.broadcasted_iota(I32, (SB_HEADS * tb, tb), 0) & (tb - 1)
    s_io = lax.broadcasted_iota(I32, (SB_HEADS * tb, tb), 1)
    causal = s_io < t_io
    u_row = lax.broadcasted_iota(I32, (tb, 2 * tb), 0)
    u_col = lax.broadcasted_iota(I32, (tb, 2 * tb), 1)
    um = jnp.where(jnp.logical_or(u_col >= tb, u_row > u_col), 1.0, 0.0).astype(BF16)
    lane = lax.broadcasted_iota(I32, (tb, pair_w), 1)
    lo_half = lane < SB_DH
    keeps = (lo_half, jnp.logical_not(lo_half))

    def process(blocks):
        pre = []
        for load_k, _, diag, wgt in blocks:
            zs = []
            for p in range(n_pairs):
                sl = slice(p * pair_w, (p + 1) * pair_w)
                q2 = q_ref[0, :, sl]
                k2 = load_k(sl)
                zero = jnp.zeros_like(q2)
                for half in range(2):
                    zs.append(_dot_nt(jnp.where(keeps[half], q2, zero), k2))
            z = jnp.concatenate(zs, axis=0)
            sp_full = jnp.maximum(z, 0.0) + jnp.log(1.0 + jnp.exp(-jnp.abs(z)))
            sp = jnp.where(causal, sp_full, 0.0) if diag else sp_full
            if wgt is not None:
                sp = sp * wgt
            lt = _dot(sp.astype(BF16), um)
            pre.append(((z - sp_full) - lt[:, :tb], lt[:, tb:]))
        acc = acc_ref[...]
        for (_, load_v, diag, wgt), (base, total) in zip(blocks, pre):
            a = jnp.exp(base - acc)
            if diag:
                a = jnp.where(causal, a, 0.0)
            if wgt is not None:
                a = a * wgt
            a = a.astype(BF16)
            acc = acc + total
            for p in range(n_pairs):
                sl = slice(p * pair_w, (p + 1) * pair_w)
                v2 = load_v(sl)
                zero = jnp.zeros_like(v2)
                o_pair = jnp.zeros((tb, pair_w), F32)
                for half in range(2):
                    h = 2 * p + half
                    o_pair = o_pair + _dot(a[h * tb:(h + 1) * tb], jnp.where(keeps[half], v2, zero))
                out_ref[:, sl] += o_pair
        acc_ref[...] = acc
        return jnp.min(acc)

    def blocked(ref):
        return lambda sl: ref[0, :, sl]

    def whole(ref):
        return lambda sl: ref[:, sl]

    on1 = jnp.where(qi >= 1, 1.0, 0.0)
    on2 = jnp.where(qi >= 2, 1.0, 0.0)
    m2 = process([
        (blocked(k0_ref), blocked(v0_ref), True, None),
        (blocked(k1_ref), blocked(v1_ref), False, on1),
        (blocked(k2_ref), blocked(v2_ref), False, on2),
    ])

    def cond(cr):
        j, m = cr
        return jnp.logical_and(j >= 0, m <= SB_SKIP_THRESHOLD)

    def body(cr):
        j, _ = cr
        r0 = pl.multiple_of(j * tb, tb)
        ck = pltpu.make_async_copy(kall_ref.at[bi, pl.ds(r0, tb), :], kbuf_ref, sem_ref.at[0])
        cv = pltpu.make_async_copy(vall_ref.at[bi, pl.ds(r0, tb), :], vbuf_ref, sem_ref.at[1])
        ck.start()
        cv.start()
        ck.wait()
        cv.wait()
        return j - 1, process([(whole(kbuf_ref), whole(vbuf_ref), False, None)])

    lax.while_loop(cond, body, (qi - SB_STATIC_BLOCKS, m2))

    o = out_ref[...]
    wd = SB_WIDTH
    dh_bits = SB_DH.bit_length() - 1
    bd = ((lax.broadcasted_iota(I32, (wd, wd), 0) >> dh_bits)
          == (lax.broadcasted_iota(I32, (wd, wd), 1) >> dh_bits)).astype(BF16)
    sq_h, sq_l = _split2(o * o)
    var = (_dot(sq_h, bd) + _dot(sq_l, bd)) * (1.0 / SB_DH)
    o_ref[0] = (o * lax.rsqrt(var + NORM_EPS) * gain_ref[...]).astype(o_ref.dtype)


def _stickbreak(sq, sk, sv, gain):
    b, s, w = sq.shape
    tb = SB_BLOCK
    cur = lambda bi, qi: (bi, qi, 0)
    prev1 = lambda bi, qi: (bi, jnp.maximum(qi - 1, 0), 0)
    prev2 = lambda bi, qi: (bi, jnp.maximum(qi - 2, 0), 0)
    fix = lambda bi, qi: (0, 0)
    blk = (1, tb, w)
    return pl.pallas_call(
        _sb_kernel,
        grid=(b, s // tb),
        in_specs=[
            pl.BlockSpec(blk, cur),
            pl.BlockSpec(blk, cur), pl.BlockSpec(blk, prev1), pl.BlockSpec(blk, prev2),
            pl.BlockSpec(blk, cur), pl.BlockSpec(blk, prev1), pl.BlockSpec(blk, prev2),
            pl.BlockSpec((1, w), fix),
            pl.BlockSpec(memory_space=pl.ANY),
            pl.BlockSpec(memory_space=pl.ANY),
        ],
        out_specs=pl.BlockSpec(blk, cur),
        out_shape=jax.ShapeDtypeStruct((b, s, w), BF16),
        scratch_shapes=[
            pltpu.VMEM((SB_HEADS * tb, tb), F32),
            pltpu.VMEM((tb, w), F32),
            pltpu.VMEM((tb, w), BF16),
            pltpu.VMEM((tb, w), BF16),
            pltpu.SemaphoreType.DMA((2,)),
        ],
        compiler_params=_cparams(2, V7X_VMEM_LIMIT_BYTES),
        name="stickbreak",
    )(sq, sk, sk, sk, sv, sv, sv, gain, sk, sv)


def _mix_router_kernel(ohg_ref, osb_ref, x_ref, wout_ref, gffn_ref, wrt_ref, br_ref,
                       h_ref, xs_ref, gate_ref, pos_ref, cnt_ref, *, tm):
    h = (x_ref[...]
         + _dot(ohg_ref[...], wout_ref[0:HG_WIDTH, :])
         + _dot(osb_ref[...], wout_ref[HG_WIDTH:HG_WIDTH + SB_WIDTH, :]))
    h_ref[...] = h
    var = jnp.mean(h * h, axis=-1, keepdims=True)
    u = h * lax.rsqrt(var + NORM_EPS) * gffn_ref[...]

    u_h, u_l = _split2(u)
    w_h, w_l = _split2(wrt_ref[...])
    logits = _dot_nt(w_h, u_h) + _dot_nt(w_h, u_l) + _dot_nt(w_l, u_h) + br_ref[...]

    e_io = lax.broadcasted_iota(I32, (N_EXPERTS, tm), 0).astype(F32)
    vals = logits
    member = jnp.zeros((N_EXPERTS, tm), F32)
    top_v, top_i = [], []
    for _ in range(TOP_K):
        m = jnp.max(vals, axis=0, keepdims=True)
        idx = jnp.min(jnp.where(vals == m, e_io, float(N_EXPERTS)), axis=0, keepdims=True)
        sel = e_io == idx
        top_v.append(m)
        top_i.append(idx)
        member = member + jnp.where(sel, 1.0, 0.0)
        vals = jnp.where(sel, -jnp.inf, vals)

    ex = [jnp.exp(tv - top_v[0]) for tv in top_v]
    den = ex[0] + ex[1] + ex[2] + ex[3]
    gates = [e / den for e in ex]

    n_io = lax.broadcasted_iota(I32, (tm, tm), 0)
    m_io = lax.broadcasted_iota(I32, (tm, tm), 1)
    before = jnp.where(n_io < m_io, 1.0, 0.0).astype(BF16)
    cexcl = _dot(member.astype(BF16), before)
    cnt = jnp.sum(member, axis=1, keepdims=True)
    run_len = jnp.floor((cnt + (RUN_ALIGN - 1)) * (1.0 / RUN_ALIGN)) * RUN_ALIGN
    cnt_ref[0] = jnp.broadcast_to(run_len, cnt_ref.shape[1:]).astype(I32)

    ee_r = lax.broadcasted_iota(I32, (N_EXPERTS, N_EXPERTS), 0)
    ee_c = lax.broadcasted_iota(I32, (N_EXPERTS, N_EXPERTS), 1)
    lower = jnp.where(ee_c < ee_r, 1.0, 0.0).astype(BF16)
    c_h, c_l = _split2(jnp.broadcast_to(run_len, (N_EXPERTS, LANES)))
    run_start = (_dot(lower, c_h) + _dot(lower, c_l))[:, 0:1]
    where_in_tile = cexcl + run_start
    pos = [jnp.sum(jnp.where(e_io == ti, where_in_tile, 0.0), axis=0, keepdims=True) for ti in top_i]

    j_io = lax.broadcasted_iota(I32, (_sorted_rows(tm), tm), 0).astype(F32)
    hit = j_io == pos[0]
    for pk in pos[1:]:
        hit = jnp.logical_or(hit, j_io == pk)
    onehot = jnp.where(hit, 1.0, 0.0).astype(BF16)
    xs_ref[...] = _dot(onehot, u.astype(BF16))

    r_io = lax.broadcasted_iota(I32, (LANES, tm), 0)
    gfull = jnp.zeros((LANES, tm), F32)
    pfull = jnp.zeros((LANES, tm), F32)
    for kk_ in range(TOP_K):
        gfull = jnp.where(r_io == kk_, jnp.broadcast_to(gates[kk_], (LANES, tm)), gfull)
        pfull = jnp.where(r_io == kk_, jnp.broadcast_to(pos[kk_], (LANES, tm)), pfull)
    gate_ref[...] = gfull.T
    pos_ref[...] = pfull.T


def _mix_router(ohg, osb, x2, w_out_bf, g_ffn, w_router_t, b_router_col, tm=512):
    n, d = x2.shape
    tm = min(tm, n)
    row = lambda i: (i, 0)
    col = lambda i: (0, i)
    fix = lambda i: (0, 0)
    return pl.pallas_call(
        functools.partial(_mix_router_kernel, tm=tm),
        grid=(n // tm,),
        in_specs=[
            pl.BlockSpec((tm, HG_WIDTH), row),
            pl.BlockSpec((tm, SB_WIDTH), row),
            pl.BlockSpec((tm, d), row),
            pl.BlockSpec(w_out_bf.shape, fix),
            pl.BlockSpec((1, d), fix),
            pl.BlockSpec(w_router_t.shape, fix),
            pl.BlockSpec(b_router_col.shape, fix),
        ],
        out_specs=[
            pl.BlockSpec((tm, d), row),
            pl.BlockSpec((_sorted_rows(tm), d), row),
            pl.BlockSpec((tm, LANES), row),
            pl.BlockSpec((tm, LANES), row),
            pl.BlockSpec((1, N_EXPERTS, LANES), lambda i: (i, 0, 0)),
        ],
        out_shape=[
            jax.ShapeDtypeStruct((n, d), F32),
            jax.ShapeDtypeStruct((n // tm * _sorted_rows(tm), d), F32),
            jax.ShapeDtypeStruct((n, LANES), F32),
            jax.ShapeDtypeStruct((n, LANES), F32),
            jax.ShapeDtypeStruct((n // tm, N_EXPERTS, LANES), I32),
        ],
        compiler_params=_cparams(1, V7X_VMEM_LIMIT_BYTES),
        name="mix_router",
    )(ohg, osb, x2, w_out_bf, g_ffn, w_router_t, b_router_col)


def _run_sizes(limit):
    sizes = []
    s = RUN_ALIGN
    while s <= limit:
        sizes.append(s)
        s *= 2
    return sizes[::-1]


def _row_run(ref, row, size):
    return ref.at[pl.ds(pl.multiple_of(row, RUN_ALIGN), size)]


def _for_each_piece(m, limit, fn):
    for size in _run_sizes(limit):
        @pl.when((m & size) != 0)
        def _(size=size):
            fn(m & ~(2 * size - 1), size)


def _run_copies(mt_ref, t, tm, make_copy):
    for e in range(N_EXPERTS):
        _for_each_piece(mt_ref[t, e], tm, lambda done, size, e=e: make_copy(e, done, size).start())


MOE_PAIR = 2


def _moe_kernel(be_ref, sb_ref, ns_ref, tlo_ref, thi_ref, nv_ref, mt_ref, ot_ref, dt_ref,
                xs_ref, *refs, tile_rows):
    n_w = 6 * MOE_PAIR
    w_refs, y_ref = refs[:n_w], refs[n_w]
    bufs, sem_ref = refs[n_w + 1:n_w + 1 + MOE_PAIR], refs[n_w + 1 + MOE_PAIR]
    i = pl.program_id(0)
    n_used_steps = ns_ref[0]

    def gather(b, half):
        e = be_ref[b]
        first_slot = b * MOE_BLOCK

        def per_tile(t, carry):
            run0 = dt_ref[t, e]
            lo = jnp.maximum(run0, first_slot)
            hi = jnp.minimum(run0 + mt_ref[t, e], first_slot + MOE_BLOCK)
            src = t * tile_rows + ot_ref[t, e] + (lo - run0)
            dst = lo - first_slot
            _for_each_piece(jnp.maximum(hi - lo, 0), MOE_BLOCK, lambda done, size: pltpu.make_async_copy(
                _row_run(xs_ref, src + done, size), _row_run(bufs[half], dst + done, size),
                sem_ref.at[half]).start())
            return carry

        lax.fori_loop(tlo_ref[b], thi_ref[b] + 1, per_tile, 0)

    def wait(b, half):
        _for_each_piece(nv_ref[b], MOE_BLOCK, lambda done, size: pltpu.make_async_copy(
            _row_run(xs_ref, 0, size), _row_run(bufs[half], 0, size), sem_ref.at[half]).wait())

    @pl.when(i == 0)
    def _():
        for half in range(MOE_PAIR):
            bufs[half][...] = jnp.zeros_like(bufs[half])
            gather(half, half)

    @pl.when(i < n_used_steps)
    def _():
        xs_now = []
        for half in range(MOE_PAIR):
            wait(MOE_PAIR * i + half, half)
            xs_now.append(bufs[half][...].astype(BF16))

        @pl.when(i + 1 < n_used_steps)
        def _():
            for half in range(MOE_PAIR):
                gather(MOE_PAIR * (i + 1) + half, half)

        for half in range(MOE_PAIR):
            wg_ref, bg_ref, wu_ref, bu_ref, wd_ref, bd_ref = w_refs[6 * half:6 * half + 6]
            x = xs_now[half]
            hg = _dot(x, wg_ref[0]) + bg_ref[0]
            hu = _dot(x, wu_ref[0]) + bu_ref[0]
            hg = jnp.minimum(hg, SWIGLU_LIMIT)
            hu = jnp.clip(hu, -SWIGLU_LIMIT, SWIGLU_LIMIT)
            glu = hg * _sigmoid(SWIGLU_ALPHA * hg)
            act = ((hu + 1.0) * glu).astype(BF16)
            y_ref[half * MOE_BLOCK:(half + 1) * MOE_BLOCK, :] = _dot(act, wd_ref[0]) + bd_ref[0]

    @pl.when(i >= n_used_steps)
    def _():
        y_ref[...] = jnp.zeros_like(y_ref)


def _moe(sched, xs, n_slots, tile_rows, wg, bg, wu, bu, wd, bd):
    d, f = wg.shape[1], wg.shape[2]
    step_rows = MOE_PAIR * MOE_BLOCK
    n_steps = n_slots // step_rows
    w_specs, w_args = [], []
    for half in range(MOE_PAIR):
        wmap = lambda i, be, sb, *_, half=half: (be[MOE_PAIR * sb[i] + half], 0, 0)
        w_specs += [pl.BlockSpec((1, d, f), wmap), pl.BlockSpec((1, 1, f), wmap),
                    pl.BlockSpec((1, d, f), wmap), pl.BlockSpec((1, 1, f), wmap),
                    pl.BlockSpec((1, f, d), wmap), pl.BlockSpec((1, 1, d), wmap)]
        w_args += [wg, bg, wu, bu, wd, bd]
    grid_spec = pltpu.PrefetchScalarGridSpec(
        num_scalar_prefetch=len(sched),
        grid=(n_steps,),
        in_specs=[pl.BlockSpec(memory_space=pl.ANY)] + w_specs,
        out_specs=pl.BlockSpec((step_rows, d), lambda i, *_: (i, 0)),
        scratch_shapes=[pltpu.VMEM((MOE_BLOCK, d), F32)] * MOE_PAIR
        + [pltpu.SemaphoreType.DMA((MOE_PAIR,))],
    )
    return pl.pallas_call(
        functools.partial(_moe_kernel, tile_rows=tile_rows),
        grid_spec=grid_spec,
        out_shape=jax.ShapeDtypeStruct((n_slots, d), F32),
        compiler_params=_cparams(1, V7X_VMEM_LIMIT_BYTES),
        name="moe",
    )(*sched, xs, *w_args)


def _combine_kernel(mt_ref, ot_ref, dt_ref, tot_ref, y_ref, pos_ref, h_ref, gate_ref, gain_ref, o_ref,
                    ybuf_ref, sem_ref, *, tm):
    t = pl.program_id(0)
    n_tiles = pl.num_programs(0)
    srows = _sorted_rows(tm)
    slot = t % 2

    def fetch(tt, s):
        _run_copies(mt_ref, tt, tm, lambda e, done, size: pltpu.make_async_copy(
            _row_run(y_ref, dt_ref[tt, e] + done, size),
            _row_run(ybuf_ref, s * srows + ot_ref[tt, e] + done, size), sem_ref.at[s]))

    @pl.when(t == 0)
    def _():
        ybuf_ref[...] = jnp.zeros_like(ybuf_ref)
        fetch(0, 0)

    @pl.when(t + 1 < n_tiles)
    def _():
        fetch(t + 1, 1 - slot)

    _for_each_piece(tot_ref[t], srows, lambda done, size: pltpu.make_async_copy(
        _row_run(y_ref, 0, size), _row_run(ybuf_ref, 0, size), sem_ref.at[slot]).wait())

    gate = gate_ref[...]
    pos = pos_ref[...]
    j_io = lax.broadcasted_iota(I32, (tm, srows), 1).astype(F32)
    wmat = jnp.zeros((tm, srows), F32)
    for k in range(TOP_K):
        wmat = wmat + jnp.where(j_io == pos[:, k:k + 1], gate[:, k:k + 1], 0.0)
    ys = ybuf_ref[pl.ds(pl.multiple_of(slot * srows, RUN_ALIGN), srows), :].astype(BF16)
    acc = h_ref[...] + _dot(wmat.astype(BF16), ys)
    var = jnp.mean(acc * acc, axis=-1, keepdims=True)
    o_ref[...] = (acc * lax.rsqrt(var + NORM_EPS) * gain_ref[...]).astype(o_ref.dtype)


def _combine(tabs, y_disp, pos_c, h, gates, gain, tm):
    n, d = h.shape
    row = lambda t, *_: (t, 0)
    fix = lambda t, *_: (0, 0)
    grid_spec = pltpu.PrefetchScalarGridSpec(
        num_scalar_prefetch=len(tabs),
        grid=(n // tm,),
        in_specs=[
            pl.BlockSpec(memory_space=pl.ANY),
            pl.BlockSpec((tm, LANES), row),
            pl.BlockSpec((tm, d), row),
            pl.BlockSpec((tm, LANES), row),
            pl.BlockSpec((1, d), fix),
        ],
        out_specs=pl.BlockSpec((tm, d), row),
        scratch_shapes=[
            pltpu.VMEM((2 * _sorted_rows(tm), d), F32),
            pltpu.SemaphoreType.DMA((2,)),
        ],
    )
    return pl.pallas_call(
        functools.partial(_combine_kernel, tm=tm),
        grid_spec=grid_spec,
        out_shape=jax.ShapeDtypeStruct((n, d), F32),
        compiler_params=_cparams(1, V7X_VMEM_LIMIT_BYTES),
        name="combine",
    )(*tabs, y_disp, pos_c, h, gates, gain)


def _routing_tables(tile_cnt, n_pairs):
    counts = jnp.sum(tile_cnt, axis=0)
    padded = ((counts + MOE_BLOCK - 1) // MOE_BLOCK) * MOE_BLOCK
    cum_pad = jnp.cumsum(padded)
    start_pad = (cum_pad - padded).astype(I32)
    step_rows = MOE_PAIR * MOE_BLOCK
    n_slots = n_pairs + tile_cnt.shape[0] * N_EXPERTS * RUN_ALIGN + N_EXPERTS * MOE_BLOCK
    n_slots = ((n_slots + step_rows - 1) // step_rows) * step_rows
    nb = n_slots // MOE_BLOCK
    block_start = jnp.arange(nb, dtype=I32) * MOE_BLOCK
    block_e = jnp.minimum(jnp.sum(cum_pad[None, :] <= block_start[:, None], axis=1), N_EXPERTS - 1)
    block_e = block_e.astype(I32)
    n_used = (cum_pad[-1] // MOE_BLOCK).astype(I32)
    n_steps_used = (n_used + MOE_PAIR - 1) // MOE_PAIR
    step_blk = jnp.minimum(jnp.arange(nb // MOE_PAIR, dtype=I32), n_steps_used - 1)
    before_tile = jnp.cumsum(tile_cnt, axis=0) - tile_cnt
    run_dst = (start_pad[None, :] + before_tile).astype(I32)
    run_src = (jnp.cumsum(tile_cnt, axis=1) - tile_cnt).astype(I32)
    dst_b = run_dst[:, block_e]
    end_b = dst_b + tile_cnt[:, block_e]
    tile_lo = jnp.sum(end_b <= block_start[None, :], axis=0).astype(I32)
    tile_hi = (jnp.sum(dst_b < block_start[None, :] + MOE_BLOCK, axis=0) - 1).astype(I32)
    used = jnp.arange(nb, dtype=I32) < n_used
    n_valid = jnp.clip((start_pad + counts)[block_e] - block_start, 0, MOE_BLOCK)
    n_valid = jnp.where(used, n_valid, 0).astype(I32)
    tile_hi = jnp.where(used, tile_hi, tile_lo - 1)
    run_tabs = (tile_cnt.astype(I32), run_src, run_dst)
    sched = (block_e, step_blk, n_steps_used.reshape(1).astype(I32), tile_lo, tile_hi, n_valid) + run_tabs
    tile_total = jnp.sum(tile_cnt, axis=1).astype(I32)
    return run_tabs + (tile_total,), sched, n_slots


def kernel(x, w_in, w_out, hg_lb_logits, hg_norm_gain, sb_norm_gain, norm_mix_gain, norm_ffn_gain,
           w_router, b_router, w_gate, b_gate, w_up, b_up, w_down, b_down, norm_final_gain):
    b, s, d = x.shape
    n = b * s
    f = w_gate.shape[-1]
    assert w_in.shape[0] == 1 and hg_lb_logits.shape[0] == 2, "single-layer trunk only"
    x2 = x.reshape(n, d).astype(F32)
    r3 = lambda a: a.reshape(b, s, a.shape[-1])

    q, k, lf, v, g, sq, sk, sv = _in_proj(
        x2, norm_mix_gain[0].reshape(1, d), w_in[0].astype(BF16), hg_lb_logits.astype(F32))
    o_hg, (wg_bf, wu_bf, wd_bf) = _hgrn2(
        r3(q), r3(k), r3(lf), r3(v), r3(g), hg_norm_gain[0].reshape(1, HG_WIDTH),
        cast_f32=(w_gate[0], w_up[0], w_down[0]))
    o_sb = _stickbreak(r3(sq), r3(sk), r3(sv), sb_norm_gain[0].reshape(1, SB_WIDTH))
    h_mid, xs, gates, pos_c, cnt = _mix_router(
        o_hg.reshape(n, HG_WIDTH), o_sb.reshape(n, SB_WIDTH), x2, w_out[0].astype(BF16),
        norm_ffn_gain[0].reshape(1, d), w_router[0].T.astype(F32),
        b_router[0].reshape(N_EXPERTS, 1).astype(F32))
    tile_cnt = cnt[:, :, 0]
    run_tabs, sched, n_slots = _routing_tables(tile_cnt, n * TOP_K)
    tm = n // tile_cnt.shape[0]
    y_disp = _moe(sched, xs, n_slots, _sorted_rows(tm),
                  wg_bf, b_gate[0].reshape(N_EXPERTS, 1, f),
                  wu_bf, b_up[0].reshape(N_EXPERTS, 1, f),
                  wd_bf, b_down[0].reshape(N_EXPERTS, 1, d))
    out = _combine(run_tabs, y_disp, pos_c, h_mid, gates, norm_final_gain.reshape(1, d), tm)
    return out.reshape(b, s, d).astype(x.dtype)
```

```python
import functools

import jax
import jax.numpy as jnp
from jax import lax
from jax.experimental import pallas as pl
from jax.experimental.pallas import tpu as pltpu

F32 = jnp.float32
BF16 = jnp.bfloat16
I32 = jnp.int32

NORM_EPS = 1e-5
HG_HEADS = 4
HG_DK = 128
HG_WIDTH = HG_HEADS * HG_DK
HG_CHUNK = 64
HG_LEVELS = (32, 16, 8)
HG_DIAG = 8
HG_FACTORED_MAX_EXPONENT = 60.0
SB_HEADS = 8
SB_DH = 64
SB_WIDTH = SB_HEADS * SB_DH
SB_BLOCK = 128
SB_STATIC_BLOCKS = 3
SB_SKIP_THRESHOLD = 104.0
N_EXPERTS = 32
TOP_K = 4
MOE_BLOCK = 256
SWIGLU_LIMIT = 7.0
SWIGLU_ALPHA = 1.702
NEG_BIG = -1e30

V7X_VMEM_LIMIT_BYTES = 56 * 1024 * 1024
LANES = 128
RUN_ALIGN = 8


def _sorted_rows(tm):
    return TOP_K * tm + N_EXPERTS * RUN_ALIGN


def _cparams(n_axes, vmem_bytes=None):
    return pltpu.CompilerParams(
        dimension_semantics=("arbitrary",) * n_axes,
        vmem_limit_bytes=vmem_bytes,
    )


def _sigmoid(x):
    return 1.0 / (1.0 + jnp.exp(-x))


def _split2(x):
    hi = x.astype(BF16)
    lo = (x - hi.astype(F32)).astype(BF16)
    return hi, lo


def _dot(a, b):
    return jnp.dot(a, b, preferred_element_type=F32)


def _dot_nt(a, b):
    return lax.dot_general(a, b, (((1,), (1,)), ((), ())), preferred_element_type=F32)


def _dot_tn(a, b):
    return lax.dot_general(a, b, (((0,), (0,)), ((), ())), preferred_element_type=F32)


def _in_proj_kernel(x_ref, gain_ref, w_ref, lbl_ref,
                    q_ref, k_ref, lf_ref, v_ref, g_ref, sq_ref, sk_ref, sv_ref):
    x = x_ref[...]
    var = jnp.mean(x * x, axis=-1, keepdims=True)
    u = (x * lax.rsqrt(var + NORM_EPS) * gain_ref[...]).astype(BF16)

    lbl = lbl_ref[...]
    mx = jnp.max(lbl, axis=0, keepdims=True)
    ex = jnp.exp(lbl - mx)
    lb = ex[0:1, :] / jnp.sum(ex, axis=0, keepdims=True)

    def seg(i):
        return _dot(u, w_ref[:, i * HG_WIDTH:(i + 1) * HG_WIDTH])

    hq = seg(0)
    q_ref[...] = hq * _sigmoid(hq)
    f_sig = _sigmoid(seg(1))
    lf_ref[...] = jnp.log(lb + (1.0 - lb) * f_sig)
    k_ref[...] = (1.0 - lb) * (1.0 - f_sig)
    v_ref[...] = seg(2)
    hg = seg(3)
    g_ref[...] = hg * _sigmoid(hg)
    sq_ref[...] = (seg(4) * (SB_DH ** -0.5)).astype(BF16)
    sk_ref[...] = seg(5).astype(BF16)
    sv_ref[...] = seg(6).astype(BF16)


def _in_proj(x2, gain, w_in_bf, lb_logits, tm=512):
    n, d = x2.shape
    cols = w_in_bf.shape[1]
    row = lambda i: (i, 0)
    fix = lambda i: (0, 0)
    o_f32 = jax.ShapeDtypeStruct((n, HG_WIDTH), F32)
    o_bf = jax.ShapeDtypeStruct((n, SB_WIDTH), BF16)
    return pl.pallas_call(
        _in_proj_kernel,
        grid=(n // tm,),
        in_specs=[
            pl.BlockSpec((tm, d), row),
            pl.BlockSpec((1, d), fix),
            pl.BlockSpec((d, cols), fix),
            pl.BlockSpec(lb_logits.shape, fix),
        ],
        out_specs=[pl.BlockSpec((tm, HG_WIDTH), row)] * 8,
        out_shape=[o_f32] * 5 + [o_bf] * 3,
        compiler_params=_cparams(1, V7X_VMEM_LIMIT_BYTES),
        name="in_proj",
    )(x2, gain, w_in_bf, lb_logits)


def _hgrn2_consts():
    c = HG_CHUNK
    t = jnp.arange(c)[:, None]
    s = jnp.arange(c)[None, :]
    mats = [(s <= t)]
    for lv in HG_LEVELS:
        ref = (t // (2 * lv)) * (2 * lv) + lv - 1
        mats.append(s <= ref)
    mats.append(s <= (t // HG_DIAG) * HG_DIAG)
    return jnp.concatenate(mats, axis=0).astype(BF16)


def _hgrn2_kernel(q_ref, k_ref, lf_ref, v_ref, g_ref, gain_ref, cmat_ref, *refs, ts, n_cast):
    c = HG_CHUNK
    w = HG_WIDTH
    cast_in, o_ref = refs[:n_cast], refs[n_cast]
    cast_out, st_ref = refs[n_cast + 1:2 * n_cast + 1], refs[2 * n_cast + 1]
    for src, dst in zip(cast_in, cast_out):
        dst[...] = src[...].astype(dst.dtype)

    @pl.when(pl.program_id(1) == 0)
    def _():
        st_ref[...] = jnp.zeros_like(st_ref)

    row_w = lax.broadcasted_iota(I32, (c, w), 0)
    row_c = lax.broadcasted_iota(I32, (c, c), 0)
    col_c = lax.broadcasted_iota(I32, (c, c), 1)
    row_d = lax.broadcasted_iota(I32, (HG_DIAG, w), 0)
    dk_bits = HG_DK.bit_length() - 1
    bd = ((lax.broadcasted_iota(I32, (w, w), 0) >> dk_bits)
          == (lax.broadcasted_iota(I32, (w, w), 1) >> dk_bits)).astype(BF16)
    cmat = cmat_ref[...]
    gain = gain_ref[...]

    diag_mask = jnp.logical_and((row_c >> 3) == (col_c >> 3), col_c <= row_c)

    def chunk(ci, carry, factored_diag):
        r0 = pl.multiple_of(ci * c, c)
        q = q_ref[0, pl.ds(r0, c), :]
        kk = k_ref[0, pl.ds(r0, c), :]
        lf = lf_ref[0, pl.ds(r0, c), :]
        v = v_ref[0, pl.ds(r0, c), :]
        g = g_ref[0, pl.ds(r0, c), :]

        lf_h, lf_l = _split2(lf)
        gg = _dot(cmat, lf_h) + _dot(cmat, lf_l)
        G = gg[0:c]

        scores = [jnp.zeros((c, c), F32) for _ in range(HG_HEADS)]
        for li, lv in enumerate(HG_LEVELS):
            gref = gg[(li + 1) * c:(li + 2) * c]
            is_q = (row_w & (2 * lv - 1)) >= lv
            e = jnp.exp(jnp.where(is_q, G - gref, gref - G))
            ql = jnp.where(is_q, q * e, 0.0).astype(BF16)
            kl = jnp.where(is_q, 0.0, kk * e).astype(BF16)
            grp_bits = (2 * lv).bit_length() - 1
            same = (row_c >> grp_bits) == (col_c >> grp_bits)
            for h in range(HG_HEADS):
                sl = slice(h * HG_DK, (h + 1) * HG_DK)
                scores[h] = scores[h] + jnp.where(same, _dot_nt(ql[:, sl], kl[:, sl]), 0.0)

        if factored_diag:
            gref = gg[(len(HG_LEVELS) + 1) * c:(len(HG_LEVELS) + 2) * c]
            qd = (q * jnp.exp(G - gref)).astype(BF16)
            kd = (kk * jnp.exp(gref - G)).astype(BF16)
            for h in range(HG_HEADS):
                sl = slice(h * HG_DK, (h + 1) * HG_DK)
                scores[h] = scores[h] + jnp.where(diag_mask, _dot_nt(qd[:, sl], kd[:, sl]), 0.0)
            o = jnp.zeros((c, w), F32)
        else:
            tiles = []
            for b in range(c // HG_DIAG):
                rs = slice(b * HG_DIAG, (b + 1) * HG_DIAG)
                gb, qb, kb = G[rs], q[rs], kk[rs]
                for s in range(HG_DIAG):
                    gs = jnp.broadcast_to(gb[s:s + 1, :], (HG_DIAG, w))
                    ks = jnp.broadcast_to(kb[s:s + 1, :], (HG_DIAG, w))
                    e = jnp.exp(jnp.where(row_d >= s, gb - gs, NEG_BIG))
                    tiles.append(qb * e * ks)
            p_all = jnp.concatenate(tiles, axis=0)
            r_all = _dot(p_all.astype(BF16), bd)
            o_blocks = []
            for b in range(c // HG_DIAG):
                vb = v[b * HG_DIAG:(b + 1) * HG_DIAG]
                ob = jnp.zeros((HG_DIAG, w), F32)
                for s in range(HG_DIAG):
                    i0 = (b * HG_DIAG + s) * HG_DIAG
                    vs = jnp.broadcast_to(vb[s:s + 1, :], (HG_DIAG, w))
                    ob = ob + r_all[i0:i0 + HG_DIAG] * vs
                o_blocks.append(ob)
            o = jnp.concatenate(o_blocks, axis=0)

        qg = (q * jnp.exp(G)).astype(BF16)
        g_last = jnp.broadcast_to(G[c - 1:c, :], (c, w))
        kh = (kk * jnp.exp(g_last - G)).astype(BF16)
        dec = jnp.exp(G[c - 1:c, :])
        v_bf = v.astype(BF16)
        outs = []
        for h in range(HG_HEADS):
            sl = slice(h * HG_DK, (h + 1) * HG_DK)
            st = st_ref[h]
            oh = (o[:, sl]
                  + _dot(scores[h].astype(BF16), v_bf[:, sl])
                  + _dot_nt(qg[:, sl], st.astype(BF16)))
            st_ref[h] = st * dec[:, sl] + _dot_tn(v_bf[:, sl], kh[:, sl])
            var = jnp.mean(oh * oh, axis=-1, keepdims=True)
            outs.append(oh * lax.rsqrt(var + NORM_EPS))
        on = jnp.concatenate(outs, axis=-1) * gain * g
        o_ref[0, pl.ds(r0, c), :] = on.astype(o_ref.dtype)
        return carry

    worst = jnp.max(-lf_ref[0]) * (HG_DIAG - 1)

    def run(factored_diag):
        lax.fori_loop(0, ts // c, functools.partial(chunk, factored_diag=factored_diag), 0, unroll=True)

    lax.cond(worst < HG_FACTORED_MAX_EXPONENT, lambda: run(True), lambda: run(False))


CAST_BLOCK_BYTES_MAX = 2 * 1024 * 1024


def _hgrn2(q, k, lf, v, g, gain, cast_f32=(), ts=256):
    b, s, w = q.shape
    n_s = s // ts
    blk = lambda bi, si: (bi, si, 0)
    fix = lambda bi, si: (0, 0)
    cmat = _hgrn2_consts()
    n_steps = b * n_s
    flat = [a.reshape(-1, a.shape[-1]) for a in cast_f32]
    rows = [a.shape[0] // n_steps for a in flat]
    riding = all(a.shape[0] % n_steps == 0 and r % 16 == 0 and r * a.shape[1] * 4 <= CAST_BLOCK_BYTES_MAX
                 for a, r in zip(flat, rows))
    if not riding:
        flat, rows = [], []
    cast_specs = [pl.BlockSpec((r, a.shape[1]), lambda bi, si: (bi * n_s + si, 0)) for a, r in zip(flat, rows)]
    outs = pl.pallas_call(
        functools.partial(_hgrn2_kernel, ts=ts, n_cast=len(flat)),
        grid=(b, n_s),
        in_specs=[pl.BlockSpec((1, ts, w), blk)] * 5 + [
            pl.BlockSpec((1, w), fix),
            pl.BlockSpec(cmat.shape, fix),
        ] + cast_specs,
        out_specs=[pl.BlockSpec((1, ts, w), blk)] + cast_specs,
        out_shape=[jax.ShapeDtypeStruct((b, s, w), BF16)]
        + [jax.ShapeDtypeStruct(a.shape, BF16) for a in flat],
        scratch_shapes=[pltpu.VMEM((HG_HEADS, HG_DK, HG_DK), F32)],
        compiler_params=_cparams(2, V7X_VMEM_LIMIT_BYTES),
        name="hgrn2",
    )(q, k, lf, v, g, gain, cmat, *flat)
    if riding:
        casted = [o.reshape(a.shape) for o, a in zip(outs[1:], cast_f32)]
    else:
        casted = [a.astype(BF16) for a in cast_f32]
    return outs[0], casted


def _sb_kernel(q_ref, k0_ref, k1_ref, k2_ref, v0_ref, v1_ref, v2_ref, gain_ref, kall_ref, vall_ref,
               o_ref, acc_ref, out_ref, kbuf_ref, vbuf_ref, sem_ref):
    tb = SB_BLOCK
    bi = pl.program_id(0)
    qi = pl.program_id(1)
    n_pairs = SB_HEADS // 2
    pair_w = 2 * SB_DH

    acc_ref[...] = jnp.zeros_like(acc_ref)
    out_ref[...] = jnp.zeros_like(out_ref)

    t_io = lax.broadcasted_iota(I32, (SB_HEADS * tb, tb), 0) & (tb - 1)
    s_io = lax.broadcasted_iota(I32, (SB_HEADS * tb, tb), 1)
    causal = s_io < t_io
    u_row = lax.broadcasted_iota(I32, (tb, 2 * tb), 0)
    u_col = lax.broadcasted_iota(I32, (tb, 2 * tb), 1)
    um = jnp.where(jnp.logical_or(u_col >= tb, u_row > u_col), 1.0, 0.0).astype(BF16)
    lane = lax.broadcasted_iota(I32, (tb, pair_w), 1)
    lo_half = lane < SB_DH
    keeps = (lo_half, jnp.logical_not(lo_half))

    def process(blocks):
        pre = []
        for load_k, _, diag, wgt in blocks:
            zs = []
            for p in range(n_pairs):
                sl = slice(p * pair_w, (p + 1) * pair_w)
                q2 = q_ref[0, :, sl]
                k2 = load_k(sl)
                zero = jnp.zeros_like(q2)
                for half in range(2):
                    zs.append(_dot_nt(jnp.where(keeps[half], q2, zero), k2))
            z = jnp.concatenate(zs, axis=0)
            sp_full = jnp.maximum(z, 0.0) + jnp.log(1.0 + jnp.exp(-jnp.abs(z)))
            sp = jnp.where(causal, sp_full, 0.0) if diag else sp_full
            if wgt is not None:
                sp = sp * wgt
            lt = _dot(sp.astype(BF16), um)
            pre.append(((z - sp_full) - lt[:, :tb], lt[:, tb:]))
        acc = acc_ref[...]
        for (_, load_v, diag, wgt), (base, total) in zip(blocks, pre):
            a = jnp.exp(base - acc)
            if diag:
                a = jnp.where(causal, a, 0.0)
            if wgt is not None:
                a = a * wgt
            a = a.astype(BF16)
            acc = acc + total
            for p in range(n_pairs):
                sl = slice(p * pair_w, (p + 1) * pair_w)
                v2 = load_v(sl)
                zero = jnp.zeros_like(v2)
                o_pair = jnp.zeros((tb, pair_w), F32)
                for half in range(2):
                    h = 2 * p + half
                    o_pair = o_pair + _dot(a[h * tb:(h + 1) * tb], jnp.where(keeps[half], v2, zero))
                out_ref[:, sl] += o_pair
        acc_ref[...] = acc
        return jnp.min(acc)

    def blocked(ref):
        return lambda sl: ref[0, :, sl]

    def whole(ref):
        return lambda sl: ref[:, sl]

    on1 = jnp.where(qi >= 1, 1.0, 0.0)
    on2 = jnp.where(qi >= 2, 1.0, 0.0)
    m2 = process([
        (blocked(k0_ref), blocked(v0_ref), True, None),
        (blocked(k1_ref), blocked(v1_ref), False, on1),
        (blocked(k2_ref), blocked(v2_ref), False, on2),
    ])

    def cond(cr):
        j, m = cr
        return jnp.logical_and(j >= 0, m <= SB_SKIP_THRESHOLD)

    def body(cr):
        j, _ = cr
        r0 = pl.multiple_of(j * tb, tb)
        ck = pltpu.make_async_copy(kall_ref.at[bi, pl.ds(r0, tb), :], kbuf_ref, sem_ref.at[0])
        cv = pltpu.make_async_copy(vall_ref.at[bi, pl.ds(r0, tb), :], vbuf_ref, sem_ref.at[1])
        ck.start()
        cv.start()
        ck.wait()
        cv.wait()
        return j - 1, process([(whole(kbuf_ref), whole(vbuf_ref), False, None)])

    lax.while_loop(cond, body, (qi - SB_STATIC_BLOCKS, m2))

    o = out_ref[...]
    wd = SB_WIDTH
    dh_bits = SB_DH.bit_length() - 1
    bd = ((lax.broadcasted_iota(I32, (wd, wd), 0) >> dh_bits)
          == (lax.broadcasted_iota(I32, (wd, wd), 1) >> dh_bits)).astype(BF16)
    sq_h, sq_l = _split2(o * o)
    var = (_dot(sq_h, bd) + _dot(sq_l, bd)) * (1.0 / SB_DH)
    o_ref[0] = (o * lax.rsqrt(var + NORM_EPS) * gain_ref[...]).astype(o_ref.dtype)


def _stickbreak(sq, sk, sv, gain):
    b, s, w = sq.shape
    tb = SB_BLOCK
    cur = lambda bi, qi: (bi, qi, 0)
    prev1 = lambda bi, qi: (bi, jnp.maximum(qi - 1, 0), 0)
    prev2 = lambda bi, qi: (bi, jnp.maximum(qi - 2, 0), 0)
    fix = lambda bi, qi: (0, 0)
    blk = (1, tb, w)
    return pl.pallas_call(
        _sb_kernel,
        grid=(b, s // tb),
        in_specs=[
            pl.BlockSpec(blk, cur),
            pl.BlockSpec(blk, cur), pl.BlockSpec(blk, prev1), pl.BlockSpec(blk, prev2),
            pl.BlockSpec(blk, cur), pl.BlockSpec(blk, prev1), pl.BlockSpec(blk, prev2),
            pl.BlockSpec((1, w), fix),
            pl.BlockSpec(memory_space=pl.ANY),
            pl.BlockSpec(memory_space=pl.ANY),
        ],
        out_specs=pl.BlockSpec(blk, cur),
        out_shape=jax.ShapeDtypeStruct((b, s, w), BF16),
        scratch_shapes=[
            pltpu.VMEM((SB_HEADS * tb, tb), F32),
            pltpu.VMEM((tb, w), F32),
            pltpu.VMEM((tb, w), BF16),
            pltpu.VMEM((tb, w), BF16),
            pltpu.SemaphoreType.DMA((2,)),
        ],
        compiler_params=_cparams(2, V7X_VMEM_LIMIT_BYTES),
        name="stickbreak",
    )(sq, sk, sk, sk, sv, sv, sv, gain, sk, sv)


def _mix_router_kernel(ohg_ref, osb_ref, x_ref, wout_ref, gffn_ref, wrt_ref, br_ref,
                       h_ref, xs_ref, gate_ref, pos_ref, cnt_ref, *, tm):
    h = (x_ref[...]
         + _dot(ohg_ref[...], wout_ref[0:HG_WIDTH, :])
         + _dot(osb_ref[...], wout_ref[HG_WIDTH:HG_WIDTH + SB_WIDTH, :]))
    h_ref[...] = h
    var = jnp.mean(h * h, axis=-1, keepdims=True)
    u = h * lax.rsqrt(var + NORM_EPS) * gffn_ref[...]

    u_h, u_l = _split2(u)
    w_h, w_l = _split2(wrt_ref[...])
    logits = _dot_nt(w_h, u_h) + _dot_nt(w_h, u_l) + _dot_nt(w_l, u_h) + br_ref[...]

    e_io = lax.broadcasted_iota(I32, (N_EXPERTS, tm), 0).astype(F32)
    vals = logits
    member = jnp.zeros((N_EXPERTS, tm), F32)
    top_v, top_i = [], []
    for _ in range(TOP_K):
        m = jnp.max(vals, axis=0, keepdims=True)
        idx = jnp.min(jnp.where(vals == m, e_io, float(N_EXPERTS)), axis=0, keepdims=True)
        sel = e_io == idx
        top_v.append(m)
        top_i.append(idx)
        member = member + jnp.where(sel, 1.0, 0.0)
        vals = jnp.where(sel, -jnp.inf, vals)

    ex = [jnp.exp(tv - top_v[0]) for tv in top_v]
    den = ex[0] + ex[1] + ex[2] + ex[3]
    gates = [e / den for e in ex]

    n_io = lax.broadcasted_iota(I32, (tm, tm), 0)
    m_io = lax.broadcasted_iota(I32, (tm, tm), 1)
    before = jnp.where(n_io < m_io, 1.0, 0.0).astype(BF16)
    cexcl = _dot(member.astype(BF16), before)
    cnt = jnp.sum(member, axis=1, keepdims=True)
    run_len = jnp.floor((cnt + (RUN_ALIGN - 1)) * (1.0 / RUN_ALIGN)) * RUN_ALIGN
    cnt_ref[0] = jnp.broadcast_to(run_len, cnt_ref.shape[1:]).astype(I32)

    ee_r = lax.broadcasted_iota(I32, (N_EXPERTS, N_EXPERTS), 0)
    ee_c = lax.broadcasted_iota(I32, (N_EXPERTS, N_EXPERTS), 1)
    lower = jnp.where(ee_c < ee_r, 1.0, 0.0).astype(BF16)
    c_h, c_l = _split2(jnp.broadcast_to(run_len, (N_EXPERTS, LANES)))
    run_start = (_dot(lower, c_h) + _dot(lower, c_l))[:, 0:1]
    where_in_tile = cexcl + run_start
    pos = [jnp.sum(jnp.where(e_io == ti, where_in_tile, 0.0), axis=0, keepdims=True) for ti in top_i]

    j_io = lax.broadcasted_iota(I32, (_sorted_rows(tm), tm), 0).astype(F32)
    hit = j_io == pos[0]
    for pk in pos[1:]:
        hit = jnp.logical_or(hit, j_io == pk)
    onehot = jnp.where(hit, 1.0, 0.0).astype(BF16)
    xs_ref[...] = _dot(onehot, u.astype(BF16))

    r_io = lax.broadcasted_iota(I32, (LANES, tm), 0)
    gfull = jnp.zeros((LANES, tm), F32)
    pfull = jnp.zeros((LANES, tm), F32)
    for kk_ in range(TOP_K):
        gfull = jnp.where(r_io == kk_, jnp.broadcast_to(gates[kk_], (LANES, tm)), gfull)
        pfull = jnp.where(r_io == kk_, jnp.broadcast_to(pos[kk_], (LANES, tm)), pfull)
    gate_ref[...] = gfull.T
    pos_ref[...] = pfull.T


def _mix_router(ohg, osb, x2, w_out_bf, g_ffn, w_router_t, b_router_col, tm=512):
    n, d = x2.shape
    tm = min(tm, n)
    row = lambda i: (i, 0)
    fix = lambda i: (0, 0)
    return pl.pallas_call(
        functools.partial(_mix_router_kernel, tm=tm),
        grid=(n // tm,),
        in_specs=[
            pl.BlockSpec((tm, HG_WIDTH), row),
            pl.BlockSpec((tm, SB_WIDTH), row),
            pl.BlockSpec((tm, d), row),
            pl.BlockSpec(w_out_bf.shape, fix),
            pl.BlockSpec((1, d), fix),
            pl.BlockSpec(w_router_t.shape, fix),
            pl.BlockSpec(b_router_col.shape, fix),
        ],
        out_specs=[
            pl.BlockSpec((tm, d), row),
            pl.BlockSpec((_sorted_rows(tm), d), row),
            pl.BlockSpec((tm, LANES), row),
            pl.BlockSpec((tm, LANES), row),
            pl.BlockSpec((1, N_EXPERTS, LANES), lambda i: (i, 0, 0)),
        ],
        out_shape=[
            jax.ShapeDtypeStruct((n, d), F32),
            jax.ShapeDtypeStruct((n // tm * _sorted_rows(tm), d), F32),
            jax.ShapeDtypeStruct((n, LANES), F32),
            jax.ShapeDtypeStruct((n, LANES), F32),
            jax.ShapeDtypeStruct((n // tm, N_EXPERTS, LANES), I32),
        ],
        compiler_params=_cparams(1, V7X_VMEM_LIMIT_BYTES),
        name="mix_router",
    )(ohg, osb, x2, w_out_bf, g_ffn, w_router_t, b_router_col)


def _run_sizes(limit):
    sizes = []
    s = RUN_ALIGN
    while s <= limit:
        sizes.append(s)
        s *= 2
    return sizes[::-1]


def _row_run(ref, row, size):
    return ref.at[pl.ds(pl.multiple_of(row, RUN_ALIGN), size)]


def _for_each_piece(m, limit, fn):
    for size in _run_sizes(limit):
        @pl.when((m & size) != 0)
        def _(size=size):
            fn(m & ~(2 * size - 1), size)


def _run_copies(mt_ref, t, tm, make_copy):
    for e in range(N_EXPERTS):
        _for_each_piece(mt_ref[t, e], tm, lambda done, size, e=e: make_copy(e, done, size).start())


MOE_PAIR = 2


def _moe_kernel(be_ref, sb_ref, ns_ref, tlo_ref, thi_ref, nv_ref, mt_ref, ot_ref, dt_ref,
                xs_ref, *refs, tile_rows):
    n_w = 6 * MOE_PAIR
    w_refs, y_ref = refs[:n_w], refs[n_w]
    bufs, sem_ref = refs[n_w + 1:n_w + 1 + MOE_PAIR], refs[n_w + 1 + MOE_PAIR]
    i = pl.program_id(0)
    n_used_steps = ns_ref[0]

    def gather(b, half):
        e = be_ref[b]
        first_slot = b * MOE_BLOCK

        def per_tile(t, carry):
            run0 = dt_ref[t, e]
            lo = jnp.maximum(run0, first_slot)
            hi = jnp.minimum(run0 + mt_ref[t, e], first_slot + MOE_BLOCK)
            src = t * tile_rows + ot_ref[t, e] + (lo - run0)
            dst = lo - first_slot
            _for_each_piece(jnp.maximum(hi - lo, 0), MOE_BLOCK, lambda done, size: pltpu.make_async_copy(
                _row_run(xs_ref, src + done, size), _row_run(bufs[half], dst + done, size),
                sem_ref.at[half]).start())
            return carry

        lax.fori_loop(tlo_ref[b], thi_ref[b] + 1, per_tile, 0)

    def wait(b, half):
        _for_each_piece(nv_ref[b], MOE_BLOCK, lambda done, size: pltpu.make_async_copy(
            _row_run(xs_ref, 0, size), _row_run(bufs[half], 0, size), sem_ref.at[half]).wait())

    @pl.when(i == 0)
    def _():
        for half in range(MOE_PAIR):
            bufs[half][...] = jnp.zeros_like(bufs[half])
            gather(half, half)

    @pl.when(i < n_used_steps)
    def _():
        xs_now = []
        for half in range(MOE_PAIR):
            wait(MOE_PAIR * i + half, half)
            xs_now.append(bufs[half][...].astype(BF16))

        @pl.when(i + 1 < n_used_steps)
        def _():
            for half in range(MOE_PAIR):
                gather(MOE_PAIR * (i + 1) + half, half)

        for half in range(MOE_PAIR):
            wg_ref, bg_ref, wu_ref, bu_ref, wd_ref, bd_ref = w_refs[6 * half:6 * half + 6]
            x = xs_now[half]
            hg = _dot(x, wg_ref[0]) + bg_ref[0]
            hu = _dot(x, wu_ref[0]) + bu_ref[0]
            hg = jnp.minimum(hg, SWIGLU_LIMIT)
            hu = jnp.clip(hu, -SWIGLU_LIMIT, SWIGLU_LIMIT)
            glu = hg * _sigmoid(SWIGLU_ALPHA * hg)
            act = ((hu + 1.0) * glu).astype(BF16)
            y_ref[half * MOE_BLOCK:(half + 1) * MOE_BLOCK, :] = _dot(act, wd_ref[0]) + bd_ref[0]

    @pl.when(i >= n_used_steps)
    def _():
        y_ref[...] = jnp.zeros_like(y_ref)


def _moe(sched, xs, n_slots, tile_rows, wg, bg, wu, bu, wd, bd):
    d, f = wg.shape[1], wg.shape[2]
    step_rows = MOE_PAIR * MOE_BLOCK
    n_steps = n_slots // step_rows
    w_specs, w_args = [], []
    for half in range(MOE_PAIR):
        wmap = lambda i, be, sb, *_, half=half: (be[MOE_PAIR * sb[i] + half], 0, 0)
        w_specs += [pl.BlockSpec((1, d, f), wmap), pl.BlockSpec((1, 1, f), wmap),
                    pl.BlockSpec((1, d, f), wmap), pl.BlockSpec((1, 1, f), wmap),
                    pl.BlockSpec((1, f, d), wmap), pl.BlockSpec((1, 1, d), wmap)]
        w_args += [wg, bg, wu, bu, wd, bd]
    grid_spec = pltpu.PrefetchScalarGridSpec(
        num_scalar_prefetch=len(sched),
        grid=(n_steps,),
        in_specs=[pl.BlockSpec(memory_space=pl.ANY)] + w_specs,
        out_specs=pl.BlockSpec((step_rows, d), lambda i, *_: (i, 0)),
        scratch_shapes=[pltpu.VMEM((MOE_BLOCK, d), F32)] * MOE_PAIR
        + [pltpu.SemaphoreType.DMA((MOE_PAIR,))],
    )
    return pl.pallas_call(
        functools.partial(_moe_kernel, tile_rows=tile_rows),
        grid_spec=grid_spec,
        out_shape=jax.ShapeDtypeStruct((n_slots, d), F32),
        compiler_params=_cparams(1, V7X_VMEM_LIMIT_BYTES),
        name="moe",
    )(*sched, xs, *w_args)


def _combine_kernel(mt_ref, ot_ref, dt_ref, tot_ref, y_ref, pos_ref, h_ref, gate_ref, gain_ref, o_ref,
                    ybuf_ref, sem_ref, *, tm):
    t = pl.program_id(0)
    n_tiles = pl.num_programs(0)
    srows = _sorted_rows(tm)
    slot = t % 2

    def fetch(tt, s):
        _run_copies(mt_ref, tt, tm, lambda e, done, size: pltpu.make_async_copy(
            _row_run(y_ref, dt_ref[tt, e] + done, size),
            _row_run(ybuf_ref, s * srows + ot_ref[tt, e] + done, size), sem_ref.at[s]))

    @pl.when(t == 0)
    def _():
        ybuf_ref[...] = jnp.zeros_like(ybuf_ref)
        fetch(0, 0)

    @pl.when(t + 1 < n_tiles)
    def _():
        fetch(t + 1, 1 - slot)

    _for_each_piece(tot_ref[t], srows, lambda done, size: pltpu.make_async_copy(
        _row_run(y_ref, 0, size), _row_run(ybuf_ref, 0, size), sem_ref.at[slot]).wait())

    gate = gate_ref[...]
    pos = pos_ref[...]
    j_io = lax.broadcasted_iota(I32, (tm, srows), 1).astype(F32)
    wmat = jnp.zeros((tm, srows), F32)
    for k in range(TOP_K):
        wmat = jnp.where(j_io == pos[:, k:k + 1], gate[:, k:k + 1], wmat)
    ys = ybuf_ref[pl.ds(pl.multiple_of(slot * srows, RUN_ALIGN), srows), :].astype(BF16)
    acc = h_ref[...] + _dot(wmat.astype(BF16), ys)
    var = jnp.mean(acc * acc, axis=-1, keepdims=True)
    o_ref[...] = (acc * lax.rsqrt(var + NORM_EPS) * gain_ref[...]).astype(o_ref.dtype)


def _combine(tabs, y_disp, pos_c, h, gates, gain, tm):
    n, d = h.shape
    row = lambda t, *_: (t, 0)
    fix = lambda t, *_: (0, 0)
    grid_spec = pltpu.PrefetchScalarGridSpec(
        num_scalar_prefetch=len(tabs),
        grid=(n // tm,),
        in_specs=[
            pl.BlockSpec(memory_space=pl.ANY),
            pl.BlockSpec((tm, LANES), row),
            pl.BlockSpec((tm, d), row),
            pl.BlockSpec((tm, LANES), row),
            pl.BlockSpec((1, d), fix),
        ],
        out_specs=pl.BlockSpec((tm, d), row),
        scratch_shapes=[
            pltpu.VMEM((2 * _sorted_rows(tm), d), F32),
            pltpu.SemaphoreType.DMA((2,)),
        ],
    )
    return pl.pallas_call(
        functools.partial(_combine_kernel, tm=tm),
        grid_spec=grid_spec,
        out_shape=jax.ShapeDtypeStruct((n, d), F32),
        compiler_params=_cparams(1, V7X_VMEM_LIMIT_BYTES),
        name="combine",
    )(*tabs, y_disp, pos_c, h, gates, gain)


def _routing_tables(tile_cnt, n_pairs):
    counts = jnp.sum(tile_cnt, axis=0)
    padded = ((counts + MOE_BLOCK - 1) // MOE_BLOCK) * MOE_BLOCK
    cum_pad = jnp.cumsum(padded)
    start_pad = (cum_pad - padded).astype(I32)
    step_rows = MOE_PAIR * MOE_BLOCK
    n_slots = n_pairs + tile_cnt.shape[0] * N_EXPERTS * RUN_ALIGN + N_EXPERTS * MOE_BLOCK
    n_slots = ((n_slots + step_rows - 1) // step_rows) * step_rows
    nb = n_slots // MOE_BLOCK
    block_start = jnp.arange(nb, dtype=I32) * MOE_BLOCK
    block_e = jnp.minimum(jnp.sum(cum_pad[None, :] <= block_start[:, None], axis=1), N_EXPERTS - 1)
    block_e = block_e.astype(I32)
    n_used = (cum_pad[-1] // MOE_BLOCK).astype(I32)
    n_steps_used = (n_used + MOE_PAIR - 1) // MOE_PAIR
    step_blk = jnp.minimum(jnp.arange(nb // MOE_PAIR, dtype=I32), n_steps_used - 1)
    before_tile = jnp.cumsum(tile_cnt, axis=0) - tile_cnt
    run_dst = (start_pad[None, :] + before_tile).astype(I32)
    run_src = (jnp.cumsum(tile_cnt, axis=1) - tile_cnt).astype(I32)
    dst_b = run_dst[:, block_e]
    end_b = dst_b + tile_cnt[:, block_e]
    tile_lo = jnp.sum(end_b <= block_start[None, :], axis=0).astype(I32)
    tile_hi = (jnp.sum(dst_b < block_start[None, :] + MOE_BLOCK, axis=0) - 1).astype(I32)
    used = jnp.arange(nb, dtype=I32) < n_used
    n_valid = jnp.clip((start_pad + counts)[block_e] - block_start, 0, MOE_BLOCK)
    n_valid = jnp.where(used, n_valid, 0).astype(I32)
    tile_hi = jnp.where(used, tile_hi, tile_lo - 1)
    run_tabs = (tile_cnt.astype(I32), run_src, run_dst)
    sched = (block_e, step_blk, n_steps_used.reshape(1).astype(I32), tile_lo, tile_hi, n_valid) + run_tabs
    tile_total = jnp.sum(tile_cnt, axis=1).astype(I32)
    return run_tabs + (tile_total,), sched, n_slots


def kernel(x, w_in, w_out, hg_lb_logits, hg_norm_gain, sb_norm_gain, norm_mix_gain, norm_ffn_gain,
           w_router, b_router, w_gate, b_gate, w_up, b_up, w_down, b_down, norm_final_gain):
    b, s, d = x.shape
    n = b * s
    f = w_gate.shape[-1]
    assert w_in.shape[0] == 1 and hg_lb_logits.shape[0] == 2, "single-layer trunk only"
    x2 = x.reshape(n, d).astype(F32)
    r3 = lambda a: a.reshape(b, s, a.shape[-1])

    q, k, lf, v, g, sq, sk, sv = _in_proj(
        x2, norm_mix_gain[0].reshape(1, d), w_in[0].astype(BF16), hg_lb_logits.astype(F32))
    o_hg, (wg_bf, wu_bf, wd_bf) = _hgrn2(
        r3(q), r3(k), r3(lf), r3(v), r3(g), hg_norm_gain[0].reshape(1, HG_WIDTH),
        cast_f32=(w_gate[0], w_up[0], w_down[0]))
    o_sb = _stickbreak(r3(sq), r3(sk), r3(sv), sb_norm_gain[0].reshape(1, SB_WIDTH))
    h_mid, xs, gates, pos_c, cnt = _mix_router(
        o_hg.reshape(n, HG_WIDTH), o_sb.reshape(n, SB_WIDTH), x2, w_out[0].astype(BF16),
        norm_ffn_gain[0].reshape(1, d), w_router[0].T.astype(F32),
        b_router[0].reshape(N_EXPERTS, 1).astype(F32))
    tile_cnt = cnt[:, :, 0]
    run_tabs, sched, n_slots = _routing_tables(tile_cnt, n * TOP_K)
    tm = n // tile_cnt.shape[0]
    y_disp = _moe(sched, xs, n_slots, _sorted_rows(tm),
                  wg_bf, b_gate[0].reshape(N_EXPERTS, 1, f),
                  wu_bf, b_up[0].reshape(N_EXPERTS, 1, f),
                  wd_bf, b_down[0].reshape(N_EXPERTS, 1, d))
    out = _combine(run_tabs, y_disp, pos_c, h_mid, gates, norm_final_gain.reshape(1, d), tm)
    return out.reshape(b, s, d).astype(x.dtype)
```

```python
import functools

import jax
import jax.numpy as jnp
from jax import lax
from jax.experimental import pallas as pl
from jax.experimental.pallas import tpu as pltpu

F32 = jnp.float32
BF16 = jnp.bfloat16
I32 = jnp.int32

NORM_EPS = 1e-5
HG_HEADS = 4
HG_DK = 128
HG_WIDTH = HG_HEADS * HG_DK
HG_CHUNK = 64
HG_LEVELS = (32, 16, 8)
HG_DIAG = 8
HG_FACTORED_MAX_EXPONENT = 60.0
SB_HEADS = 8
SB_DH = 64
SB_WIDTH = SB_HEADS * SB_DH
SB_BLOCK = 128
SB_STATIC_BLOCKS = 3
SB_SKIP_THRESHOLD = 104.0
N_EXPERTS = 32
TOP_K = 4
MOE_BLOCK = 256
SWIGLU_LIMIT = 7.0
SWIGLU_ALPHA = 1.702
NEG_BIG = -1e30

V7X_VMEM_LIMIT_BYTES = 56 * 1024 * 1024
LANES = 128
RUN_ALIGN = 8


def _sorted_rows(tm):
    return TOP_K * tm + N_EXPERTS * RUN_ALIGN


def _cparams(n_axes, vmem_bytes=None):
    return pltpu.CompilerParams(
        dimension_semantics=("arbitrary",) * n_axes,
        vmem_limit_bytes=vmem_bytes,
    )


def _sigmoid(x):
    return 1.0 / (1.0 + jnp.exp(-x))


def _split2(x):
    hi = x.astype(BF16)
    lo = (x - hi.astype(F32)).astype(BF16)
    return hi, lo


def _dot(a, b):
    return jnp.dot(a, b, preferred_element_type=F32)


def _dot_nt(a, b):
    return lax.dot_general(a, b, (((1,), (1,)), ((), ())), preferred_element_type=F32)


def _dot_tn(a, b):
    return lax.dot_general(a, b, (((0,), (0,)), ((), ())), preferred_element_type=F32)


def _in_proj_kernel(x_ref, gain_ref, w_ref, lbl_ref,
                    q_ref, k_ref, lf_ref, v_ref, g_ref, sq_ref, sk_ref, sv_ref):
    x = x_ref[...]
    var = jnp.mean(x * x, axis=-1, keepdims=True)
    u = (x * lax.rsqrt(var + NORM_EPS) * gain_ref[...]).astype(BF16)

    lbl = lbl_ref[...]
    mx = jnp.max(lbl, axis=0, keepdims=True)
    ex = jnp.exp(lbl - mx)
    lb = ex[0:1, :] / jnp.sum(ex, axis=0, keepdims=True)

    def seg(i):
        return _dot(u, w_ref[:, i * HG_WIDTH:(i + 1) * HG_WIDTH])

    hq = seg(0)
    q_ref[...] = hq * _sigmoid(hq)
    f_sig = _sigmoid(seg(1))
    lf_ref[...] = jnp.log(lb + (1.0 - lb) * f_sig)
    k_ref[...] = (1.0 - lb) * (1.0 - f_sig)
    v_ref[...] = seg(2)
    hg = seg(3)
    g_ref[...] = hg * _sigmoid(hg)
    sq_ref[...] = (seg(4) * (SB_DH ** -0.5)).astype(BF16)
    sk_ref[...] = seg(5).astype(BF16)
    sv_ref[...] = seg(6).astype(BF16)


def _in_proj(x2, gain, w_in_bf, lb_logits, tm=512):
    n, d = x2.shape
    cols = w_in_bf.shape[1]
    row = lambda i: (i, 0)
    fix = lambda i: (0, 0)
    o_f32 = jax.ShapeDtypeStruct((n, HG_WIDTH), F32)
    o_bf = jax.ShapeDtypeStruct((n, SB_WIDTH), BF16)
    return pl.pallas_call(
        _in_proj_kernel,
        grid=(n // tm,),
        in_specs=[
            pl.BlockSpec((tm, d), row),
            pl.BlockSpec((1, d), fix),
            pl.BlockSpec((d, cols), fix),
            pl.BlockSpec(lb_logits.shape, fix),
        ],
        out_specs=[pl.BlockSpec((tm, HG_WIDTH), row)] * 8,
        out_shape=[o_f32] * 5 + [o_bf] * 3,
        compiler_params=_cparams(1, V7X_VMEM_LIMIT_BYTES),
        name="in_proj",
    )(x2, gain, w_in_bf, lb_logits)


def _hgrn2_consts():
    c = HG_CHUNK
    t = jnp.arange(c)[:, None]
    s = jnp.arange(c)[None, :]
    mats = [(s <= t)]
    for lv in HG_LEVELS:
        ref = (t // (2 * lv)) * (2 * lv) + lv - 1
        mats.append(s <= ref)
    mats.append(s <= (t // HG_DIAG) * HG_DIAG)
    return jnp.concatenate(mats, axis=0).astype(BF16)


def _hgrn2_kernel(q_ref, k_ref, lf_ref, v_ref, g_ref, gain_ref, cmat_ref, *refs, ts, n_cast):
    c = HG_CHUNK
    w = HG_WIDTH
    cast_in, o_ref = refs[:n_cast], refs[n_cast]
    cast_out, st_ref = refs[n_cast + 1:2 * n_cast + 1], refs[2 * n_cast + 1]
    for src, dst in zip(cast_in, cast_out):
        dst[...] = src[...].astype(dst.dtype)

    @pl.when(pl.program_id(1) == 0)
    def _():
        st_ref[...] = jnp.zeros_like(st_ref)

    row_w = lax.broadcasted_iota(I32, (c, w), 0)
    row_c = lax.broadcasted_iota(I32, (c, c), 0)
    col_c = lax.broadcasted_iota(I32, (c, c), 1)
    row_d = lax.broadcasted_iota(I32, (HG_DIAG, w), 0)
    dk_bits = HG_DK.bit_length() - 1
    bd = ((lax.broadcasted_iota(I32, (w, w), 0) >> dk_bits)
          == (lax.broadcasted_iota(I32, (w, w), 1) >> dk_bits)).astype(BF16)
    cmat = cmat_ref[...]
    gain = gain_ref[...]

    diag_mask = jnp.logical_and((row_c >> 3) == (col_c >> 3), col_c <= row_c)

    def chunk(ci, carry, factored_diag):
        r0 = pl.multiple_of(ci * c, c)
        q = q_ref[0, pl.ds(r0, c), :]
        kk = k_ref[0, pl.ds(r0, c), :]
        lf = lf_ref[0, pl.ds(r0, c), :]
        v = v_ref[0, pl.ds(r0, c), :]
        g = g_ref[0, pl.ds(r0, c), :]

        lf_h, lf_l = _split2(lf)
        gg = _dot(cmat, lf_h) + _dot(cmat, lf_l)
        G = gg[0:c]

        scores = [jnp.zeros((c, c), F32) for _ in range(HG_HEADS)]
        for li, lv in enumerate(HG_LEVELS):
            gref = gg[(li + 1) * c:(li + 2) * c]
            is_q = (row_w & (2 * lv - 1)) >= lv
            e = jnp.exp(jnp.where(is_q, G - gref, gref - G))
            ql = jnp.where(is_q, q * e, 0.0).astype(BF16)
            kl = jnp.where(is_q, 0.0, kk * e).astype(BF16)
            grp_bits = (2 * lv).bit_length() - 1
            same = (row_c >> grp_bits) == (col_c >> grp_bits)
            for h in range(HG_HEADS):
                sl = slice(h * HG_DK, (h + 1) * HG_DK)
                scores[h] = scores[h] + jnp.where(same, _dot_nt(ql[:, sl], kl[:, sl]), 0.0)

        if factored_diag:
            gref = gg[(len(HG_LEVELS) + 1) * c:(len(HG_LEVELS) + 2) * c]
            qd = (q * jnp.exp(G - gref)).astype(BF16)
            kd = (kk * jnp.exp(gref - G)).astype(BF16)
            for h in range(HG_HEADS):
                sl = slice(h * HG_DK, (h + 1) * HG_DK)
                scores[h] = scores[h] + jnp.where(diag_mask, _dot_nt(qd[:, sl], kd[:, sl]), 0.0)
            o = jnp.zeros((c, w), F32)
        else:
            tiles = []
            for b in range(c // HG_DIAG):
                rs = slice(b * HG_DIAG, (b + 1) * HG_DIAG)
                gb, qb, kb = G[rs], q[rs], kk[rs]
                for s in range(HG_DIAG):
                    gs = jnp.broadcast_to(gb[s:s + 1, :], (HG_DIAG, w))
                    ks = jnp.broadcast_to(kb[s:s + 1, :], (HG_DIAG, w))
                    e = jnp.exp(jnp.where(row_d >= s, gb - gs, NEG_BIG))
                    tiles.append(qb * e * ks)
            p_all = jnp.concatenate(tiles, axis=0)
            r_all = _dot(p_all.astype(BF16), bd)
            o_blocks = []
            for b in range(c // HG_DIAG):
                vb = v[b * HG_DIAG:(b + 1) * HG_DIAG]
                ob = jnp.zeros((HG_DIAG, w), F32)
                for s in range(HG_DIAG):
                    i0 = (b * HG_DIAG + s) * HG_DIAG
                    vs = jnp.broadcast_to(vb[s:s + 1, :], (HG_DIAG, w))
                    ob = ob + r_all[i0:i0 + HG_DIAG] * vs
                o_blocks.append(ob)
            o = jnp.concatenate(o_blocks, axis=0)

        qg = (q * jnp.exp(G)).astype(BF16)
        g_last = jnp.broadcast_to(G[c - 1:c, :], (c, w))
        kh = (kk * jnp.exp(g_last - G)).astype(BF16)
        dec = jnp.exp(G[c - 1:c, :])
        v_bf = v.astype(BF16)
        outs = []
        for h in range(HG_HEADS):
            sl = slice(h * HG_DK, (h + 1) * HG_DK)
            st = st_ref[h]
            oh = (o[:, sl]
                  + _dot(scores[h].astype(BF16), v_bf[:, sl])
                  + _dot_nt(qg[:, sl], st.astype(BF16)))
            st_ref[h] = st * dec[:, sl] + _dot_tn(v_bf[:, sl], kh[:, sl])
            var = jnp.mean(oh * oh, axis=-1, keepdims=True)
            outs.append(oh * lax.rsqrt(var + NORM_EPS))
        on = jnp.concatenate(outs, axis=-1) * gain * g
        o_ref[0, pl.ds(r0, c), :] = on.astype(o_ref.dtype)
        return carry

    worst = jnp.max(-lf_ref[0]) * (HG_DIAG - 1)

    def run(factored_diag):
        lax.fori_loop(0, ts // c, functools.partial(chunk, factored_diag=factored_diag), 0, unroll=True)

    lax.cond(worst < HG_FACTORED_MAX_EXPONENT, lambda: run(True), lambda: run(False))


CAST_BLOCK_BYTES_MAX = 2 * 1024 * 1024


def _hgrn2(q, k, lf, v, g, gain, cast_f32=(), ts=256):
    b, s, w = q.shape
    n_s = s // ts
    blk = lambda bi, si: (bi, si, 0)
    fix = lambda bi, si: (0, 0)
    cmat = _hgrn2_consts()
    n_steps = b * n_s
    flat = [a.reshape(-1, a.shape[-1]) for a in cast_f32]
    rows = [a.shape[0] // n_steps for a in flat]
    riding = all(a.shape[0] % n_steps == 0 and r % 16 == 0 and r * a.shape[1] * 4 <= CAST_BLOCK_BYTES_MAX
                 for a, r in zip(flat, rows))
    if not riding:
        flat, rows = [], []
    cast_specs = [pl.BlockSpec((r, a.shape[1]), lambda bi, si: (bi * n_s + si, 0)) for a, r in zip(flat, rows)]
    outs = pl.pallas_call(
        functools.partial(_hgrn2_kernel, ts=ts, n_cast=len(flat)),
        grid=(b, n_s),
        in_specs=[pl.BlockSpec((1, ts, w), blk)] * 5 + [
            pl.BlockSpec((1, w), fix),
            pl.BlockSpec(cmat.shape, fix),
        ] + cast_specs,
        out_specs=[pl.BlockSpec((1, ts, w), blk)] + cast_specs,
        out_shape=[jax.ShapeDtypeStruct((b, s, w), BF16)]
        + [jax.ShapeDtypeStruct(a.shape, BF16) for a in flat],
        scratch_shapes=[pltpu.VMEM((HG_HEADS, HG_DK, HG_DK), F32)],
        compiler_params=_cparams(2, V7X_VMEM_LIMIT_BYTES),
        name="hgrn2",
    )(q, k, lf, v, g, gain, cmat, *flat)
    if riding:
        casted = [o.reshape(a.shape) for o, a in zip(outs[1:], cast_f32)]
    else:
        casted = [a.astype(BF16) for a in cast_f32]
    return outs[0], casted


def _sb_kernel(q_ref, k0_ref, k1_ref, k2_ref, v0_ref, v1_ref, v2_ref, gain_ref, kall_ref, vall_ref,
               o_ref, acc_ref, out_ref, kbuf_ref, vbuf_ref, sem_ref):
    tb = SB_BLOCK
    bi = pl.program_id(0)
    qi = pl.program_id(1)
    n_pairs = SB_HEADS // 2
    pair_w = 2 * SB_DH

    acc_ref[...] = jnp.zeros_like(acc_ref)
    out_ref[...] = jnp.zeros_like(out_ref)

    t_io = lax.broadcasted_iota(I32, (SB_HEADS * tb, tb), 0) & (tb - 1)
    s_io = lax.broadcasted_iota(I32, (SB_HEADS * tb, tb), 1)
    causal = s_io < t_io
    u_row = lax.broadcasted_iota(I32, (tb, 2 * tb), 0)
    u_col = lax.broadcasted_iota(I32, (tb, 2 * tb), 1)
    um = jnp.where(jnp.logical_or(u_col >= tb, u_row > u_col), 1.0, 0.0).astype(BF16)
    lane = lax.broadcasted_iota(I32, (tb, pair_w), 1)
    lo_half = lane < SB_DH
    keeps = (lo_half, jnp.logical_not(lo_half))

    def process(blocks):
        pre = []
        for load_k, _, diag, wgt in blocks:
            zs = []
            for p in range(n_pairs):
                sl = slice(p * pair_w, (p + 1) * pair_w)
                q2 = q_ref[0, :, sl]
                k2 = load_k(sl)
                zero = jnp.zeros_like(q2)
                for half in range(2):
                    zs.append(_dot_nt(jnp.where(keeps[half], q2, zero), k2))
            z = jnp.concatenate(zs, axis=0)
            sp_full = jnp.maximum(z, 0.0) + jnp.log(1.0 + jnp.exp(-jnp.abs(z)))
            sp = jnp.where(causal, sp_full, 0.0) if diag else sp_full
            if wgt is not None:
                sp = sp * wgt
            lt = _dot(sp.astype(BF16), um)
            pre.append(((z - sp_full) - lt[:, :tb], lt[:, tb:]))
        acc = acc_ref[...]
        for (_, load_v, diag, wgt), (base, total) in zip(blocks, pre):
            a = jnp.exp(base - acc)
            if diag:
                a = jnp.where(causal, a, 0.0)
            if wgt is not None:
                a = a * wgt
            a = a.astype(BF16)
            acc = acc + total
            for p in range(n_pairs):
                sl = slice(p * pair_w, (p + 1) * pair_w)
                v2 = load_v(sl)
                zero = jnp.zeros_like(v2)
                o_pair = jnp.zeros((tb, pair_w), F32)
                for half in range(2):
                    h = 2 * p + half
                    o_pair = o_pair + _dot(a[h * tb:(h + 1) * tb], jnp.where(keeps[half], v2, zero))
                out_ref[:, sl] += o_pair
        acc_ref[...] = acc
        return jnp.min(acc)

    def blocked(ref):
        return lambda sl: ref[0, :, sl]

    def whole(ref):
        return lambda sl: ref[:, sl]

    on1 = jnp.where(qi >= 1, 1.0, 0.0)
    on2 = jnp.where(qi >= 2, 1.0, 0.0)
    m2 = process([
        (blocked(k0_ref), blocked(v0_ref), True, None),
        (blocked(k1_ref), blocked(v1_ref), False, on1),
        (blocked(k2_ref), blocked(v2_ref), False, on2),
    ])

    def cond(cr):
        j, m = cr
        return jnp.logical_and(j >= 0, m <= SB_SKIP_THRESHOLD)

    def body(cr):
        j, _ = cr
        r0 = pl.multiple_of(j * tb, tb)
        ck = pltpu.make_async_copy(kall_ref.at[bi, pl.ds(r0, tb), :], kbuf_ref, sem_ref.at[0])
        cv = pltpu.make_async_copy(vall_ref.at[bi, pl.ds(r0, tb), :], vbuf_ref, sem_ref.at[1])
        ck.start()
        cv.start()
        ck.wait()
        cv.wait()
        return j - 1, process([(whole(kbuf_ref), whole(vbuf_ref), False, None)])

    lax.while_loop(cond, body, (qi - SB_STATIC_BLOCKS, m2))

    o = out_ref[...]
    wd = SB_WIDTH
    dh_bits = SB_DH.bit_length() - 1
    bd = ((lax.broadcasted_iota(I32, (wd, wd), 0) >> dh_bits)
          == (lax.broadcasted_iota(I32, (wd, wd), 1) >> dh_bits)).astype(BF16)
    sq_h, sq_l = _split2(o * o)
    var = (_dot(sq_h, bd) + _dot(sq_l, bd)) * (1.0 / SB_DH)
    o_ref[0] = (o * lax.rsqrt(var + NORM_EPS) * gain_ref[...]).astype(o_ref.dtype)


def _stickbreak(sq, sk, sv, gain):
    b, s, w = sq.shape
    tb = SB_BLOCK
    cur = lambda bi, qi: (bi, qi, 0)
    prev1 = lambda bi, qi: (bi, jnp.maximum(qi - 1, 0), 0)
    prev2 = lambda bi, qi: (bi, jnp.maximum(qi - 2, 0), 0)
    fix = lambda bi, qi: (0, 0)
    blk = (1, tb, w)
    return pl.pallas_call(
        _sb_kernel,
        grid=(b, s // tb),
        in_specs=[
            pl.BlockSpec(blk, cur),
            pl.BlockSpec(blk, cur), pl.BlockSpec(blk, prev1), pl.BlockSpec(blk, prev2),
            pl.BlockSpec(blk, cur), pl.BlockSpec(blk, prev1), pl.BlockSpec(blk, prev2),
            pl.BlockSpec((1, w), fix),
            pl.BlockSpec(memory_space=pl.ANY),
            pl.BlockSpec(memory_space=pl.ANY),
        ],
        out_specs=pl.BlockSpec(blk, cur),
        out_shape=jax.ShapeDtypeStruct((b, s, w), BF16),
        scratch_shapes=[
            pltpu.VMEM((SB_HEADS * tb, tb), F32),
            pltpu.VMEM((tb, w), F32),
            pltpu.VMEM((tb, w), BF16),
            pltpu.VMEM((tb, w), BF16),
            pltpu.SemaphoreType.DMA((2,)),
        ],
        compiler_params=_cparams(2, V7X_VMEM_LIMIT_BYTES),
        name="stickbreak",
    )(sq, sk, sk, sk, sv, sv, sv, gain, sk, sv)


def _mix_router_kernel(ohg_ref, osb_ref, x_ref, wout_ref, gffn_ref, wrt_ref, br_ref,
                       h_ref, xs_ref, gate_ref, pos_ref, cnt_ref, *, tm):
    h = (x_ref[...]
         + _dot(ohg_ref[...], wout_ref[0:HG_WIDTH, :])
         + _dot(osb_ref[...], wout_ref[HG_WIDTH:HG_WIDTH + SB_WIDTH, :]))
    h_ref[...] = h
    var = jnp.mean(h * h, axis=-1, keepdims=True)
    u = h * lax.rsqrt(var + NORM_EPS) * gffn_ref[...]

    u_h, u_l = _split2(u)
    w_h, w_l = _split2(wrt_ref[...])
    logits = _dot_nt(w_h, u_h) + _dot_nt(w_h, u_l) + _dot_nt(w_l, u_h) + br_ref[...]

    e_io = lax.broadcasted_iota(I32, (N_EXPERTS, tm), 0).astype(F32)
    vals = logits
    member = jnp.zeros((N_EXPERTS, tm), F32)
    top_v, top_i = [], []
    for _ in range(TOP_K):
        m = jnp.max(vals, axis=0, keepdims=True)
        idx = jnp.min(jnp.where(vals == m, e_io, float(N_EXPERTS)), axis=0, keepdims=True)
        sel = e_io == idx
        top_v.append(m)
        top_i.append(idx)
        member = member + jnp.where(sel, 1.0, 0.0)
        vals = jnp.where(sel, -jnp.inf, vals)

    ex = [jnp.exp(tv - top_v[0]) for tv in top_v]
    den = ex[0] + ex[1] + ex[2] + ex[3]
    gates = [e / den for e in ex]

    n_io = lax.broadcasted_iota(I32, (tm, tm), 0)
    m_io = lax.broadcasted_iota(I32, (tm, tm), 1)
    before = jnp.where(n_io < m_io, 1.0, 0.0).astype(BF16)
    cexcl = _dot(member.astype(BF16), before)
    cnt = jnp.sum(member, axis=1, keepdims=True)
    run_len = jnp.floor((cnt + (RUN_ALIGN - 1)) * (1.0 / RUN_ALIGN)) * RUN_ALIGN
    cnt_ref[0] = jnp.broadcast_to(run_len, cnt_ref.shape[1:]).astype(I32)

    ee_r = lax.broadcasted_iota(I32, (N_EXPERTS, N_EXPERTS), 0)
    ee_c = lax.broadcasted_iota(I32, (N_EXPERTS, N_EXPERTS), 1)
    lower = jnp.where(ee_c < ee_r, 1.0, 0.0).astype(BF16)
    c_h, c_l = _split2(jnp.broadcast_to(run_len, (N_EXPERTS, LANES)))
    run_start = (_dot(lower, c_h) + _dot(lower, c_l))[:, 0:1]
    where_in_tile = cexcl + run_start
    pos = [jnp.sum(jnp.where(e_io == ti, where_in_tile, 0.0), axis=0, keepdims=True) for ti in top_i]

    j_io = lax.broadcasted_iota(I32, (_sorted_rows(tm), tm), 0).astype(F32)
    hit = j_io == pos[0]
    for pk in pos[1:]:
        hit = jnp.logical_or(hit, j_io == pk)
    onehot = jnp.where(hit, 1.0, 0.0).astype(BF16)
    xs_ref[...] = _dot(onehot, u.astype(BF16))

    r_io = lax.broadcasted_iota(I32, (LANES, tm), 0)
    gfull = jnp.zeros((LANES, tm), F32)
    pfull = jnp.zeros((LANES, tm), F32)
    for kk_ in range(TOP_K):
        gfull = jnp.where(r_io == kk_, jnp.broadcast_to(gates[kk_], (LANES, tm)), gfull)
        pfull = jnp.where(r_io == kk_, jnp.broadcast_to(pos[kk_], (LANES, tm)), pfull)
    gate_ref[...] = gfull.T
    pos_ref[...] = pfull.T


def _mix_router(ohg, osb, x2, w_out_bf, g_ffn, w_router_t, b_router_col, tm=512):
    n, d = x2.shape
    tm = min(tm, n)
    row = lambda i: (i, 0)
    fix = lambda i: (0, 0)
    return pl.pallas_call(
        functools.partial(_mix_router_kernel, tm=tm),
        grid=(n // tm,),
        in_specs=[
            pl.BlockSpec((tm, HG_WIDTH), row),
            pl.BlockSpec((tm, SB_WIDTH), row),
            pl.BlockSpec((tm, d), row),
            pl.BlockSpec(w_out_bf.shape, fix),
            pl.BlockSpec((1, d), fix),
            pl.BlockSpec(w_router_t.shape, fix),
            pl.BlockSpec(b_router_col.shape, fix),
        ],
        out_specs=[
            pl.BlockSpec((tm, d), row),
            pl.BlockSpec((_sorted_rows(tm), d), row),
            pl.BlockSpec((tm, LANES), row),
            pl.BlockSpec((tm, LANES), row),
            pl.BlockSpec((1, N_EXPERTS, LANES), lambda i: (i, 0, 0)),
        ],
        out_shape=[
            jax.ShapeDtypeStruct((n, d), F32),
            jax.ShapeDtypeStruct((n // tm * _sorted_rows(tm), d), F32),
            jax.ShapeDtypeStruct((n, LANES), F32),
            jax.ShapeDtypeStruct((n, LANES), F32),
            jax.ShapeDtypeStruct((n // tm, N_EXPERTS, LANES), I32),
        ],
        compiler_params=_cparams(1, V7X_VMEM_LIMIT_BYTES),
        name="mix_router",
    )(ohg, osb, x2, w_out_bf, g_ffn, w_router_t, b_router_col)


def _run_sizes(limit):
    sizes = []
    s = RUN_ALIGN
    while s <= limit:
        sizes.append(s)
        s *= 2
    return sizes[::-1]


def _row_run(ref, row, size):
    return ref.at[pl.ds(pl.multiple_of(row, RUN_ALIGN), size)]


def _for_each_piece(m, limit, fn):
    for size in _run_sizes(limit):
        @pl.when((m & size) != 0)
        def _(size=size):
            fn(m & ~(2 * size - 1), size)


def _run_copies(mt_ref, t, tm, make_copy):
    for e in range(N_EXPERTS):
        _for_each_piece(mt_ref[t, e], tm, lambda done, size, e=e: make_copy(e, done, size).start())


MOE_PAIR = 2
MOE_INLINE_TILES = 6


def _moe_kernel(be_ref, sb_ref, ns_ref, tlo_ref, thi_ref, nv_ref, mt_ref, ot_ref, dt_ref,
                xs_ref, *refs, tile_rows):
    n_w = 6 * MOE_PAIR
    w_refs, y_ref = refs[:n_w], refs[n_w]
    bufs, sem_ref = refs[n_w + 1:n_w + 1 + MOE_PAIR], refs[n_w + 1 + MOE_PAIR]
    i = pl.program_id(0)
    n_used_steps = ns_ref[0]

    def gather(b_any, half, part, enabled=True):
        b = jnp.minimum(b_any, be_ref.shape[0] - 1)
        e = be_ref[b]
        first_slot = b * MOE_BLOCK

        n_tiles = mt_ref.shape[0]
        t_first = tlo_ref[b]
        t_last = jnp.where(enabled, thi_ref[b], t_first - 1)

        def per_tile(t_any, carry):
            t = jnp.minimum(t_any, n_tiles - 1)
            run0 = dt_ref[t, e]
            lo = jnp.maximum(run0, first_slot)
            hi = jnp.minimum(run0 + mt_ref[t, e], first_slot + MOE_BLOCK)
            src = t * tile_rows + ot_ref[t, e] + (lo - run0)
            dst = lo - first_slot
            length = jnp.where(t_any <= t_last, jnp.maximum(hi - lo, 0), 0)
            _for_each_piece(length, MOE_BLOCK, lambda done, size: pltpu.make_async_copy(
                _row_run(xs_ref, src + done, size), _row_run(bufs[half], dst + done, size),
                sem_ref.at[half]).start())
            return carry

        if part == "inline":
            for j in range(MOE_INLINE_TILES):
                per_tile(t_first + j, 0)
        else:
            lax.fori_loop(t_first + MOE_INLINE_TILES, t_last + 1, per_tile, 0)

    def wait(b, half):
        _for_each_piece(nv_ref[b], MOE_BLOCK, lambda done, size: pltpu.make_async_copy(
            _row_run(xs_ref, 0, size), _row_run(bufs[half], 0, size), sem_ref.at[half]).wait())

    @pl.when(i == 0)
    def _():
        for half in range(MOE_PAIR):
            bufs[half][...] = jnp.zeros_like(bufs[half])
            gather(half, half, "inline")
            gather(half, half, "rest")

    @pl.when(i < n_used_steps)
    def _():
        xs_now = []
        for half in range(MOE_PAIR):
            wait(MOE_PAIR * i + half, half)
            xs_now.append(bufs[half][...].astype(BF16))
        more = i + 1 < n_used_steps
        for half in range(MOE_PAIR):
            gather(MOE_PAIR * (i + 1) + half, half, "inline", enabled=more)

        for half in range(MOE_PAIR):
            wg_ref, bg_ref, wu_ref, bu_ref, wd_ref, bd_ref = w_refs[6 * half:6 * half + 6]
            x = xs_now[half]
            hg = _dot(x, wg_ref[0]) + bg_ref[0]
            hu = _dot(x, wu_ref[0]) + bu_ref[0]
            hg = jnp.minimum(hg, SWIGLU_LIMIT)
            hu = jnp.clip(hu, -SWIGLU_LIMIT, SWIGLU_LIMIT)
            glu = hg * _sigmoid(SWIGLU_ALPHA * hg)
            act = ((hu + 1.0) * glu).astype(BF16)
            y_ref[half * MOE_BLOCK:(half + 1) * MOE_BLOCK, :] = _dot(act, wd_ref[0]) + bd_ref[0]

        for half in range(MOE_PAIR):
            gather(MOE_PAIR * (i + 1) + half, half, "rest", enabled=more)

    @pl.when(i >= n_used_steps)
    def _():
        y_ref[...] = jnp.zeros_like(y_ref)


def _moe(sched, xs, n_slots, tile_rows, wg, bg, wu, bu, wd, bd):
    d, f = wg.shape[1], wg.shape[2]
    step_rows = MOE_PAIR * MOE_BLOCK
    n_steps = n_slots // step_rows
    w_specs, w_args = [], []
    for half in range(MOE_PAIR):
        wmap = lambda i, be, sb, *_, half=half: (be[MOE_PAIR * sb[i] + half], 0, 0)
        w_specs += [pl.BlockSpec((1, d, f), wmap), pl.BlockSpec((1, 1, f), wmap),
                    pl.BlockSpec((1, d, f), wmap), pl.BlockSpec((1, 1, f), wmap),
                    pl.BlockSpec((1, f, d), wmap), pl.BlockSpec((1, 1, d), wmap)]
        w_args += [wg, bg, wu, bu, wd, bd]
    grid_spec = pltpu.PrefetchScalarGridSpec(
        num_scalar_prefetch=len(sched),
        grid=(n_steps,),
        in_specs=[pl.BlockSpec(memory_space=pl.ANY)] + w_specs,
        out_specs=pl.BlockSpec((step_rows, d), lambda i, *_: (i, 0)),
        scratch_shapes=[pltpu.VMEM((MOE_BLOCK, d), F32)] * MOE_PAIR
        + [pltpu.SemaphoreType.DMA((MOE_PAIR,))],
    )
    return pl.pallas_call(
        functools.partial(_moe_kernel, tile_rows=tile_rows),
        grid_spec=grid_spec,
        out_shape=jax.ShapeDtypeStruct((n_slots, d), F32),
        compiler_params=_cparams(1, V7X_VMEM_LIMIT_BYTES),
        name="moe",
    )(*sched, xs, *w_args)


def _combine_kernel(mt_ref, ot_ref, dt_ref, tot_ref, y_ref, pos_ref, h_ref, gate_ref, gain_ref, o_ref,
                    ybuf_ref, sem_ref, *, tm):
    t = pl.program_id(0)
    n_tiles = pl.num_programs(0)
    srows = _sorted_rows(tm)
    slot = t % 2

    def fetch(tt, s):
        _run_copies(mt_ref, tt, tm, lambda e, done, size: pltpu.make_async_copy(
            _row_run(y_ref, dt_ref[tt, e] + done, size),
            _row_run(ybuf_ref, s * srows + ot_ref[tt, e] + done, size), sem_ref.at[s]))

    @pl.when(t == 0)
    def _():
        ybuf_ref[...] = jnp.zeros_like(ybuf_ref)
        fetch(0, 0)

    @pl.when(t + 1 < n_tiles)
    def _():
        fetch(t + 1, 1 - slot)

    _for_each_piece(tot_ref[t], srows, lambda done, size: pltpu.make_async_copy(
        _row_run(y_ref, 0, size), _row_run(ybuf_ref, 0, size), sem_ref.at[slot]).wait())

    gate = gate_ref[...]
    pos = pos_ref[...]
    j_io = lax.broadcasted_iota(I32, (tm, srows), 1).astype(F32)
    wmat = jnp.zeros((tm, srows), F32)
    for k in range(TOP_K):
        wmat = jnp.where(j_io == pos[:, k:k + 1], gate[:, k:k + 1], wmat)
    ys = ybuf_ref[pl.ds(pl.multiple_of(slot * srows, RUN_ALIGN), srows), :].astype(BF16)
    acc = h_ref[...] + _dot(wmat.astype(BF16), ys)
    var = jnp.mean(acc * acc, axis=-1, keepdims=True)
    o_ref[...] = (acc * lax.rsqrt(var + NORM_EPS) * gain_ref[...]).astype(o_ref.dtype)


def _combine(tabs, y_disp, pos_c, h, gates, gain, tm):
    n, d = h.shape
    row = lambda t, *_: (t, 0)
    fix = lambda t, *_: (0, 0)
    grid_spec = pltpu.PrefetchScalarGridSpec(
        num_scalar_prefetch=len(tabs),
        grid=(n // tm,),
        in_specs=[
            pl.BlockSpec(memory_space=pl.ANY),
            pl.BlockSpec((tm, LANES), row),
            pl.BlockSpec((tm, d), row),
            pl.BlockSpec((tm, LANES), row),
            pl.BlockSpec((1, d), fix),
        ],
        out_specs=pl.BlockSpec((tm, d), row),
        scratch_shapes=[
            pltpu.VMEM((2 * _sorted_rows(tm), d), F32),
            pltpu.SemaphoreType.DMA((2,)),
        ],
    )
    return pl.pallas_call(
        functools.partial(_combine_kernel, tm=tm),
        grid_spec=grid_spec,
        out_shape=jax.ShapeDtypeStruct((n, d), F32),
        compiler_params=_cparams(1, V7X_VMEM_LIMIT_BYTES),
        name="combine",
    )(*tabs, y_disp, pos_c, h, gates, gain)


def _routing_tables(tile_cnt, n_pairs):
    counts = jnp.sum(tile_cnt, axis=0)
    padded = ((counts + MOE_BLOCK - 1) // MOE_BLOCK) * MOE_BLOCK
    cum_pad = jnp.cumsum(padded)
    start_pad = (cum_pad - padded).astype(I32)
    step_rows = MOE_PAIR * MOE_BLOCK
    n_slots = n_pairs + tile_cnt.shape[0] * N_EXPERTS * RUN_ALIGN + N_EXPERTS * MOE_BLOCK
    n_slots = ((n_slots + step_rows - 1) // step_rows) * step_rows
    nb = n_slots // MOE_BLOCK
    block_start = jnp.arange(nb, dtype=I32) * MOE_BLOCK
    block_e = jnp.minimum(jnp.sum(cum_pad[None, :] <= block_start[:, None], axis=1), N_EXPERTS - 1)
    block_e = block_e.astype(I32)
    n_used = (cum_pad[-1] // MOE_BLOCK).astype(I32)
    n_steps_used = (n_used + MOE_PAIR - 1) // MOE_PAIR
    step_blk = jnp.minimum(jnp.arange(nb // MOE_PAIR, dtype=I32), n_steps_used - 1)
    before_tile = jnp.cumsum(tile_cnt, axis=0) - tile_cnt
    run_dst = (start_pad[None, :] + before_tile).astype(I32)
    run_src = (jnp.cumsum(tile_cnt, axis=1) - tile_cnt).astype(I32)
    dst_b = run_dst[:, block_e]
    end_b = dst_b + tile_cnt[:, block_e]
    tile_lo = jnp.sum(end_b <= block_start[None, :], axis=0).astype(I32)
    tile_hi = (jnp.sum(dst_b < block_start[None, :] + MOE_BLOCK, axis=0) - 1).astype(I32)
    used = jnp.arange(nb, dtype=I32) < n_used
    n_valid = jnp.clip((start_pad + counts)[block_e] - block_start, 0, MOE_BLOCK)
    n_valid = jnp.where(used, n_valid, 0).astype(I32)
    tile_hi = jnp.where(used, tile_hi, tile_lo - 1)
    run_tabs = (tile_cnt.astype(I32), run_src, run_dst)
    sched = (block_e, step_blk, n_steps_used.reshape(1).astype(I32), tile_lo, tile_hi, n_valid) + run_tabs
    tile_total = jnp.sum(tile_cnt, axis=1).astype(I32)
    return run_tabs + (tile_total,), sched, n_slots


def kernel(x, w_in, w_out, hg_lb_logits, hg_norm_gain, sb_norm_gain, norm_mix_gain, norm_ffn_gain,
           w_router, b_router, w_gate, b_gate, w_up, b_up, w_down, b_down, norm_final_gain):
    b, s, d = x.shape
    n = b * s
    f = w_gate.shape[-1]
    assert w_in.shape[0] == 1 and hg_lb_logits.shape[0] == 2, "single-layer trunk only"
    x2 = x.reshape(n, d).astype(F32)
    r3 = lambda a: a.reshape(b, s, a.shape[-1])

    q, k, lf, v, g, sq, sk, sv = _in_proj(
        x2, norm_mix_gain[0].reshape(1, d), w_in[0].astype(BF16), hg_lb_logits.astype(F32))
    o_hg, (wg_bf, wu_bf, wd_bf) = _hgrn2(
        r3(q), r3(k), r3(lf), r3(v), r3(g), hg_norm_gain[0].reshape(1, HG_WIDTH),
        cast_f32=(w_gate[0], w_up[0], w_down[0]))
    o_sb = _stickbreak(r3(sq), r3(sk), r3(sv), sb_norm_gain[0].reshape(1, SB_WIDTH))
    h_mid, xs, gates, pos_c, cnt = _mix_router(
        o_hg.reshape(n, HG_WIDTH), o_sb.reshape(n, SB_WIDTH), x2, w_out[0].astype(BF16),
        norm_ffn_gain[0].reshape(1, d), w_router[0].T.astype(F32),
        b_router[0].reshape(N_EXPERTS, 1).astype(F32))
    tile_cnt = cnt[:, :, 0]
    run_tabs, sched, n_slots = _routing_tables(tile_cnt, n * TOP_K)
    tm = n // tile_cnt.shape[0]
    y_disp = _moe(sched, xs, n_slots, _sorted_rows(tm),
                  wg_bf, b_gate[0].reshape(N_EXPERTS, 1, f),
                  wu_bf, b_up[0].reshape(N_EXPERTS, 1, f),
                  wd_bf, b_down[0].reshape(N_EXPERTS, 1, d))
    out = _combine(run_tabs, y_disp, pos_c, h_mid, gates, norm_final_gain.reshape(1, d), tm)
    return out.reshape(b, s, d).astype(x.dtype)
```

```python
import functools

import jax
import jax.numpy as jnp
from jax import lax
from jax.experimental import pallas as pl
from jax.experimental.pallas import tpu as pltpu

F32 = jnp.float32
BF16 = jnp.bfloat16
I32 = jnp.int32

NORM_EPS = 1e-5
HG_HEADS = 4
HG_DK = 128
HG_WIDTH = HG_HEADS * HG_DK
HG_CHUNK = 64
HG_LEVELS = (32, 16, 8)
HG_DIAG = 8
HG_FACTORED_MAX_EXPONENT = 60.0
SB_HEADS = 8
SB_DH = 64
SB_WIDTH = SB_HEADS * SB_DH
SB_BLOCK = 128
SB_STATIC_BLOCKS = 3
SB_SKIP_THRESHOLD = 104.0
N_EXPERTS = 32
TOP_K = 4
MOE_BLOCK = 256
SWIGLU_LIMIT = 7.0
SWIGLU_ALPHA = 1.702
NEG_BIG = -1e30

V7X_VMEM_LIMIT_BYTES = 56 * 1024 * 1024
LANES = 128
RUN_ALIGN = 8


def _sorted_rows(tm):
    return TOP_K * tm + N_EXPERTS * RUN_ALIGN


def _cparams(n_axes, vmem_bytes=None):
    return pltpu.CompilerParams(
        dimension_semantics=("arbitrary",) * n_axes,
        vmem_limit_bytes=vmem_bytes,
    )


def _sigmoid(x):
    return 1.0 / (1.0 + jnp.exp(-x))


def _split2(x):
    hi = x.astype(BF16)
    lo = (x - hi.astype(F32)).astype(BF16)
    return hi, lo


def _dot(a, b):
    return jnp.dot(a, b, preferred_element_type=F32)


def _dot_nt(a, b):
    return lax.dot_general(a, b, (((1,), (1,)), ((), ())), preferred_element_type=F32)


def _dot_tn(a, b):
    return lax.dot_general(a, b, (((0,), (0,)), ((), ())), preferred_element_type=F32)


def _in_proj_kernel(x_ref, gain_ref, w_ref, lbl_ref,
                    q_ref, k_ref, lf_ref, v_ref, g_ref, sq_ref, sk_ref, sv_ref):
    x = x_ref[...]
    var = jnp.mean(x * x, axis=-1, keepdims=True)
    u = (x * lax.rsqrt(var + NORM_EPS) * gain_ref[...]).astype(BF16)

    lbl = lbl_ref[...]
    mx = jnp.max(lbl, axis=0, keepdims=True)
    ex = jnp.exp(lbl - mx)
    lb = ex[0:1, :] / jnp.sum(ex, axis=0, keepdims=True)

    def seg(i):
        return _dot(u, w_ref[:, i * HG_WIDTH:(i + 1) * HG_WIDTH])

    hq = seg(0)
    q_ref[...] = hq * _sigmoid(hq)
    f_sig = _sigmoid(seg(1))
    lf_ref[...] = jnp.log(lb + (1.0 - lb) * f_sig)
    k_ref[...] = (1.0 - lb) * (1.0 - f_sig)
    v_ref[...] = seg(2)
    hg = seg(3)
    g_ref[...] = hg * _sigmoid(hg)
    sq_ref[...] = (seg(4) * (SB_DH ** -0.5)).astype(BF16)
    sk_ref[...] = seg(5).astype(BF16)
    sv_ref[...] = seg(6).astype(BF16)


def _in_proj(x2, gain, w_in_bf, lb_logits, tm=512):
    n, d = x2.shape
    cols = w_in_bf.shape[1]
    row = lambda i: (i, 0)
    fix = lambda i: (0, 0)
    o_f32 = jax.ShapeDtypeStruct((n, HG_WIDTH), F32)
    o_bf = jax.ShapeDtypeStruct((n, SB_WIDTH), BF16)
    return pl.pallas_call(
        _in_proj_kernel,
        grid=(n // tm,),
        in_specs=[
            pl.BlockSpec((tm, d), row),
            pl.BlockSpec((1, d), fix),
            pl.BlockSpec((d, cols), fix),
            pl.BlockSpec(lb_logits.shape, fix),
        ],
        out_specs=[pl.BlockSpec((tm, HG_WIDTH), row)] * 8,
        out_shape=[o_f32] * 5 + [o_bf] * 3,
        compiler_params=_cparams(1, V7X_VMEM_LIMIT_BYTES),
        name="in_proj",
    )(x2, gain, w_in_bf, lb_logits)


def _hgrn2_consts():
    c = HG_CHUNK
    t = jnp.arange(c)[:, None]
    s = jnp.arange(c)[None, :]
    mats = [(s <= t)]
    for lv in HG_LEVELS:
        ref = (t // (2 * lv)) * (2 * lv) + lv - 1
        mats.append(s <= ref)
    mats.append(s <= (t // HG_DIAG) * HG_DIAG)
    return jnp.concatenate(mats, axis=0).astype(BF16)


def _hgrn2_kernel(q_ref, k_ref, lf_ref, v_ref, g_ref, gain_ref, cmat_ref, *refs, ts, n_cast):
    c = HG_CHUNK
    w = HG_WIDTH
    cast_in, o_ref = refs[:n_cast], refs[n_cast]
    cast_out, st_ref = refs[n_cast + 1:2 * n_cast + 1], refs[2 * n_cast + 1]
    for src, dst in zip(cast_in, cast_out):
        dst[...] = src[...].astype(dst.dtype)

    @pl.when(pl.program_id(1) == 0)
    def _():
        st_ref[...] = jnp.zeros_like(st_ref)

    row_w = lax.broadcasted_iota(I32, (c, w), 0)
    row_c = lax.broadcasted_iota(I32, (c, c), 0)
    col_c = lax.broadcasted_iota(I32, (c, c), 1)
    row_d = lax.broadcasted_iota(I32, (HG_DIAG, w), 0)
    dk_bits = HG_DK.bit_length() - 1
    bd = ((lax.broadcasted_iota(I32, (w, w), 0) >> dk_bits)
          == (lax.broadcasted_iota(I32, (w, w), 1) >> dk_bits)).astype(BF16)
    cmat = cmat_ref[...]
    gain = gain_ref[...]

    diag_mask = jnp.logical_and((row_c >> 3) == (col_c >> 3), col_c <= row_c)

    def chunk(ci, carry, factored_diag):
        r0 = pl.multiple_of(ci * c, c)
        q = q_ref[0, pl.ds(r0, c), :]
        kk = k_ref[0, pl.ds(r0, c), :]
        lf = lf_ref[0, pl.ds(r0, c), :]
        v = v_ref[0, pl.ds(r0, c), :]
        g = g_ref[0, pl.ds(r0, c), :]

        lf_h, lf_l = _split2(lf)
        gg = _dot(cmat, lf_h) + _dot(cmat, lf_l)
        G = gg[0:c]

        scores = [jnp.zeros((c, c), F32) for _ in range(HG_HEADS)]
        for li, lv in enumerate(HG_LEVELS):
            gref = gg[(li + 1) * c:(li + 2) * c]
            is_q = (row_w & (2 * lv - 1)) >= lv
            e = jnp.exp(jnp.where(is_q, G - gref, gref - G))
            ql = jnp.where(is_q, q * e, 0.0).astype(BF16)
            kl = jnp.where(is_q, 0.0, kk * e).astype(BF16)
            grp_bits = (2 * lv).bit_length() - 1
            same = (row_c >> grp_bits) == (col_c >> grp_bits)
            for h in range(HG_HEADS):
                sl = slice(h * HG_DK, (h + 1) * HG_DK)
                scores[h] = scores[h] + jnp.where(same, _dot_nt(ql[:, sl], kl[:, sl]), 0.0)

        if factored_diag:
            gref = gg[(len(HG_LEVELS) + 1) * c:(len(HG_LEVELS) + 2) * c]
            qd = (q * jnp.exp(G - gref)).astype(BF16)
            kd = (kk * jnp.exp(gref - G)).astype(BF16)
            for h in range(HG_HEADS):
                sl = slice(h * HG_DK, (h + 1) * HG_DK)
                scores[h] = scores[h] + jnp.where(diag_mask, _dot_nt(qd[:, sl], kd[:, sl]), 0.0)
            o = jnp.zeros((c, w), F32)
        else:
            tiles = []
            for b in range(c // HG_DIAG):
                rs = slice(b * HG_DIAG, (b + 1) * HG_DIAG)
                gb, qb, kb = G[rs], q[rs], kk[rs]
                for s in range(HG_DIAG):
                    gs = jnp.broadcast_to(gb[s:s + 1, :], (HG_DIAG, w))
                    ks = jnp.broadcast_to(kb[s:s + 1, :], (HG_DIAG, w))
                    e = jnp.exp(jnp.where(row_d >= s, gb - gs, NEG_BIG))
                    tiles.append(qb * e * ks)
            p_all = jnp.concatenate(tiles, axis=0)
            r_all = _dot(p_all.astype(BF16), bd)
            o_blocks = []
            for b in range(c // HG_DIAG):
                vb = v[b * HG_DIAG:(b + 1) * HG_DIAG]
                ob = jnp.zeros((HG_DIAG, w), F32)
                for s in range(HG_DIAG):
                    i0 = (b * HG_DIAG + s) * HG_DIAG
                    vs = jnp.broadcast_to(vb[s:s + 1, :], (HG_DIAG, w))
                    ob = ob + r_all[i0:i0 + HG_DIAG] * vs
                o_blocks.append(ob)
            o = jnp.concatenate(o_blocks, axis=0)

        qg = (q * jnp.exp(G)).astype(BF16)
        g_last = jnp.broadcast_to(G[c - 1:c, :], (c, w))
        kh = (kk * jnp.exp(g_last - G)).astype(BF16)
        dec = jnp.exp(G[c - 1:c, :])
        v_bf = v.astype(BF16)
        outs = []
        for h in range(HG_HEADS):
            sl = slice(h * HG_DK, (h + 1) * HG_DK)
            st = st_ref[h]
            oh = (o[:, sl]
                  + _dot(scores[h].astype(BF16), v_bf[:, sl])
                  + _dot_nt(qg[:, sl], st.astype(BF16)))
            st_ref[h] = st * dec[:, sl] + _dot_tn(v_bf[:, sl], kh[:, sl])
            var = jnp.mean(oh * oh, axis=-1, keepdims=True)
            outs.append(oh * lax.rsqrt(var + NORM_EPS))
        on = jnp.concatenate(outs, axis=-1) * gain * g
        o_ref[0, pl.ds(r0, c), :] = on.astype(o_ref.dtype)
        return carry

    worst = jnp.max(-lf_ref[0]) * (HG_DIAG - 1)

    def run(factored_diag):
        lax.fori_loop(0, ts // c, functools.partial(chunk, factored_diag=factored_diag), 0, unroll=True)

    lax.cond(worst < HG_FACTORED_MAX_EXPONENT, lambda: run(True), lambda: run(False))


CAST_BLOCK_BYTES_MAX = 2 * 1024 * 1024


def _hgrn2(q, k, lf, v, g, gain, cast_f32=(), ts=256):
    b, s, w = q.shape
    n_s = s // ts
    blk = lambda bi, si: (bi, si, 0)
    fix = lambda bi, si: (0, 0)
    cmat = _hgrn2_consts()
    n_steps = b * n_s
    flat = [a.reshape(-1, a.shape[-1]) for a in cast_f32]
    rows = [a.shape[0] // n_steps for a in flat]
    riding = all(a.shape[0] % n_steps == 0 and r % 16 == 0 and r * a.shape[1] * 4 <= CAST_BLOCK_BYTES_MAX
                 for a, r in zip(flat, rows))
    if not riding:
        flat, rows = [], []
    cast_specs = [pl.BlockSpec((r, a.shape[1]), lambda bi, si: (bi * n_s + si, 0)) for a, r in zip(flat, rows)]
    outs = pl.pallas_call(
        functools.partial(_hgrn2_kernel, ts=ts, n_cast=len(flat)),
        grid=(b, n_s),
        in_specs=[pl.BlockSpec((1, ts, w), blk)] * 5 + [
            pl.BlockSpec((1, w), fix),
            pl.BlockSpec(cmat.shape, fix),
        ] + cast_specs,
        out_specs=[pl.BlockSpec((1, ts, w), blk)] + cast_specs,
        out_shape=[jax.ShapeDtypeStruct((b, s, w), BF16)]
        + [jax.ShapeDtypeStruct(a.shape, BF16) for a in flat],
        scratch_shapes=[pltpu.VMEM((HG_HEADS, HG_DK, HG_DK), F32)],
        compiler_params=_cparams(2, V7X_VMEM_LIMIT_BYTES),
        name="hgrn2",
    )(q, k, lf, v, g, gain, cmat, *flat)
    if riding:
        casted = [o.reshape(a.shape) for o, a in zip(outs[1:], cast_f32)]
    else:
        casted = [a.astype(BF16) for a in cast_f32]
    return outs[0], casted


def _sb_kernel(q_ref, k0_ref, k1_ref, k2_ref, v0_ref, v1_ref, v2_ref, gain_ref, kall_ref, vall_ref,
               o_ref, acc_ref, out_ref, kbuf_ref, vbuf_ref, sem_ref):
    tb = SB_BLOCK
    bi = pl.program_id(0)
    qi = pl.program_id(1)
    n_pairs = SB_HEADS // 2
    pair_w = 2 * SB_DH

    acc_ref[...] = jnp.zeros_like(acc_ref)
    out_ref[...] = jnp.zeros_like(out_ref)

    t_io = lax.broadcasted_iota(I32, (SB_HEADS * tb, tb), 0) & (tb - 1)
    s_io = lax.broadcasted_iota(I32, (SB_HEADS * tb, tb), 1)
    causal = s_io < t_io
    u_row = lax.broadcasted_iota(I32, (tb, 2 * tb), 0)
    u_col = lax.broadcasted_iota(I32, (tb, 2 * tb), 1)
    um = jnp.where(jnp.logical_or(u_col >= tb, u_row > u_col), 1.0, 0.0).astype(BF16)
    lane = lax.broadcasted_iota(I32, (tb, pair_w), 1)
    lo_half = lane < SB_DH
    keeps = (lo_half, jnp.logical_not(lo_half))

    def process(blocks):
        pre = []
        for load_k, _, diag, wgt in blocks:
            zs = []
            for p in range(n_pairs):
                sl = slice(p * pair_w, (p + 1) * pair_w)
                q2 = q_ref[0, :, sl]
                k2 = load_k(sl)
                zero = jnp.zeros_like(q2)
                for half in range(2):
                    zs.append(_dot_nt(jnp.where(keeps[half], q2, zero), k2))
            z = jnp.concatenate(zs, axis=0)
            sp_full = jnp.maximum(z, 0.0) + jnp.log(1.0 + jnp.exp(-jnp.abs(z)))
            sp = jnp.where(causal, sp_full, 0.0) if diag else sp_full
            if wgt is not None:
                sp = sp * wgt
            lt = _dot(sp.astype(BF16), um)
            pre.append(((z - sp_full) - lt[:, :tb], lt[:, tb:]))
        acc = acc_ref[...]
        for (_, load_v, diag, wgt), (base, total) in zip(blocks, pre):
            a = jnp.exp(base - acc)
            if diag:
                a = jnp.where(causal, a, 0.0)
            if wgt is not None:
                a = a * wgt
            a = a.astype(BF16)
            acc = acc + total
            for p in range(n_pairs):
                sl = slice(p * pair_w, (p + 1) * pair_w)
                v2 = load_v(sl)
                zero = jnp.zeros_like(v2)
                o_pair = jnp.zeros((tb, pair_w), F32)
                for half in range(2):
                    h = 2 * p + half
                    o_pair = o_pair + _dot(a[h * tb:(h + 1) * tb], jnp.where(keeps[half], v2, zero))
                out_ref[:, sl] += o_pair
        acc_ref[...] = acc
        return jnp.min(acc)

    def blocked(ref):
        return lambda sl: ref[0, :, sl]

    def whole(ref):
        return lambda sl: ref[:, sl]

    on1 = jnp.where(qi >= 1, 1.0, 0.0)
    on2 = jnp.where(qi >= 2, 1.0, 0.0)
    m2 = process([
        (blocked(k0_ref), blocked(v0_ref), True, None),
        (blocked(k1_ref), blocked(v1_ref), False, on1),
        (blocked(k2_ref), blocked(v2_ref), False, on2),
    ])

    def cond(cr):
        j, m = cr
        return jnp.logical_and(j >= 0, m <= SB_SKIP_THRESHOLD)

    def body(cr):
        j, _ = cr
        r0 = pl.multiple_of(j * tb, tb)
        ck = pltpu.make_async_copy(kall_ref.at[bi, pl.ds(r0, tb), :], kbuf_ref, sem_ref.at[0])
        cv = pltpu.make_async_copy(vall_ref.at[bi, pl.ds(r0, tb), :], vbuf_ref, sem_ref.at[1])
        ck.start()
        cv.start()
        ck.wait()
        cv.wait()
        return j - 1, process([(whole(kbuf_ref), whole(vbuf_ref), False, None)])

    lax.while_loop(cond, body, (qi - SB_STATIC_BLOCKS, m2))

    o = out_ref[...]
    wd = SB_WIDTH
    dh_bits = SB_DH.bit_length() - 1
    bd = ((lax.broadcasted_iota(I32, (wd, wd), 0) >> dh_bits)
          == (lax.broadcasted_iota(I32, (wd, wd), 1) >> dh_bits)).astype(BF16)
    sq_h, sq_l = _split2(o * o)
    var = (_dot(sq_h, bd) + _dot(sq_l, bd)) * (1.0 / SB_DH)
    o_ref[0] = (o * lax.rsqrt(var + NORM_EPS) * gain_ref[...]).astype(o_ref.dtype)


def _stickbreak(sq, sk, sv, gain):
    b, s, w = sq.shape
    tb = SB_BLOCK
    cur = lambda bi, qi: (bi, qi, 0)
    prev1 = lambda bi, qi: (bi, jnp.maximum(qi - 1, 0), 0)
    prev2 = lambda bi, qi: (bi, jnp.maximum(qi - 2, 0), 0)
    fix = lambda bi, qi: (0, 0)
    blk = (1, tb, w)
    return pl.pallas_call(
        _sb_kernel,
        grid=(b, s // tb),
        in_specs=[
            pl.BlockSpec(blk, cur),
            pl.BlockSpec(blk, cur), pl.BlockSpec(blk, prev1), pl.BlockSpec(blk, prev2),
            pl.BlockSpec(blk, cur), pl.BlockSpec(blk, prev1), pl.BlockSpec(blk, prev2),
            pl.BlockSpec((1, w), fix),
            pl.BlockSpec(memory_space=pl.ANY),
            pl.BlockSpec(memory_space=pl.ANY),
        ],
        out_specs=pl.BlockSpec(blk, cur),
        out_shape=jax.ShapeDtypeStruct((b, s, w), BF16),
        scratch_shapes=[
            pltpu.VMEM((SB_HEADS * tb, tb), F32),
            pltpu.VMEM((tb, w), F32),
            pltpu.VMEM((tb, w), BF16),
            pltpu.VMEM((tb, w), BF16),
            pltpu.SemaphoreType.DMA((2,)),
        ],
        compiler_params=_cparams(2, V7X_VMEM_LIMIT_BYTES),
        name="stickbreak",
    )(sq, sk, sk, sk, sv, sv, sv, gain, sk, sv)


def _mix_router_kernel(ohg_ref, osb_ref, x_ref, wout_ref, gffn_ref, wrt_ref, br_ref,
                       h_ref, xs_ref, gate_ref, pos_ref, cnt_ref, *, tm):
    h = (x_ref[...]
         + _dot(ohg_ref[...], wout_ref[0:HG_WIDTH, :])
         + _dot(osb_ref[...], wout_ref[HG_WIDTH:HG_WIDTH + SB_WIDTH, :]))
    h_ref[...] = h
    var = jnp.mean(h * h, axis=-1, keepdims=True)
    u = h * lax.rsqrt(var + NORM_EPS) * gffn_ref[...]

    u_h, u_l = _split2(u)
    w_h, w_l = _split2(wrt_ref[...])
    logits = _dot_nt(w_h, u_h) + _dot_nt(w_h, u_l) + _dot_nt(w_l, u_h) + br_ref[...]

    e_io = lax.broadcasted_iota(I32, (N_EXPERTS, tm), 0).astype(F32)
    vals = logits
    member = jnp.zeros((N_EXPERTS, tm), F32)
    top_v, top_i = [], []
    for _ in range(TOP_K):
        m = jnp.max(vals, axis=0, keepdims=True)
        idx = jnp.min(jnp.where(vals == m, e_io, float(N_EXPERTS)), axis=0, keepdims=True)
        sel = e_io == idx
        top_v.append(m)
        top_i.append(idx)
        member = member + jnp.where(sel, 1.0, 0.0)
        vals = jnp.where(sel, -jnp.inf, vals)

    ex = [jnp.exp(tv - top_v[0]) for tv in top_v]
    den = ex[0] + ex[1] + ex[2] + ex[3]
    gates = [e / den for e in ex]

    n_io = lax.broadcasted_iota(I32, (tm, tm), 0)
    m_io = lax.broadcasted_iota(I32, (tm, tm), 1)
    before = jnp.where(n_io < m_io, 1.0, 0.0).astype(BF16)
    cexcl = _dot(member.astype(BF16), before)
    cnt = jnp.sum(member, axis=1, keepdims=True)
    run_len = jnp.floor((cnt + (RUN_ALIGN - 1)) * (1.0 / RUN_ALIGN)) * RUN_ALIGN
    cnt_ref[0] = jnp.broadcast_to(run_len, cnt_ref.shape[1:]).astype(I32)

    ee_r = lax.broadcasted_iota(I32, (N_EXPERTS, N_EXPERTS), 0)
    ee_c = lax.broadcasted_iota(I32, (N_EXPERTS, N_EXPERTS), 1)
    lower = jnp.where(ee_c < ee_r, 1.0, 0.0).astype(BF16)
    c_h, c_l = _split2(jnp.broadcast_to(run_len, (N_EXPERTS, LANES)))
    run_start = (_dot(lower, c_h) + _dot(lower, c_l))[:, 0:1]
    where_in_tile = cexcl + run_start
    pos = [jnp.sum(jnp.where(e_io == ti, where_in_tile, 0.0), axis=0, keepdims=True) for ti in top_i]

    j_io = lax.broadcasted_iota(I32, (_sorted_rows(tm), tm), 0).astype(F32)
    hit = j_io == pos[0]
    for pk in pos[1:]:
        hit = jnp.logical_or(hit, j_io == pk)
    onehot = jnp.where(hit, 1.0, 0.0).astype(BF16)
    xs_ref[...] = _dot(onehot, u.astype(BF16))

    r_io = lax.broadcasted_iota(I32, (LANES, tm), 0)
    gfull = jnp.zeros((LANES, tm), F32)
    pfull = jnp.zeros((LANES, tm), F32)
    for kk_ in range(TOP_K):
        gfull = jnp.where(r_io == kk_, jnp.broadcast_to(gates[kk_], (LANES, tm)), gfull)
        pfull = jnp.where(r_io == kk_, jnp.broadcast_to(pos[kk_], (LANES, tm)), pfull)
    gate_ref[...] = gfull.T
    pos_ref[...] = pfull.T


def _mix_router(ohg, osb, x2, w_out_bf, g_ffn, w_router_t, b_router_col, tm=512):
    n, d = x2.shape
    tm = min(tm, n)
    row = lambda i: (i, 0)
    fix = lambda i: (0, 0)
    return pl.pallas_call(
        functools.partial(_mix_router_kernel, tm=tm),
        grid=(n // tm,),
        in_specs=[
            pl.BlockSpec((tm, HG_WIDTH), row),
            pl.BlockSpec((tm, SB_WIDTH), row),
            pl.BlockSpec((tm, d), row),
            pl.BlockSpec(w_out_bf.shape, fix),
            pl.BlockSpec((1, d), fix),
            pl.BlockSpec(w_router_t.shape, fix),
            pl.BlockSpec(b_router_col.shape, fix),
        ],
        out_specs=[
            pl.BlockSpec((tm, d), row),
            pl.BlockSpec((_sorted_rows(tm), d), row),
            pl.BlockSpec((tm, LANES), row),
            pl.BlockSpec((tm, LANES), row),
            pl.BlockSpec((1, N_EXPERTS, LANES), lambda i: (i, 0, 0)),
        ],
        out_shape=[
            jax.ShapeDtypeStruct((n, d), F32),
            jax.ShapeDtypeStruct((n // tm * _sorted_rows(tm), d), F32),
            jax.ShapeDtypeStruct((n, LANES), F32),
            jax.ShapeDtypeStruct((n, LANES), F32),
            jax.ShapeDtypeStruct((n // tm, N_EXPERTS, LANES), I32),
        ],
        compiler_params=_cparams(1, V7X_VMEM_LIMIT_BYTES),
        name="mix_router",
    )(ohg, osb, x2, w_out_bf, g_ffn, w_router_t, b_router_col)


def _run_sizes(limit):
    sizes = []
    s = RUN_ALIGN
    while s <= limit:
        sizes.append(s)
        s *= 2
    return sizes[::-1]


def _row_run(ref, row, size):
    return ref.at[pl.ds(pl.multiple_of(row, RUN_ALIGN), size)]


def _for_each_piece(m, limit, fn):
    for size in _run_sizes(limit):
        @pl.when((m & size) != 0)
        def _(size=size):
            fn(m & ~(2 * size - 1), size)


def _run_copies(mt_ref, t, tm, make_copy):
    for e in range(N_EXPERTS):
        _for_each_piece(mt_ref[t, e], tm,
                        lambda done, size, e=e: make_copy(e, done, size).start(priority=e % 2))


MOE_PAIR = 2


def _moe_kernel(be_ref, sb_ref, ns_ref, tlo_ref, thi_ref, nv_ref, mt_ref, ot_ref, dt_ref,
                xs_ref, *refs, tile_rows):
    n_w = 6 * MOE_PAIR
    w_refs, y_ref = refs[:n_w], refs[n_w]
    bufs, sem_ref = refs[n_w + 1:n_w + 1 + MOE_PAIR], refs[n_w + 1 + MOE_PAIR]
    i = pl.program_id(0)
    n_used_steps = ns_ref[0]

    def gather(b, half):
        e = be_ref[b]
        first_slot = b * MOE_BLOCK

        def per_tile(t, carry):
            run0 = dt_ref[t, e]
            lo = jnp.maximum(run0, first_slot)
            hi = jnp.minimum(run0 + mt_ref[t, e], first_slot + MOE_BLOCK)
            src = t * tile_rows + ot_ref[t, e] + (lo - run0)
            dst = lo - first_slot
            _for_each_piece(jnp.maximum(hi - lo, 0), MOE_BLOCK, lambda done, size: pltpu.make_async_copy(
                _row_run(xs_ref, src + done, size), _row_run(bufs[half], dst + done, size),
                sem_ref.at[half]).start())
            return carry

        lax.fori_loop(tlo_ref[b], thi_ref[b] + 1, per_tile, 0)

    def wait(b, half):
        _for_each_piece(nv_ref[b], MOE_BLOCK, lambda done, size: pltpu.make_async_copy(
            _row_run(xs_ref, 0, size), _row_run(bufs[half], 0, size), sem_ref.at[half]).wait())

    @pl.when(i == 0)
    def _():
        for half in range(MOE_PAIR):
            bufs[half][...] = jnp.zeros_like(bufs[half])
            gather(half, half)

    @pl.when(i < n_used_steps)
    def _():
        xs_now = []
        for half in range(MOE_PAIR):
            wait(MOE_PAIR * i + half, half)
            xs_now.append(bufs[half][...].astype(BF16))

        @pl.when(i + 1 < n_used_steps)
        def _():
            for half in range(MOE_PAIR):
                gather(MOE_PAIR * (i + 1) + half, half)

        for half in range(MOE_PAIR):
            wg_ref, bg_ref, wu_ref, bu_ref, wd_ref, bd_ref = w_refs[6 * half:6 * half + 6]
            x = xs_now[half]
            hg = _dot(x, wg_ref[0]) + bg_ref[0]
            hu = _dot(x, wu_ref[0]) + bu_ref[0]
            hg = jnp.minimum(hg, SWIGLU_LIMIT)
            hu = jnp.clip(hu, -SWIGLU_LIMIT, SWIGLU_LIMIT)
            glu = hg * _sigmoid(SWIGLU_ALPHA * hg)
            act = ((hu + 1.0) * glu).astype(BF16)
            y_ref[half * MOE_BLOCK:(half + 1) * MOE_BLOCK, :] = _dot(act, wd_ref[0]) + bd_ref[0]

    @pl.when(i >= n_used_steps)
    def _():
        y_ref[...] = jnp.zeros_like(y_ref)


def _moe(sched, xs, n_slots, tile_rows, wg, bg, wu, bu, wd, bd):
    d, f = wg.shape[1], wg.shape[2]
    step_rows = MOE_PAIR * MOE_BLOCK
    n_steps = n_slots // step_rows
    w_specs, w_args = [], []
    for half in range(MOE_PAIR):
        wmap = lambda i, be, sb, *_, half=half: (be[MOE_PAIR * sb[i] + half], 0, 0)
        w_specs += [pl.BlockSpec((1, d, f), wmap), pl.BlockSpec((1, 1, f), wmap),
                    pl.BlockSpec((1, d, f), wmap), pl.BlockSpec((1, 1, f), wmap),
                    pl.BlockSpec((1, f, d), wmap), pl.BlockSpec((1, 1, d), wmap)]
        w_args += [wg, bg, wu, bu, wd, bd]
    grid_spec = pltpu.PrefetchScalarGridSpec(
        num_scalar_prefetch=len(sched),
        grid=(n_steps,),
        in_specs=[pl.BlockSpec(memory_space=pl.ANY)] + w_specs,
        out_specs=pl.BlockSpec((step_rows, d), lambda i, *_: (i, 0)),
        scratch_shapes=[pltpu.VMEM((MOE_BLOCK, d), F32)] * MOE_PAIR
        + [pltpu.SemaphoreType.DMA((MOE_PAIR,))],
    )
    return pl.pallas_call(
        functools.partial(_moe_kernel, tile_rows=tile_rows),
        grid_spec=grid_spec,
        out_shape=jax.ShapeDtypeStruct((n_slots, d), F32),
        compiler_params=_cparams(1, V7X_VMEM_LIMIT_BYTES),
        name="moe",
    )(*sched, xs, *w_args)


def _combine_kernel(mt_ref, ot_ref, dt_ref, tot_ref, y_ref, pos_ref, h_ref, gate_ref, gain_ref, o_ref,
                    ybuf_ref, sem_ref, *, tm):
    t = pl.program_id(0)
    n_tiles = pl.num_programs(0)
    srows = _sorted_rows(tm)
    slot = t % 2

    def fetch(tt, s):
        _run_copies(mt_ref, tt, tm, lambda e, done, size: pltpu.make_async_copy(
            _row_run(y_ref, dt_ref[tt, e] + done, size),
            _row_run(ybuf_ref, s * srows + ot_ref[tt, e] + done, size), sem_ref.at[s]))

    @pl.when(t == 0)
    def _():
        ybuf_ref[...] = jnp.zeros_like(ybuf_ref)
        fetch(0, 0)

    @pl.when(t + 1 < n_tiles)
    def _():
        fetch(t + 1, 1 - slot)

    _for_each_piece(tot_ref[t], srows, lambda done, size: pltpu.make_async_copy(
        _row_run(y_ref, 0, size), _row_run(ybuf_ref, 0, size), sem_ref.at[slot]).wait())

    gate = gate_ref[...]
    pos = pos_ref[...]
    j_io = lax.broadcasted_iota(I32, (tm, srows), 1).astype(F32)
    wmat = jnp.zeros((tm, srows), F32)
    for k in range(TOP_K):
        wmat = jnp.where(j_io == pos[:, k:k + 1], gate[:, k:k + 1], wmat)
    ys = ybuf_ref[pl.ds(pl.multiple_of(slot * srows, RUN_ALIGN), srows), :].astype(BF16)
    acc = h_ref[...] + _dot(wmat.astype(BF16), ys)
    var = jnp.mean(acc * acc, axis=-1, keepdims=True)
    o_ref[...] = (acc * lax.rsqrt(var + NORM_EPS) * gain_ref[...]).astype(o_ref.dtype)


def _combine(tabs, y_disp, pos_c, h, gates, gain, tm):
    n, d = h.shape
    row = lambda t, *_: (t, 0)
    fix = lambda t, *_: (0, 0)
    grid_spec = pltpu.PrefetchScalarGridSpec(
        num_scalar_prefetch=len(tabs),
        grid=(n // tm,),
        in_specs=[
            pl.BlockSpec(memory_space=pl.ANY),
            pl.BlockSpec((tm, LANES), row),
            pl.BlockSpec((tm, d), row),
            pl.BlockSpec((tm, LANES), row),
            pl.BlockSpec((1, d), fix),
        ],
        out_specs=pl.BlockSpec((tm, d), row),
        scratch_shapes=[
            pltpu.VMEM((2 * _sorted_rows(tm), d), F32),
            pltpu.SemaphoreType.DMA((2,)),
        ],
    )
    return pl.pallas_call(
        functools.partial(_combine_kernel, tm=tm),
        grid_spec=grid_spec,
        out_shape=jax.ShapeDtypeStruct((n, d), F32),
        compiler_params=_cparams(1, V7X_VMEM_LIMIT_BYTES),
        name="combine",
    )(*tabs, y_disp, pos_c, h, gates, gain)


def _routing_tables(tile_cnt, n_pairs):
    counts = jnp.sum(tile_cnt, axis=0)
    padded = ((counts + MOE_BLOCK - 1) // MOE_BLOCK) * MOE_BLOCK
    cum_pad = jnp.cumsum(padded)
    start_pad = (cum_pad - padded).astype(I32)
    step_rows = MOE_PAIR * MOE_BLOCK
    n_slots = n_pairs + tile_cnt.shape[0] * N_EXPERTS * RUN_ALIGN + N_EXPERTS * MOE_BLOCK
    n_slots = ((n_slots + step_rows - 1) // step_rows) * step_rows
    nb = n_slots // MOE_BLOCK
    block_start = jnp.arange(nb, dtype=I32) * MOE_BLOCK
    block_e = jnp.minimum(jnp.sum(cum_pad[None, :] <= block_start[:, None], axis=1), N_EXPERTS - 1)
    block_e = block_e.astype(I32)
    n_used = (cum_pad[-1] // MOE_BLOCK).astype(I32)
    n_steps_used = (n_used + MOE_PAIR - 1) // MOE_PAIR
    step_blk = jnp.minimum(jnp.arange(nb // MOE_PAIR, dtype=I32), n_steps_used - 1)
    before_tile = jnp.cumsum(tile_cnt, axis=0) - tile_cnt
    run_dst = (start_pad[None, :] + before_tile).astype(I32)
    run_src = (jnp.cumsum(tile_cnt, axis=1) - tile_cnt).astype(I32)
    dst_b = run_dst[:, block_e]
    end_b = dst_b + tile_cnt[:, block_e]
    tile_lo = jnp.sum(end_b <= block_start[None, :], axis=0).astype(I32)
    tile_hi = (jnp.sum(dst_b < block_start[None, :] + MOE_BLOCK, axis=0) - 1).astype(I32)
    used = jnp.arange(nb, dtype=I32) < n_used
    n_valid = jnp.clip((start_pad + counts)[block_e] - block_start, 0, MOE_BLOCK)
    n_valid = jnp.where(used, n_valid, 0).astype(I32)
    tile_hi = jnp.where(used, tile_hi, tile_lo - 1)
    run_tabs = (tile_cnt.astype(I32), run_src, run_dst)
    sched = (block_e, step_blk, n_steps_used.reshape(1).astype(I32), tile_lo, tile_hi, n_valid) + run_tabs
    tile_total = jnp.sum(tile_cnt, axis=1).astype(I32)
    return run_tabs + (tile_total,), sched, n_slots


def kernel(x, w_in, w_out, hg_lb_logits, hg_norm_gain, sb_norm_gain, norm_mix_gain, norm_ffn_gain,
           w_router, b_router, w_gate, b_gate, w_up, b_up, w_down, b_down, norm_final_gain):
    b, s, d = x.shape
    n = b * s
    f = w_gate.shape[-1]
    assert w_in.shape[0] == 1 and hg_lb_logits.shape[0] == 2, "single-layer trunk only"
    x2 = x.reshape(n, d).astype(F32)
    r3 = lambda a: a.reshape(b, s, a.shape[-1])

    q, k, lf, v, g, sq, sk, sv = _in_proj(
        x2, norm_mix_gain[0].reshape(1, d), w_in[0].astype(BF16), hg_lb_logits.astype(F32))
    o_hg, (wg_bf, wu_bf, wd_bf) = _hgrn2(
        r3(q), r3(k), r3(lf), r3(v), r3(g), hg_norm_gain[0].reshape(1, HG_WIDTH),
        cast_f32=(w_gate[0], w_up[0], w_down[0]))
    o_sb = _stickbreak(r3(sq), r3(sk), r3(sv), sb_norm_gain[0].reshape(1, SB_WIDTH))
    h_mid, xs, gates, pos_c, cnt = _mix_router(
        o_hg.reshape(n, HG_WIDTH), o_sb.reshape(n, SB_WIDTH), x2, w_out[0].astype(BF16),
        norm_ffn_gain[0].reshape(1, d), w_router[0].T.astype(F32),
        b_router[0].reshape(N_EXPERTS, 1).astype(F32))
    tile_cnt = cnt[:, :, 0]
    run_tabs, sched, n_slots = _routing_tables(tile_cnt, n * TOP_K)
    tm = n // tile_cnt.shape[0]
    y_disp = _moe(sched, xs, n_slots, _sorted_rows(tm),
                  wg_bf, b_gate[0].reshape(N_EXPERTS, 1, f),
                  wu_bf, b_up[0].reshape(N_EXPERTS, 1, f),
                  wd_bf, b_down[0].reshape(N_EXPERTS, 1, d))
    out = _combine(run_tabs, y_disp, pos_c, h_mid, gates, norm_final_gain.reshape(1, d), tm)
    return out.reshape(b, s, d).astype(x.dtype)
```
